```python
import math
import jax, jax.numpy as jnp
from jax import lax
import numpy as np

D_MODEL = 2048
BATCH = 2
SEQ = 8192
DEPTH = 2

MEM_LEN = 256
CONV_WIDTH = 1024
CONV_K = 3
DIFF_HEADS = 8
DIFF_D = 64
DIFF_V = 2 * DIFF_D
DIFF_QK = DIFF_HEADS * 2 * DIFF_D
DIFF_VW = DIFF_HEADS * DIFF_V
X_HEADS = 4
X_HEAD_DIM = 256
X_WIDTH = X_HEADS * X_HEAD_DIM
N_BRANCH = 3
ROPE_THETA = 500000.0
ROT_FRAC = 4
D_FF = ((8 * D_MODEL // 3 + 255) // 256) * 256
Q_BLOCK = 128
EPS = 1e-6
IN_SIZES = (CONV_WIDTH, CONV_WIDTH, CONV_WIDTH, DIFF_QK, DIFF_QK, DIFF_VW, X_WIDTH, N_BRANCH * D_MODEL)
N_IN = CONV_WIDTH * 3 + DIFF_QK * 2 + DIFF_VW + X_WIDTH + N_BRANCH * D_MODEL

kernel_name = 'hybrid_conv_diffattn_memxattn_swiglu'


def rms_norm(t, g):
    t32 = t.astype(jnp.float32)
    out = t32 * lax.rsqrt(jnp.mean(t32 * t32, axis=-1, keepdims=True) + EPS)
    return (out * g.astype(jnp.float32)).astype(t.dtype)


def partial_rotary(t, positions):
    d = t.shape[-1]
    rot = d // ROT_FRAC
    half = rot // 2
    inv_freq = ROPE_THETA ** (-jnp.arange(half, dtype=jnp.float32) / half)
    ang = positions.astype(jnp.float32)[..., None] * inv_freq
    ang = ang[:, :, None, None, :]
    cos, sin = jnp.cos(ang), jnp.sin(ang)
    t32 = t.astype(jnp.float32)
    x1 = t32[..., :half]
    x2 = t32[..., half:rot]
    out = jnp.concatenate([x1 * cos - x2 * sin, x2 * cos + x1 * sin, t32[..., rot:]], axis=-1)
    return out.astype(t.dtype)


def causal_short_conv(u, w):
    S = u.shape[1]
    up = jnp.pad(u, ((0, 0), (CONV_K - 1, 0), (0, 0)))
    y = up[:, 0:S] * w[0]
    for j in range(1, CONV_K):
        y = y + up[:, j:j + S] * w[j]
    return y


def diff_attention(q, k, v, lam):
    B, S, H, _, d = q.shape
    nb = S // Q_BLOCK
    k32 = k.astype(jnp.float32)
    v32 = v.astype(jnp.float32)
    qb = (q.astype(jnp.float32) * (d ** -0.5)).reshape(B, nb, Q_BLOCK, H, 2, d).transpose(1, 0, 2, 3, 4, 5)
    starts = jnp.arange(nb, dtype=jnp.int32) * Q_BLOCK
    k_idx = jnp.arange(S, dtype=jnp.int32)

    def block(args):
        q_blk, start = args
        s = jnp.einsum('bqhcd,bkhcd->bhcqk', q_blk, k32)
        causal = (start + jnp.arange(Q_BLOCK, dtype=jnp.int32))[:, None] >= k_idx[None, :]
        s = jnp.where(causal, s, -jnp.inf)
        p = jax.nn.softmax(s, axis=-1)
        a = p[:, :, 0] - lam * p[:, :, 1]
        return jnp.einsum('bhqk,bkhe->bqhe', a, v32)

    o = lax.map(block, (qb, starts))
    return o.transpose(1, 0, 2, 3, 4).reshape(B, S, H, 2 * d)


def setup_inputs(seed: int = 0) -> dict:
    key = jax.random.key(seed)
    ks = jax.random.split(key, 24)
    f32 = jnp.float32

    def nrm(k, shape, fan_in):
        return jax.random.normal(k, shape, f32) * (fan_in ** -0.5)

    def gain(k, shape):
        return 1.0 + 0.02 * jax.random.normal(k, shape, f32)

    return {
        'x': jax.random.normal(ks[0], (BATCH, SEQ, D_MODEL), f32),
        'mem': jax.random.normal(ks[1], (BATCH, MEM_LEN, D_MODEL), f32),
        'positions': jnp.broadcast_to(jnp.arange(SEQ, dtype=jnp.int32), (BATCH, SEQ)),
        'g_mix': gain(ks[2], (DEPTH, D_MODEL)),
        'w_in': nrm(ks[3], (DEPTH, D_MODEL, N_IN), D_MODEL),
        'conv_w': nrm(ks[4], (DEPTH, CONV_K, CONV_WIDTH), CONV_K),
        'w_conv_out': nrm(ks[5], (DEPTH, CONV_WIDTH, D_MODEL), CONV_WIDTH),
        'g_q_diff': gain(ks[6], (DEPTH, DIFF_D)),
        'g_k_diff': gain(ks[7], (DEPTH, DIFF_D)),
        'lambda_vecs': 0.1 * jax.random.normal(ks[8], (DEPTH, 4, DIFF_D), f32),
        'g_subln': gain(ks[9], (DEPTH, DIFF_V)),
        'w_diff_out': nrm(ks[10], (DEPTH, DIFF_VW, D_MODEL), DIFF_VW),
        'g_mem': gain(ks[11], (DEPTH, D_MODEL)),
        'w_mem_kv': nrm(ks[12], (DEPTH, D_MODEL, 2 * X_WIDTH), D_MODEL),
        'g_q_x': gain(ks[13], (DEPTH, X_HEAD_DIM)),
        'g_k_x': gain(ks[14], (DEPTH, X_HEAD_DIM)),
        'w_x_out': nrm(ks[15], (DEPTH, X_WIDTH, D_MODEL), X_WIDTH),
        'w_o': nrm(ks[16], (DEPTH, D_MODEL, D_MODEL), D_MODEL),
        'g_ffn': gain(ks[17], (DEPTH, D_MODEL)),
        'w_gate_up': nrm(ks[18], (DEPTH, D_MODEL, 2 * D_FF), D_MODEL),
        'w_down': nrm(ks[19], (DEPTH, D_FF, D_MODEL), D_FF),
    }


def reference(x, mem, positions, g_mix, w_in, conv_w, w_conv_out, g_q_diff, g_k_diff,
              lambda_vecs, g_subln, w_diff_out, g_mem, w_mem_kv, g_q_x, g_k_x, w_x_out,
              w_o, g_ffn, w_gate_up, w_down):
    B, S, D = x.shape
    M = mem.shape[1]
    split_idx = [int(v) for v in np.cumsum(IN_SIZES)[:-1]]
    for l in range(DEPTH):
        lam_init = 0.8 - 0.6 * math.exp(-0.3 * l)
        h = rms_norm(x, g_mix[l])
        proj = jnp.einsum('bsd,dn->bsn', h, w_in[l])
        cb, cc, cx, dq, dk, dv, xq, gates = jnp.split(proj, split_idx, axis=-1)

        y_a = jnp.einsum('bsc,cd->bsd', cb * causal_short_conv(cc * cx, conv_w[l]), w_conv_out[l])

        dq = partial_rotary(rms_norm(dq.reshape(B, S, DIFF_HEADS, 2, DIFF_D), g_q_diff[l]), positions)
        dk = partial_rotary(rms_norm(dk.reshape(B, S, DIFF_HEADS, 2, DIFF_D), g_k_diff[l]), positions)
        dv = dv.reshape(B, S, DIFF_HEADS, DIFF_V)
        lv = lambda_vecs[l].astype(jnp.float32)
        lam = jnp.exp(jnp.sum(lv[0] * lv[1])) - jnp.exp(jnp.sum(lv[2] * lv[3])) + lam_init
        o_b = rms_norm(diff_attention(dq, dk, dv, lam), g_subln[l]) * (1.0 - lam_init)
        y_b = jnp.einsum('bsn,nd->bsd', o_b.reshape(B, S, DIFF_VW).astype(x.dtype), w_diff_out[l])

        kv = jnp.einsum('bmd,dn->bmn', rms_norm(mem, g_mem[l]), w_mem_kv[l]).reshape(B, M, 2, X_HEADS, X_HEAD_DIM)
        mk = rms_norm(kv[:, :, 0], g_k_x[l]).astype(jnp.float32)
        mv = kv[:, :, 1].astype(jnp.float32)
        xq = rms_norm(xq.reshape(B, S, X_HEADS, X_HEAD_DIM), g_q_x[l]).astype(jnp.float32)
        s_x = jnp.einsum('bshd,bmhd->bhsm', xq, mk) * (X_HEAD_DIM ** -0.5)
        p_x = jax.nn.softmax(s_x, axis=-1)
        o_c = jnp.einsum('bhsm,bmhd->bshd', p_x, mv).reshape(B, S, X_WIDTH).astype(x.dtype)
        y_c = jnp.einsum('bsn,nd->bsd', o_c, w_x_out[l])

        g = jax.nn.sigmoid(gates.astype(jnp.float32)).reshape(B, S, N_BRANCH, D).astype(x.dtype)
        merged = g[:, :, 0] * y_a + g[:, :, 1] * y_b + g[:, :, 2] * y_c
        x = x + jnp.einsum('bsd,de->bse', merged, w_o[l])

        h2 = rms_norm(x, g_ffn[l])
        gu = jnp.einsum('bsd,df->bsf', h2, w_gate_up[l])
        a, b = jnp.split(gu, 2, axis=-1)
        x = x + jnp.einsum('bsf,fd->bsd', jax.nn.silu(a) * b, w_down[l])
    return x
```

```python
import functools
import math

import jax
import jax.numpy as jnp
from jax import lax
from jax.experimental import pallas as pl
from jax.experimental.pallas import tpu as pltpu

EPS = 1e-6
CONV_WIDTH = 1024
CONV_K = 3
DIFF_HEADS = 8
DIFF_D = 64
DIFF_V = 2 * DIFF_D
DIFF_QK = DIFF_HEADS * 2 * DIFF_D
DIFF_VW = DIFF_HEADS * DIFF_V
X_HEADS = 4
X_HEAD_DIM = 256
X_WIDTH = X_HEADS * X_HEAD_DIM
N_BRANCH = 3
ROPE_THETA = 500000.0
ROT_FRAC = 4
ROT = DIFF_D // ROT_FRAC
ROT_HALF = ROT // 2

LANES = 128
SUBLANES = 8
VMEM_LIMIT_BYTES = 56 * 1024 * 1024

F32 = jnp.float32
BF16 = jnp.bfloat16


def _compiler_params(semantics):
    return pltpu.CompilerParams(dimension_semantics=semantics, vmem_limit_bytes=VMEM_LIMIT_BYTES)


def _rms_rows(t, g):
    ms = jnp.mean(t * t, axis=-1, keepdims=True)
    return t * lax.rsqrt(ms + EPS) * g


def _norm_matmul_kernel(x_ref, g_ref, w_ref, o_ref, h_ref):
    @pl.when(pl.program_id(1) == 0)
    def _():
        h_ref[...] = _rms_rows(x_ref[...], g_ref[...]).astype(h_ref.dtype)

    o_ref[...] = jnp.dot(h_ref[...], w_ref[...], preferred_element_type=F32).astype(o_ref.dtype)


def norm_matmul(x, g, w, *, tm, tn, out_dtype):
    T, D = x.shape
    N = w.shape[1]
    assert T % tm == 0 and N % tn == 0
    return pl.pallas_call(
        _norm_matmul_kernel,
        grid=(T // tm, N // tn),
        in_specs=[
            pl.BlockSpec((tm, D), lambda i, j: (i, 0)),
            pl.BlockSpec((1, D), lambda i, j: (0, 0)),
            pl.BlockSpec((D, tn), lambda i, j: (0, j)),
        ],
        out_specs=pl.BlockSpec((tm, tn), lambda i, j: (i, j)),
        out_shape=jax.ShapeDtypeStruct((T, N), out_dtype),
        scratch_shapes=[pltpu.VMEM((tm, D), BF16)],
        compiler_params=_compiler_params(("parallel", "arbitrary")),
        name="norm_matmul",
    )(x, g.reshape(1, D), w)


def _group_mean_sq(t, gsum):
    sq = t * t
    hi = sq.astype(BF16)
    lo = (sq - hi.astype(F32)).astype(BF16)
    s = jnp.dot(hi, gsum, preferred_element_type=F32) + jnp.dot(lo, gsum, preferred_element_type=F32)
    return s * (1.0 / DIFF_D)


def _qk_prep_kernel(pos_ref, invf_ref, gq_ref, gk_ref, q_ref, k_ref, qo_ref, ko_ref):
    lane = lax.broadcasted_iota(jnp.int32, (1, LANES), 1)
    d = lane & (DIFF_D - 1)
    ang = pos_ref[...].astype(F32) * invf_ref[...]
    cos = jnp.cos(ang)
    sin = jnp.sin(ang)
    c_keep = jnp.where(d < ROT, cos, 1.0)
    s_from_hi = jnp.where(d < ROT_HALF, -sin, 0.0)
    s_from_lo = jnp.where((d >= ROT_HALF) & (d < ROT), sin, 0.0)

    row = lax.broadcasted_iota(jnp.int32, (LANES, LANES), 0)
    col = lax.broadcasted_iota(jnp.int32, (LANES, LANES), 1)
    gsum = jnp.where((row < DIFF_D) == (col < DIFF_D), 1.0, 0.0).astype(BF16)

    def prep(src_ref, g_ref, dst_ref, scale):
        g = g_ref[...]
        for c in range(DIFF_QK // LANES):
            sl = slice(c * LANES, (c + 1) * LANES)
            t = src_ref[:, sl].astype(F32)
            tn = t * lax.rsqrt(_group_mean_sq(t, gsum) + EPS) * g
            up = pltpu.roll(tn, LANES - ROT_HALF, axis=1)
            dn = pltpu.roll(tn, ROT_HALF, axis=1)
            r = tn * c_keep + up * s_from_hi + dn * s_from_lo
            if scale != 1.0:
                r = r * scale
            dst_ref[:, sl] = r.astype(dst_ref.dtype)

    prep(q_ref, gq_ref, qo_ref, DIFF_D ** -0.5)
    prep(k_ref, gk_ref, ko_ref, 1.0)


def qk_prep(proj, pos, inv_freq_lanes, g_q, g_k, *, q_col, k_col, tm):
    T = proj.shape[0]
    assert T % tm == 0 and q_col % DIFF_QK == 0 and k_col % DIFF_QK == 0
    gq = jnp.tile(g_q, LANES // DIFF_D).reshape(1, LANES)
    gk = jnp.tile(g_k, LANES // DIFF_D).reshape(1, LANES)
    out = jax.ShapeDtypeStruct((T, DIFF_QK), BF16)
    return pl.pallas_call(
        _qk_prep_kernel,
        grid=(T // tm,),
        in_specs=[
            pl.BlockSpec((tm, 1), lambda i: (i, 0)),
            pl.BlockSpec((1, LANES), lambda i: (0, 0)),
            pl.BlockSpec((1, LANES), lambda i: (0, 0)),
            pl.BlockSpec((1, LANES), lambda i: (0, 0)),
            pl.BlockSpec((tm, DIFF_QK), lambda i: (i, q_col // DIFF_QK)),
            pl.BlockSpec((tm, DIFF_QK), lambda i: (i, k_col // DIFF_QK)),
        ],
        out_specs=[pl.BlockSpec((tm, DIFF_QK), lambda i: (i, 0))] * 2,
        out_shape=[out, out],
        compiler_params=_compiler_params(("parallel",)),
        name="qk_prep",
    )(pos, inv_freq_lanes, gq, gk, proj, proj)


def _diff_attn_kernel(lam_ref, g_ref, q_ref, k_ref, v_ref, o_ref, m_ref, l_ref, acc_ref,
                      *, tq, tk, lam_init):
    i = pl.program_id(2)
    j = pl.program_id(3)

    @pl.when(j == 0)
    def _():
        m_ref[...] = jnp.full(m_ref.shape, -jnp.inf, F32)
        l_ref[...] = jnp.zeros(l_ref.shape, F32)
        acc_ref[...] = jnp.zeros(acc_ref.shape, F32)

    @pl.when(j * tk <= i * tq + (tq - 1))
    def _():
        q = q_ref[...]
        k = k_ref[...]
        v = v_ref[...]
        lane = lax.broadcasted_iota(jnp.int32, (1, LANES), 1)
        rows = i * tq + lax.broadcasted_iota(jnp.int32, (tq, tk), 0)
        cols = j * tk + lax.broadcasted_iota(jnp.int32, (tq, tk), 1)
        causal = rows >= cols
        zero = jnp.zeros_like(q)
        for c in range(2):
            in_comp = (lane >= c * DIFF_D) & (lane < (c + 1) * DIFF_D)
            qc = jnp.where(in_comp, q, zero)
            s = lax.dot_general(qc, k, (((1,), (1,)), ((), ())), preferred_element_type=F32)
            s = jnp.where(causal, s, -jnp.inf)
            m_prev = m_ref[c]
            m_new = jnp.maximum(m_prev, jnp.max(s, axis=-1, keepdims=True))
            alpha = jnp.exp(m_prev - m_new)
            p = jnp.exp(s - m_new)
            l_ref[c] = alpha * l_ref[c] + jnp.sum(p, axis=-1, keepdims=True)
            acc_ref[c] = alpha * acc_ref[c] + jnp.dot(p.astype(v.dtype), v, preferred_element_type=F32)
            m_ref[c] = m_new

    @pl.when(j == pl.num_programs(3) - 1)
    def _():
        lv = lam_ref[...]
        e1 = jnp.exp(jnp.sum(lv[0:1, :] * lv[1:2, :], axis=-1, keepdims=True))
        e2 = jnp.exp(jnp.sum(lv[2:3, :] * lv[3:4, :], axis=-1, keepdims=True))
        lam = e1 - e2 + lam_init
        o = acc_ref[0] / l_ref[0] - lam * (acc_ref[1] / l_ref[1])
        o_ref[...] = (_rms_rows(o, g_ref[...]) * (1.0 - lam_init)).astype(o_ref.dtype)


def diff_attn(q, k, proj, lambda_vecs, g_subln, *, batch, v_col, tq, tk, lam_init):
    T = q.shape[0]
    S = T // batch
    assert S % tq == 0 and S % tk == 0 and v_col % DIFF_V == 0
    nq, nk = S // tq, S // tk

    def q_map(b, h, i, j):
        return (b * nq + i, h)

    def kv_block(b, i, j):
        return b * nk + jnp.minimum(j, (i * tq + (tq - 1)) // tk)

    return pl.pallas_call(
        functools.partial(_diff_attn_kernel, tq=tq, tk=tk, lam_init=lam_init),
        grid=(batch, DIFF_HEADS, nq, nk),
        in_specs=[
            pl.BlockSpec((4, DIFF_D), lambda b, h, i, j: (0, 0)),
            pl.BlockSpec((1, DIFF_V), lambda b, h, i, j: (0, 0)),
            pl.BlockSpec((tq, DIFF_V), q_map),
            pl.BlockSpec((tk, DIFF_V), lambda b, h, i, j: (kv_block(b, i, j), h)),
            pl.BlockSpec((tk, DIFF_V), lambda b, h, i, j: (kv_block(b, i, j), v_col // DIFF_V + h)),
        ],
        out_specs=pl.BlockSpec((tq, DIFF_V), q_map),
        out_shape=jax.ShapeDtypeStruct((T, DIFF_VW), BF16),
        scratch_shapes=[
            pltpu.VMEM((2, tq, 1), F32),
            pltpu.VMEM((2, tq, 1), F32),
            pltpu.VMEM((2, tq, DIFF_V), F32),
        ],
        compiler_params=_compiler_params(("parallel", "parallel", "parallel", "arbitrary")),
        name="diff_attn",
    )(lambda_vecs, g_subln.reshape(1, DIFF_V), q, k, proj)


def _mem_prep_kernel(kv_ref, g_ref, mk_ref, mv_ref):
    g = g_ref[...]
    for h in range(X_HEADS):
        sl = slice(h * X_HEAD_DIM, (h + 1) * X_HEAD_DIM)
        kh = _rms_rows(kv_ref[:, sl], g) * (X_HEAD_DIM ** -0.5)
        mk_ref[:, sl] = kh.astype(mk_ref.dtype)
    mv_ref[...] = kv_ref[:, X_WIDTH:].astype(mv_ref.dtype)


def mem_prep(kv, g_k_x):
    R = kv.shape[0]
    out = jax.ShapeDtypeStruct((R, X_WIDTH), BF16)
    return pl.pallas_call(
        _mem_prep_kernel,
        grid=(1,),
        in_specs=[
            pl.BlockSpec((R, 2 * X_WIDTH), lambda i: (0, 0)),
            pl.BlockSpec((1, X_HEAD_DIM), lambda i: (0, 0)),
        ],
        out_specs=[pl.BlockSpec((R, X_WIDTH), lambda i: (0, 0))] * 2,
        out_shape=[out, out],
        compiler_params=_compiler_params(("arbitrary",)),
        name="mem_prep",
    )(kv, g_k_x.reshape(1, X_HEAD_DIM))


MERGE_CHUNK = 512


def _merge_kernel(gates_ref, cb_ref, cc_ref, cx_ref, cch_ref, cxh_ref, xq_ref, ob_ref, x_ref,
                  convw_ref, gqx_ref, mk_ref, mv_ref, wa_ref, wb_ref, wc_ref, wo_ref,
                  o_ref, ua_ref, oc_ref, mg_ref, *, tiles_per_seq):
    tm = x_ref.shape[0]
    d_model = x_ref.shape[1]
    first = pl.program_id(0) % tiles_per_seq == 0

    z = cc_ref[...].astype(F32) * cx_ref[...].astype(F32)
    zp = cch_ref[...].astype(F32) * cxh_ref[...].astype(F32)
    zp = jnp.where(first, 0.0, zp)
    w = convw_ref[...]
    row8 = lax.broadcasted_iota(jnp.int32, (SUBLANES, 1), 0)
    y = z * w[CONV_K - 1:CONV_K, :]
    for back in range(1, CONV_K):
        zs = pltpu.roll(z, back, axis=0)
        ps = pltpu.roll(zp, back, axis=0)
        head = jnp.where(row8 < back, ps, zs[:SUBLANES])
        zs = jnp.concatenate([head, zs[SUBLANES:]], axis=0)
        y = y + zs * w[CONV_K - 1 - back:CONV_K - back, :]
    ua_ref[...] = (cb_ref[...].astype(F32) * y).astype(ua_ref.dtype)

    gq = gqx_ref[...]
    for h in range(X_HEADS):
        sl = slice(h * X_HEAD_DIM, (h + 1) * X_HEAD_DIM)
        qn = _rms_rows(xq_ref[:, sl].astype(F32), gq).astype(BF16)
        s = lax.dot_general(qn, mk_ref[:, sl], (((1,), (1,)), ((), ())), preferred_element_type=F32)
        s = s - jnp.max(s, axis=-1, keepdims=True)
        p = jnp.exp(s)
        p = p / jnp.sum(p, axis=-1, keepdims=True)
        oc_ref[:, sl] = jnp.dot(p.astype(BF16), mv_ref[:, sl], preferred_element_type=F32).astype(oc_ref.dtype)

    ua = ua_ref[...]
    ob = ob_ref[...]
    oc = oc_ref[...]
    for c in range(d_model // MERGE_CHUNK):
        sl = slice(c * MERGE_CHUNK, (c + 1) * MERGE_CHUNK)
        merged = jnp.zeros((tm, MERGE_CHUNK), F32)
        for br, (act, w_ref) in enumerate(((ua, wa_ref), (ob, wb_ref), (oc, wc_ref))):
            gsl = slice(br * d_model + c * MERGE_CHUNK, br * d_model + (c + 1) * MERGE_CHUNK)
            gate = jax.nn.sigmoid(gates_ref[:, gsl].astype(F32))
            merged = merged + gate * jnp.dot(act, w_ref[:, sl], preferred_element_type=F32)
        mg_ref[:, sl] = merged.astype(mg_ref.dtype)

    o_ref[...] = x_ref[...] + jnp.dot(mg_ref[...], wo_ref[...], preferred_element_type=F32)


def merge(proj, ob, x, conv_w, g_q_x, mk, mv, w_conv_out, w_diff_out, w_x_out, w_o,
          *, batch, cols, tm):
    T, D = x.shape
    S = T // batch
    M = mk.shape[0] // batch
    assert S % tm == 0 and tm % SUBLANES == 0
    tiles_per_seq = S // tm
    halo_blocks = tm // SUBLANES

    def col_spec(name, width):
        assert cols[name] % width == 0
        return pl.BlockSpec((tm, width), lambda i: (i, cols[name] // width))

    def halo_spec(name):
        return pl.BlockSpec((SUBLANES, CONV_WIDTH),
                            lambda i: (jnp.maximum(i * halo_blocks - 1, 0), cols[name] // CONV_WIDTH))

    def resident(shape):
        return pl.BlockSpec(shape, lambda i: (0, 0), pipeline_mode=pl.Buffered(1))

    return pl.pallas_call(
        functools.partial(_merge_kernel, tiles_per_seq=tiles_per_seq),
        grid=(T // tm,),
        in_specs=[
            col_spec("gates", N_BRANCH * D),
            col_spec("cb", CONV_WIDTH),
            col_spec("cc", CONV_WIDTH),
            col_spec("cx", CONV_WIDTH),
            halo_spec("cc"),
            halo_spec("cx"),
            col_spec("xq", X_WIDTH),
            pl.BlockSpec((tm, DIFF_VW), lambda i: (i, 0)),
            pl.BlockSpec((tm, D), lambda i: (i, 0)),
            pl.BlockSpec((CONV_K, CONV_WIDTH), lambda i: (0, 0)),
            pl.BlockSpec((1, X_HEAD_DIM), lambda i: (0, 0)),
            pl.BlockSpec((M, X_WIDTH), lambda i: (i // tiles_per_seq, 0)),
            pl.BlockSpec((M, X_WIDTH), lambda i: (i // tiles_per_seq, 0)),
            resident((CONV_WIDTH, D)),
            resident((DIFF_VW, D)),
            resident((X_WIDTH, D)),
            resident((D, D)),
        ],
        out_specs=pl.BlockSpec((tm, D), lambda i: (i, 0)),
        out_shape=jax.ShapeDtypeStruct((T, D), F32),
        scratch_shapes=[
            pltpu.VMEM((tm, CONV_WIDTH), BF16),
            pltpu.VMEM((tm, X_WIDTH), BF16),
            pltpu.VMEM((tm, D), BF16),
        ],
        compiler_params=_compiler_params(("parallel",)),
        name="merge",
    )(proj, proj, proj, proj, proj, proj, proj, ob, x, conv_w, g_q_x.reshape(1, X_HEAD_DIM),
      mk, mv, w_conv_out, w_diff_out, w_x_out, w_o)


def _ffn_kernel(x_ref, g_ref, wg_ref, wu_ref, wd_ref, o_ref, h_ref):
    @pl.when(pl.program_id(1) == 0)
    def _():
        x = x_ref[...]
        h_ref[...] = _rms_rows(x, g_ref[...]).astype(h_ref.dtype)
        o_ref[...] = x

    h = h_ref[...]
    a = jnp.dot(h, wg_ref[...], preferred_element_type=F32)
    b = jnp.dot(h, wu_ref[...], preferred_element_type=F32)
    act = (a * jax.nn.sigmoid(a) * b).astype(BF16)
    o_ref[...] += jnp.dot(act, wd_ref[...], preferred_element_type=F32)


def ffn(x, g, w_gate_up, w_down, *, tm, tf):
    T, D = x.shape
    d_ff = w_down.shape[0]
    assert T % tm == 0 and d_ff % tf == 0
    nf = d_ff // tf
    return pl.pallas_call(
        _ffn_kernel,
        grid=(T // tm, nf),
        in_specs=[
            pl.BlockSpec((tm, D), lambda i, f: (i, 0)),
            pl.BlockSpec((1, D), lambda i, f: (0, 0)),
            pl.BlockSpec((D, tf), lambda i, f: (0, f)),
            pl.BlockSpec((D, tf), lambda i, f: (0, nf + f)),
            pl.BlockSpec((tf, D), lambda i, f: (f, 0)),
        ],
        out_specs=pl.BlockSpec((tm, D), lambda i, f: (i, 0)),
        out_shape=jax.ShapeDtypeStruct((T, D), F32),
        scratch_shapes=[pltpu.VMEM((tm, D), BF16)],
        compiler_params=_compiler_params(("parallel", "arbitrary")),
        name="ffn",
    )(x, g.reshape(1, D), w_gate_up, w_gate_up, w_down)


def _tile(n, want):
    t = min(n, want)
    while n % t:
        t -= SUBLANES
    return t


def kernel(x, mem, positions, g_mix, w_in, conv_w, w_conv_out, g_q_diff, g_k_diff, lambda_vecs,
           g_subln, w_diff_out, g_mem, w_mem_kv, g_q_x, g_k_x, w_x_out, w_o, g_ffn, w_gate_up, w_down):
    B, S, D = x.shape
    M = mem.shape[1]
    T = B * S
    depth = w_in.shape[0]
    d_ff = w_down.shape[1]

    n_conv = 3 * CONV_WIDTH
    n_diff = 2 * DIFF_QK + DIFF_VW
    gates_at = n_conv + n_diff + X_WIDTH
    n_gates = N_BRANCH * D
    cols = {
        "gates": 0,
        "cb": n_gates,
        "cc": n_gates + CONV_WIDTH,
        "cx": n_gates + 2 * CONV_WIDTH,
        "dq": n_gates + n_conv,
        "dk": n_gates + n_conv + DIFF_QK,
        "dv": n_gates + n_conv + 2 * DIFF_QK,
        "xq": n_gates + n_conv + n_diff,
    }

    half = ROT_HALF
    inv_freq = ROPE_THETA ** (-jnp.arange(half, dtype=F32) / half)
    lane_d = jnp.arange(LANES) % DIFF_D
    inv_freq_lanes = jnp.where(lane_d < ROT, inv_freq[lane_d % half], 0.0).astype(F32).reshape(1, LANES)

    xf = x.reshape(T, D)
    memf = mem.reshape(B * M, D)
    pos = positions.reshape(T, 1)

    tm_proj = _tile(T, 1024)
    tm_prep = _tile(T, 512)
    tq = _tile(S, 256)
    tm_merge = _tile(S, 256)
    tm_ffn = _tile(T, 512)
    tf = _tile(d_ff, 512)

    for l in range(depth):
        lam_init = 0.8 - 0.6 * math.exp(-0.3 * l)
        w_in_l = jnp.concatenate([w_in[l][:, gates_at:], w_in[l][:, :gates_at]], axis=1).astype(BF16)

        proj = norm_matmul(xf, g_mix[l], w_in_l, tm=tm_proj, tn=1024, out_dtype=BF16)
        q, k = qk_prep(proj, pos, inv_freq_lanes, g_q_diff[l], g_k_diff[l],
                       q_col=cols["dq"], k_col=cols["dk"], tm=tm_prep)
        ob = diff_attn(q, k, proj, lambda_vecs[l], g_subln[l], batch=B, v_col=cols["dv"],
                       tq=tq, tk=tq, lam_init=lam_init)

        kv = norm_matmul(memf, g_mem[l], w_mem_kv[l].astype(BF16), tm=B * M, tn=1024, out_dtype=F32)
        mk, mv = mem_prep(kv, g_k_x[l])

        xf = merge(proj, ob, xf, conv_w[l], g_q_x[l], mk, mv,
                   w_conv_out[l].astype(BF16), w_diff_out[l].astype(BF16), w_x_out[l].astype(BF16),
                   w_o[l].astype(BF16), batch=B, cols=cols, tm=tm_merge)
        xf = ffn(xf, g_ffn[l], w_gate_up[l].astype(BF16), w_down[l].astype(BF16), tm=tm_ffn, tf=tf)

    return xf.reshape(B, S, D)
```

```python
import functools
import math

import jax
import jax.numpy as jnp
from jax import lax
from jax.experimental import pallas as pl
from jax.experimental.pallas import tpu as pltpu

EPS = 1e-6
CONV_WIDTH = 1024
CONV_K = 3
DIFF_HEADS = 8
DIFF_D = 64
DIFF_V = 2 * DIFF_D
DIFF_QK = DIFF_HEADS * 2 * DIFF_D
DIFF_VW = DIFF_HEADS * DIFF_V
X_HEADS = 4
X_HEAD_DIM = 256
X_WIDTH = X_HEADS * X_HEAD_DIM
N_BRANCH = 3
ROPE_THETA = 500000.0
ROT_FRAC = 4
ROT = DIFF_D // ROT_FRAC
ROT_HALF = ROT // 2

LANES = 128
SUBLANES = 8
VMEM_LIMIT_BYTES = 56 * 1024 * 1024

F32 = jnp.float32
BF16 = jnp.bfloat16


def _compiler_params(semantics):
    return pltpu.CompilerParams(dimension_semantics=semantics, vmem_limit_bytes=VMEM_LIMIT_BYTES)


def _rms_rows(t, g):
    ms = jnp.mean(t * t, axis=-1, keepdims=True)
    return t * lax.rsqrt(ms + EPS) * g


def _norm_matmul_kernel(x_ref, g_ref, w_ref, o_ref, h_ref):
    @pl.when(pl.program_id(1) == 0)
    def _():
        h_ref[...] = _rms_rows(x_ref[...], g_ref[...]).astype(h_ref.dtype)

    o_ref[...] = jnp.dot(h_ref[...], w_ref[...], preferred_element_type=F32).astype(o_ref.dtype)


def norm_matmul(x, g, w, *, tm, tn, out_dtype):
    T, D = x.shape
    N = w.shape[1]
    assert T % tm == 0 and N % tn == 0
    return pl.pallas_call(
        _norm_matmul_kernel,
        grid=(T // tm, N // tn),
        in_specs=[
            pl.BlockSpec((tm, D), lambda i, j: (i, 0)),
            pl.BlockSpec((1, D), lambda i, j: (0, 0)),
            pl.BlockSpec((D, tn), lambda i, j: (0, j)),
        ],
        out_specs=pl.BlockSpec((tm, tn), lambda i, j: (i, j)),
        out_shape=jax.ShapeDtypeStruct((T, N), out_dtype),
        scratch_shapes=[pltpu.VMEM((tm, D), BF16)],
        compiler_params=_compiler_params(("parallel", "arbitrary")),
        name="norm_matmul",
    )(x, g.reshape(1, D), w)


def _group_mean_sq(t, gsum):
    sq = t * t
    hi = sq.astype(BF16)
    lo = (sq - hi.astype(F32)).astype(BF16)
    s = jnp.dot(hi, gsum, preferred_element_type=F32) + jnp.dot(lo, gsum, preferred_element_type=F32)
    return s * (1.0 / DIFF_D)


def _qk_prep_kernel(pos_ref, invf_ref, gq_ref, gk_ref, q_ref, k_ref, v_ref, qt_ref, ko_ref, vt_ref,
                    *, tk):
    tm = q_ref.shape[0]
    lane = lax.broadcasted_iota(jnp.int32, (1, LANES), 1)
    d = lane & (DIFF_D - 1)
    ang = pos_ref[...].astype(F32) * invf_ref[...]
    cos = jnp.cos(ang)
    sin = jnp.sin(ang)
    c_keep = jnp.where(d < ROT, cos, 1.0)
    s_from_hi = jnp.where(d < ROT_HALF, -sin, 0.0)
    s_from_lo = jnp.where((d >= ROT_HALF) & (d < ROT), sin, 0.0)

    row = lax.broadcasted_iota(jnp.int32, (LANES, LANES), 0)
    col = lax.broadcasted_iota(jnp.int32, (LANES, LANES), 1)
    gsum = jnp.where((row < DIFF_D) == (col < DIFF_D), 1.0, 0.0).astype(BF16)

    def norm_rot(src_ref, g, c):
        t = src_ref[:, c * LANES:(c + 1) * LANES].astype(F32)
        tn = t * lax.rsqrt(_group_mean_sq(t, gsum) + EPS) * g
        up = pltpu.roll(tn, LANES - ROT_HALF, axis=1)
        dn = pltpu.roll(tn, ROT_HALF, axis=1)
        return tn * c_keep + up * s_from_hi + dn * s_from_lo

    gq = gq_ref[...]
    gk = gk_ref[...]
    for h in range(DIFF_HEADS):
        sl = slice(h * LANES, (h + 1) * LANES)
        q = norm_rot(q_ref, gq, h) * (DIFF_D ** -0.5)
        qt_ref[h, 0] = q.T.astype(qt_ref.dtype)
        ko_ref[:, sl] = norm_rot(k_ref, gk, h).astype(ko_ref.dtype)
        vt = v_ref[:, sl].astype(F32).T
        for u in range(tm // tk):
            vt_ref[h, u] = vt[:, u * tk:(u + 1) * tk].astype(vt_ref.dtype)


def qk_prep(proj, pos, inv_freq_lanes, g_q, g_k, *, q_col, k_col, v_col, tm, tk):
    T = proj.shape[0]
    assert T % tm == 0 and tm % tk == 0
    assert q_col % DIFF_QK == 0 and k_col % DIFF_QK == 0 and v_col % DIFF_VW == 0 and DIFF_V == LANES
    gq = jnp.tile(g_q, LANES // DIFF_D).reshape(1, LANES)
    gk = jnp.tile(g_k, LANES // DIFF_D).reshape(1, LANES)
    return pl.pallas_call(
        functools.partial(_qk_prep_kernel, tk=tk),
        grid=(T // tm,),
        in_specs=[
            pl.BlockSpec((tm, 1), lambda i: (i, 0)),
            pl.BlockSpec((1, LANES), lambda i: (0, 0)),
            pl.BlockSpec((1, LANES), lambda i: (0, 0)),
            pl.BlockSpec((1, LANES), lambda i: (0, 0)),
            pl.BlockSpec((tm, DIFF_QK), lambda i: (i, q_col // DIFF_QK)),
            pl.BlockSpec((tm, DIFF_QK), lambda i: (i, k_col // DIFF_QK)),
            pl.BlockSpec((tm, DIFF_VW), lambda i: (i, v_col // DIFF_VW)),
        ],
        out_specs=[
            pl.BlockSpec((DIFF_HEADS, 1, LANES, tm), lambda i: (0, i, 0, 0)),
            pl.BlockSpec((tm, DIFF_QK), lambda i: (i, 0)),
            pl.BlockSpec((DIFF_HEADS, tm // tk, LANES, tk), lambda i: (0, i, 0, 0)),
        ],
        out_shape=[
            jax.ShapeDtypeStruct((DIFF_HEADS, T // tm, LANES, tm), BF16),
            jax.ShapeDtypeStruct((T, DIFF_QK), BF16),
            jax.ShapeDtypeStruct((DIFF_HEADS, T // tk, LANES, tk), BF16),
        ],
        compiler_params=_compiler_params(("parallel",)),
        name="qk_prep",
    )(pos, inv_freq_lanes, gq, gk, proj, proj, proj)


def _diff_attn_kernel(lam_ref, g_ref, qt_ref, k_ref, vt_ref, o_ref, m_ref, l_ref, acc_ref,
                      *, tq, tk, lam_init):
    i = pl.program_id(2)
    qt = qt_ref[0, 0]
    zeros = jnp.zeros((DIFF_D, tq), qt.dtype)
    q_comp = (jnp.concatenate([qt[:DIFF_D], zeros], axis=0),
              jnp.concatenate([zeros, qt[DIFF_D:]], axis=0))

    m_ref[...] = jnp.full(m_ref.shape, -jnp.inf, F32)
    l_ref[...] = jnp.zeros(l_ref.shape, F32)
    acc_ref[...] = jnp.zeros(acc_ref.shape, F32)

    def kv_block(j, masked):
        k = k_ref[pl.ds(pl.multiple_of(j * tk, tk), tk), :]
        vt = vt_ref[0, j]
        if masked:
            kpos = j * tk + lax.broadcasted_iota(jnp.int32, (tk, tq), 0)
            qpos = i * tq + lax.broadcasted_iota(jnp.int32, (tk, tq), 1)
            causal = kpos <= qpos
        for c in range(2):
            s = jnp.dot(k, q_comp[c], preferred_element_type=F32)
            if masked:
                s = jnp.where(causal, s, -jnp.inf)
            m_prev = m_ref[c]
            m_new = jnp.maximum(m_prev, jnp.max(s, axis=0, keepdims=True))
            alpha = jnp.exp(m_prev - m_new)
            p = jnp.exp(s - m_new)
            l_ref[c] = alpha * l_ref[c] + jnp.sum(p, axis=0, keepdims=True)
            acc_ref[c] = alpha * acc_ref[c] + jnp.dot(vt, p.astype(vt.dtype), preferred_element_type=F32)
            m_ref[c] = m_new

    n_full = (i * tq) // tk

    def full_block(j, carry):
        kv_block(j, masked=False)
        return carry

    lax.fori_loop(0, n_full, full_block, 0)
    for d in range(tq // tk):
        kv_block(n_full + d, masked=True)

    lv = lam_ref[...]
    e1 = jnp.exp(jnp.sum(lv[0:1, :] * lv[1:2, :], axis=-1, keepdims=True))
    e2 = jnp.exp(jnp.sum(lv[2:3, :] * lv[3:4, :], axis=-1, keepdims=True))
    lam = e1 - e2 + lam_init
    o = acc_ref[0] / l_ref[0] - lam * (acc_ref[1] / l_ref[1])
    o = o * lax.rsqrt(jnp.mean(o * o, axis=0, keepdims=True) + EPS)
    o_ref[...] = (o.T * g_ref[...] * (1.0 - lam_init)).astype(o_ref.dtype)


def diff_attn(qt, k, vt, lambda_vecs, g_subln, *, batch, tq, tk, lam_init):
    T = k.shape[0]
    S = T // batch
    assert S % tq == 0 and tq % tk == 0
    nq, nk = S // tq, S // tk
    return pl.pallas_call(
        functools.partial(_diff_attn_kernel, tq=tq, tk=tk, lam_init=lam_init),
        grid=(batch, DIFF_HEADS, nq),
        in_specs=[
            pl.BlockSpec((4, DIFF_D), lambda b, h, i: (0, 0)),
            pl.BlockSpec((1, DIFF_V), lambda b, h, i: (0, 0)),
            pl.BlockSpec((1, 1, LANES, tq), lambda b, h, i: (h, b * nq + i, 0, 0)),
            pl.BlockSpec((S, LANES), lambda b, h, i: (b, h)),
            pl.BlockSpec((1, nk, LANES, tk), lambda b, h, i: (h, b, 0, 0)),
        ],
        out_specs=pl.BlockSpec((tq, DIFF_V), lambda b, h, i: (b * nq + i, h)),
        out_shape=jax.ShapeDtypeStruct((T, DIFF_VW), BF16),
        scratch_shapes=[
            pltpu.VMEM((2, 1, tq), F32),
            pltpu.VMEM((2, 1, tq), F32),
            pltpu.VMEM((2, DIFF_V, tq), F32),
        ],
        compiler_params=_compiler_params(("parallel", "parallel", "arbitrary")),
        name="diff_attn",
    )(lambda_vecs, g_subln.reshape(1, DIFF_V), qt, k, vt)


def _mem_prep_kernel(kv_ref, g_ref, mk_ref, mv_ref):
    g = g_ref[...]
    for h in range(X_HEADS):
        sl = slice(h * X_HEAD_DIM, (h + 1) * X_HEAD_DIM)
        kh = _rms_rows(kv_ref[:, sl], g) * (X_HEAD_DIM ** -0.5)
        mk_ref[:, sl] = kh.astype(mk_ref.dtype)
    mv_ref[...] = kv_ref[:, X_WIDTH:].astype(mv_ref.dtype)


def mem_prep(kv, g_k_x):
    R = kv.shape[0]
    out = jax.ShapeDtypeStruct((R, X_WIDTH), BF16)
    return pl.pallas_call(
        _mem_prep_kernel,
        grid=(1,),
        in_specs=[
            pl.BlockSpec((R, 2 * X_WIDTH), lambda i: (0, 0)),
            pl.BlockSpec((1, X_HEAD_DIM), lambda i: (0, 0)),
        ],
        out_specs=[pl.BlockSpec((R, X_WIDTH), lambda i: (0, 0))] * 2,
        out_shape=[out, out],
        compiler_params=_compiler_params(("arbitrary",)),
        name="mem_prep",
    )(kv, g_k_x.reshape(1, X_HEAD_DIM))


MERGE_CHUNK = 512


def _merge_kernel(gates_ref, cb_ref, cc_ref, cx_ref, cch_ref, cxh_ref, xq_ref, ob_ref, x_ref,
                  convw_ref, gqx_ref, mk_ref, mv_ref, wa_ref, wb_ref, wc_ref, wo_ref,
                  o_ref, ua_ref, oc_ref, mg_ref, *, tiles_per_seq):
    tm = x_ref.shape[0]
    d_model = x_ref.shape[1]
    first = pl.program_id(0) % tiles_per_seq == 0

    z = cc_ref[...].astype(F32) * cx_ref[...].astype(F32)
    zp = cch_ref[...].astype(F32) * cxh_ref[...].astype(F32)
    zp = jnp.where(first, 0.0, zp)
    w = convw_ref[...]
    row8 = lax.broadcasted_iota(jnp.int32, (SUBLANES, 1), 0)
    y = z * w[CONV_K - 1:CONV_K, :]
    for back in range(1, CONV_K):
        zs = pltpu.roll(z, back, axis=0)
        ps = pltpu.roll(zp, back, axis=0)
        head = jnp.where(row8 < back, ps, zs[:SUBLANES])
        zs = jnp.concatenate([head, zs[SUBLANES:]], axis=0)
        y = y + zs * w[CONV_K - 1 - back:CONV_K - back, :]
    ua_ref[...] = (cb_ref[...].astype(F32) * y).astype(ua_ref.dtype)

    gq = gqx_ref[...]
    for h in range(X_HEADS):
        sl = slice(h * X_HEAD_DIM, (h + 1) * X_HEAD_DIM)
        qn = _rms_rows(xq_ref[:, sl].astype(F32), gq).astype(BF16)
        s = lax.dot_general(qn, mk_ref[:, sl], (((1,), (1,)), ((), ())), preferred_element_type=F32)
        s = s - jnp.max(s, axis=-1, keepdims=True)
        p = jnp.exp(s)
        p = p / jnp.sum(p, axis=-1, keepdims=True)
        oc_ref[:, sl] = jnp.dot(p.astype(BF16), mv_ref[:, sl], preferred_element_type=F32).astype(oc_ref.dtype)

    ua = ua_ref[...]
    ob = ob_ref[...]
    oc = oc_ref[...]
    for c in range(d_model // MERGE_CHUNK):
        sl = slice(c * MERGE_CHUNK, (c + 1) * MERGE_CHUNK)
        merged = jnp.zeros((tm, MERGE_CHUNK), F32)
        for br, (act, w_ref) in enumerate(((ua, wa_ref), (ob, wb_ref), (oc, wc_ref))):
            gsl = slice(br * d_model + c * MERGE_CHUNK, br * d_model + (c + 1) * MERGE_CHUNK)
            gate = jax.nn.sigmoid(gates_ref[:, gsl].astype(F32))
            merged = merged + gate * jnp.dot(act, w_ref[:, sl], preferred_element_type=F32)
        mg_ref[:, sl] = merged.astype(mg_ref.dtype)

    o_ref[...] = x_ref[...] + jnp.dot(mg_ref[...], wo_ref[...], preferred_element_type=F32)


def merge(proj, ob, x, conv_w, g_q_x, mk, mv, w_conv_out, w_diff_out, w_x_out, w_o,
          *, batch, cols, tm):
    T, D = x.shape
    S = T // batch
    M = mk.shape[0] // batch
    assert S % tm == 0 and tm % SUBLANES == 0
    tiles_per_seq = S // tm
    halo_blocks = tm // SUBLANES

    def col_spec(name, width):
        assert cols[name] % width == 0
        return pl.BlockSpec((tm, width), lambda i: (i, cols[name] // width))

    def halo_spec(name):
        return pl.BlockSpec((SUBLANES, CONV_WIDTH),
                            lambda i: (jnp.maximum(i * halo_blocks - 1, 0), cols[name] // CONV_WIDTH))

    def resident(shape):
        return pl.BlockSpec(shape, lambda i: (0, 0), pipeline_mode=pl.Buffered(1))

    return pl.pallas_call(
        functools.partial(_merge_kernel, tiles_per_seq=tiles_per_seq),
        grid=(T // tm,),
        in_specs=[
            col_spec("gates", N_BRANCH * D),
            col_spec("cb", CONV_WIDTH),
            col_spec("cc", CONV_WIDTH),
            col_spec("cx", CONV_WIDTH),
            halo_spec("cc"),
            halo_spec("cx"),
            col_spec("xq", X_WIDTH),
            pl.BlockSpec((tm, DIFF_VW), lambda i: (i, 0)),
            pl.BlockSpec((tm, D), lambda i: (i, 0)),
            pl.BlockSpec((CONV_K, CONV_WIDTH), lambda i: (0, 0)),
            pl.BlockSpec((1, X_HEAD_DIM), lambda i: (0, 0)),
            pl.BlockSpec((M, X_WIDTH), lambda i: (i // tiles_per_seq, 0)),
            pl.BlockSpec((M, X_WIDTH), lambda i: (i // tiles_per_seq, 0)),
            resident((CONV_WIDTH, D)),
            resident((DIFF_VW, D)),
            resident((X_WIDTH, D)),
            resident((D, D)),
        ],
        out_specs=pl.BlockSpec((tm, D), lambda i: (i, 0)),
        out_shape=jax.ShapeDtypeStruct((T, D), F32),
        scratch_shapes=[
            pltpu.VMEM((tm, CONV_WIDTH), BF16),
            pltpu.VMEM((tm, X_WIDTH), BF16),
            pltpu.VMEM((tm, D), BF16),
        ],
        compiler_params=_compiler_params(("parallel",)),
        name="merge",
    )(proj, proj, proj, proj, proj, proj, proj, ob, x, conv_w, g_q_x.reshape(1, X_HEAD_DIM),
      mk, mv, w_conv_out, w_diff_out, w_x_out, w_o)


def _ffn_kernel(x_ref, g_ref, wg_ref, wu_ref, wd_ref, o_ref, h_ref):
    @pl.when(pl.program_id(1) == 0)
    def _():
        x = x_ref[...]
        h_ref[...] = _rms_rows(x, g_ref[...]).astype(h_ref.dtype)
        o_ref[...] = x

    h = h_ref[...]
    a = jnp.dot(h, wg_ref[...], preferred_element_type=F32)
    b = jnp.dot(h, wu_ref[...], preferred_element_type=F32)
    act = (a * jax.nn.sigmoid(a) * b).astype(BF16)
    o_ref[...] += jnp.dot(act, wd_ref[...], preferred_element_type=F32)


def ffn(x, g, w_gate_up, w_down, *, tm, tf):
    T, D = x.shape
    d_ff = w_down.shape[0]
    assert T % tm == 0 and d_ff % tf == 0
    nf = d_ff // tf
    return pl.pallas_call(
        _ffn_kernel,
        grid=(T // tm, nf),
        in_specs=[
            pl.BlockSpec((tm, D), lambda i, f: (i, 0)),
            pl.BlockSpec((1, D), lambda i, f: (0, 0)),
            pl.BlockSpec((D, tf), lambda i, f: (0, f)),
            pl.BlockSpec((D, tf), lambda i, f: (0, nf + f)),
            pl.BlockSpec((tf, D), lambda i, f: (f, 0)),
        ],
        out_specs=pl.BlockSpec((tm, D), lambda i, f: (i, 0)),
        out_shape=jax.ShapeDtypeStruct((T, D), F32),
        scratch_shapes=[pltpu.VMEM((tm, D), BF16)],
        compiler_params=_compiler_params(("parallel", "arbitrary")),
        name="ffn",
    )(x, g.reshape(1, D), w_gate_up, w_gate_up, w_down)


def _tile(n, want):
    t = min(n, want)
    while n % t:
        t -= SUBLANES
    return t


def kernel(x, mem, positions, g_mix, w_in, conv_w, w_conv_out, g_q_diff, g_k_diff, lambda_vecs,
           g_subln, w_diff_out, g_mem, w_mem_kv, g_q_x, g_k_x, w_x_out, w_o, g_ffn, w_gate_up, w_down):
    B, S, D = x.shape
    M = mem.shape[1]
    T = B * S
    depth = w_in.shape[0]
    d_ff = w_down.shape[1]

    n_conv = 3 * CONV_WIDTH
    n_diff = 2 * DIFF_QK + DIFF_VW
    gates_at = n_conv + n_diff + X_WIDTH
    n_gates = N_BRANCH * D
    cols = {
        "gates": 0,
        "cb": n_gates,
        "cc": n_gates + CONV_WIDTH,
        "cx": n_gates + 2 * CONV_WIDTH,
        "dq": n_gates + n_conv,
        "dk": n_gates + n_conv + DIFF_QK,
        "dv": n_gates + n_conv + 2 * DIFF_QK,
        "xq": n_gates + n_conv + n_diff,
    }

    half = ROT_HALF
    inv_freq = ROPE_THETA ** (-jnp.arange(half, dtype=F32) / half)
    lane_d = jnp.arange(LANES) % DIFF_D
    inv_freq_lanes = jnp.where(lane_d < ROT, inv_freq[lane_d % half], 0.0).astype(F32).reshape(1, LANES)

    xf = x.reshape(T, D)
    memf = mem.reshape(B * M, D)
    pos = positions.reshape(T, 1)

    tm_proj = _tile(T, 1024)
    tq = _tile(S, 512)
    tk = _tile(tq, 256)
    tm_merge = _tile(S, 256)
    tm_ffn = _tile(T, 512)
    tf = _tile(d_ff, 512)

    for l in range(depth):
        lam_init = 0.8 - 0.6 * math.exp(-0.3 * l)
        w_in_l = jnp.concatenate([w_in[l][:, gates_at:], w_in[l][:, :gates_at]], axis=1).astype(BF16)

        proj = norm_matmul(xf, g_mix[l], w_in_l, tm=tm_proj, tn=1024, out_dtype=BF16)
        qt, k, vt = qk_prep(proj, pos, inv_freq_lanes, g_q_diff[l], g_k_diff[l],
                            q_col=cols["dq"], k_col=cols["dk"], v_col=cols["dv"], tm=tq, tk=tk)
        ob = diff_attn(qt, k, vt, lambda_vecs[l], g_subln[l], batch=B, tq=tq, tk=tk, lam_init=lam_init)

        kv = norm_matmul(memf, g_mem[l], w_mem_kv[l].astype(BF16), tm=B * M, tn=1024, out_dtype=F32)
        mk, mv = mem_prep(kv, g_k_x[l])

        xf = merge(proj, ob, xf, conv_w[l], g_q_x[l], mk, mv,
                   w_conv_out[l].astype(BF16), w_diff_out[l].astype(BF16), w_x_out[l].astype(BF16),
                   w_o[l].astype(BF16), batch=B, cols=cols, tm=tm_merge)
        xf = ffn(xf, g_ffn[l], w_gate_up[l].astype(BF16), w_down[l].astype(BF16), tm=tm_ffn, tf=tf)

    return xf.reshape(B, S, D)
```

```python
import functools
import math

import jax
import jax.numpy as jnp
from jax import lax
from jax.experimental import pallas as pl
from jax.experimental.pallas import tpu as pltpu

EPS = 1e-6
CONV_WIDTH = 1024
CONV_K = 3
DIFF_HEADS = 8
DIFF_D = 64
DIFF_V = 2 * DIFF_D
DIFF_QK = DIFF_HEADS * 2 * DIFF_D
DIFF_VW = DIFF_HEADS * DIFF_V
X_HEADS = 4
X_HEAD_DIM = 256
X_WIDTH = X_HEADS * X_HEAD_DIM
N_BRANCH = 3
ROPE_THETA = 500000.0
ROT_FRAC = 4
ROT = DIFF_D // ROT_FRAC
ROT_HALF = ROT // 2

LANES = 128
SUBLANES = 8
BF16_ROWS = 16
V_ROWS = DIFF_V + BF16_ROWS
Q_SCALE = DIFF_D ** -0.5 * math.log2(math.e)
VMEM_LIMIT_BYTES = 56 * 1024 * 1024

F32 = jnp.float32
BF16 = jnp.bfloat16


def _compiler_params(semantics):
    return pltpu.CompilerParams(dimension_semantics=semantics, vmem_limit_bytes=VMEM_LIMIT_BYTES)


def _rms_rows(t, g):
    ms = jnp.mean(t * t, axis=-1, keepdims=True)
    return t * lax.rsqrt(ms + EPS) * g


def _norm_matmul_kernel(x_ref, g_ref, w_ref, o_ref, h_ref):
    @pl.when(pl.program_id(1) == 0)
    def _():
        h_ref[...] = _rms_rows(x_ref[...], g_ref[...]).astype(h_ref.dtype)

    o_ref[...] = jnp.dot(h_ref[...], w_ref[...], preferred_element_type=F32).astype(o_ref.dtype)


def norm_matmul(x, g, w, *, tm, tn, out_dtype):
    T, D = x.shape
    N = w.shape[1]
    assert T % tm == 0 and N % tn == 0
    return pl.pallas_call(
        _norm_matmul_kernel,
        grid=(T // tm, N // tn),
        in_specs=[
            pl.BlockSpec((tm, D), lambda i, j: (i, 0)),
            pl.BlockSpec((1, D), lambda i, j: (0, 0)),
            pl.BlockSpec((D, tn), lambda i, j: (0, j)),
        ],
        out_specs=pl.BlockSpec((tm, tn), lambda i, j: (i, j)),
        out_shape=jax.ShapeDtypeStruct((T, N), out_dtype),
        scratch_shapes=[pltpu.VMEM((tm, D), BF16)],
        compiler_params=_compiler_params(("parallel", "arbitrary")),
        name="norm_matmul",
    )(x, g.reshape(1, D), w)


def _group_mean_sq(t, gsum):
    sq = t * t
    hi = sq.astype(BF16)
    lo = (sq - hi.astype(F32)).astype(BF16)
    s = jnp.dot(hi, gsum, preferred_element_type=F32) + jnp.dot(lo, gsum, preferred_element_type=F32)
    return s * (1.0 / DIFF_D)


def _qk_prep_kernel(pos_ref, invf_ref, gq_ref, gk_ref, q_ref, k_ref, v_ref, qt_ref, ko_ref, vt_ref,
                    *, tk):
    tm = q_ref.shape[0]
    lane = lax.broadcasted_iota(jnp.int32, (1, LANES), 1)
    d = lane & (DIFF_D - 1)
    ang = pos_ref[...].astype(F32) * invf_ref[...]
    cos = jnp.cos(ang)
    sin = jnp.sin(ang)
    c_keep = jnp.where(d < ROT, cos, 1.0)
    s_from_hi = jnp.where(d < ROT_HALF, -sin, 0.0)
    s_from_lo = jnp.where((d >= ROT_HALF) & (d < ROT), sin, 0.0)

    row = lax.broadcasted_iota(jnp.int32, (LANES, LANES), 0)
    col = lax.broadcasted_iota(jnp.int32, (LANES, LANES), 1)
    gsum = jnp.where((row < DIFF_D) == (col < DIFF_D), 1.0, 0.0).astype(BF16)

    def norm_rot(src_ref, g, c):
        t = src_ref[:, c * LANES:(c + 1) * LANES].astype(F32)
        tn = t * lax.rsqrt(_group_mean_sq(t, gsum) + EPS) * g
        up = pltpu.roll(tn, LANES - ROT_HALF, axis=1)
        dn = pltpu.roll(tn, ROT_HALF, axis=1)
        return tn * c_keep + up * s_from_hi + dn * s_from_lo

    gq = gq_ref[...]
    gk = gk_ref[...]
    pad_row = lax.broadcasted_iota(jnp.int32, (V_ROWS - DIFF_V, tk), 0)
    ones_row = jnp.where(pad_row == 0, 1.0, 0.0).astype(vt_ref.dtype)
    for h in range(DIFF_HEADS):
        sl = slice(h * LANES, (h + 1) * LANES)
        q = norm_rot(q_ref, gq, h) * Q_SCALE
        qt_ref[h, 0] = q.T.astype(qt_ref.dtype)
        ko_ref[:, sl] = norm_rot(k_ref, gk, h).astype(ko_ref.dtype)
        vt = v_ref[:, sl].astype(F32).T
        for u in range(tm // tk):
            vt_ref[h, u, :DIFF_V] = vt[:, u * tk:(u + 1) * tk].astype(vt_ref.dtype)
            vt_ref[h, u, DIFF_V:] = ones_row


def qk_prep(proj, pos, inv_freq_lanes, g_q, g_k, *, q_col, k_col, v_col, tm, tk):
    T = proj.shape[0]
    assert T % tm == 0 and tm % tk == 0
    assert q_col % DIFF_QK == 0 and k_col % DIFF_QK == 0 and v_col % DIFF_VW == 0 and DIFF_V == LANES
    gq = jnp.tile(g_q, LANES // DIFF_D).reshape(1, LANES)
    gk = jnp.tile(g_k, LANES // DIFF_D).reshape(1, LANES)
    return pl.pallas_call(
        functools.partial(_qk_prep_kernel, tk=tk),
        grid=(T // tm,),
        in_specs=[
            pl.BlockSpec((tm, 1), lambda i: (i, 0)),
            pl.BlockSpec((1, LANES), lambda i: (0, 0)),
            pl.BlockSpec((1, LANES), lambda i: (0, 0)),
            pl.BlockSpec((1, LANES), lambda i: (0, 0)),
            pl.BlockSpec((tm, DIFF_QK), lambda i: (i, q_col // DIFF_QK)),
            pl.BlockSpec((tm, DIFF_QK), lambda i: (i, k_col // DIFF_QK)),
            pl.BlockSpec((tm, DIFF_VW), lambda i: (i, v_col // DIFF_VW)),
        ],
        out_specs=[
            pl.BlockSpec((DIFF_HEADS, 1, LANES, tm), lambda i: (0, i, 0, 0)),
            pl.BlockSpec((tm, DIFF_QK), lambda i: (i, 0)),
            pl.BlockSpec((DIFF_HEADS, tm // tk, V_ROWS, tk), lambda i: (0, i, 0, 0)),
        ],
        out_shape=[
            jax.ShapeDtypeStruct((DIFF_HEADS, T // tm, LANES, tm), BF16),
            jax.ShapeDtypeStruct((T, DIFF_QK), BF16),
            jax.ShapeDtypeStruct((DIFF_HEADS, T // tk, V_ROWS, tk), BF16),
        ],
        compiler_params=_compiler_params(("parallel",)),
        name="qk_prep",
    )(pos, inv_freq_lanes, gq, gk, proj, proj, proj)


def _diff_attn_kernel(lam_ref, g_ref, qt_ref, k_ref, vt_ref, o_ref, m_ref, acc_ref,
                      s0_ref, s1_ref, p0_ref, p1_ref, a0_ref, a1_ref, bm0_ref, bm1_ref,
                      *, tq, tk, lam_init):
    assert tq == 2 * tk
    s_bufs, p_bufs, a_bufs, bm_bufs = (s0_ref, s1_ref), (p0_ref, p1_ref), (a0_ref, a1_ref), (bm0_ref, bm1_ref)
    i = pl.program_id(2)
    n_blocks = 2 * i + 2
    qt = qt_ref[0, 0]
    zeros = jnp.zeros((DIFF_D, tq), qt.dtype)
    q_comp = (jnp.concatenate([qt[:DIFF_D], zeros], axis=0),
              jnp.concatenate([zeros, qt[DIFF_D:]], axis=0))

    m_ref[...] = jnp.full(m_ref.shape, -jnp.inf, F32)
    acc_ref[...] = jnp.zeros(acc_ref.shape, F32)

    def scores(u, slot, masked):
        k = k_ref[pl.ds(pl.multiple_of(u * tk, tk), tk), :]
        if masked:
            kpos = u * tk + lax.broadcasted_iota(jnp.int32, (tk, tq), 0)
            qpos = i * tq + lax.broadcasted_iota(jnp.int32, (tk, tq), 1)
            causal = kpos <= qpos
        for c in range(2):
            s = jnp.dot(k, q_comp[c], preferred_element_type=F32)
            if masked:
                s = jnp.where(causal, s, -jnp.inf)
            s_bufs[slot][c] = s
            bm_bufs[slot][c] = jnp.max(s, axis=0, keepdims=True)

    def softmax(slot):
        for c in range(2):
            m_prev = m_ref[c]
            m_new = jnp.maximum(m_prev, bm_bufs[slot][c])
            a_bufs[slot][c] = jnp.exp2(m_prev - m_new)
            m_ref[c] = m_new
            p_bufs[slot][c] = jnp.exp2(s_bufs[slot][c] - m_new).astype(p_bufs[slot].dtype)

    def accumulate(u, slot):
        vt = vt_ref[0, u]
        for c in range(2):
            pv = jnp.dot(vt, p_bufs[slot][c], preferred_element_type=F32)
            acc_ref[c] = a_bufs[slot][c] * acc_ref[c] + pv

    scores(0, 0, masked=True)
    scores(1, 1, masked=True)
    softmax(0)

    def step_pair(u, masked):
        scores(u, 0, masked)
        accumulate(u - 2, 0)
        softmax(1)
        scores(u + 1, 1, masked)
        accumulate(u - 1, 1)
        softmax(0)

    def full_pair(r, carry):
        step_pair(2 * r, masked=False)
        return carry

    lax.fori_loop(1, i, full_pair, 0)

    @pl.when(i >= 1)
    def _():
        step_pair(2 * i, masked=True)

    accumulate(n_blocks - 2, 0)
    softmax(1)
    accumulate(n_blocks - 1, 1)

    lv = lam_ref[...]
    e1 = jnp.exp(jnp.sum(lv[0:1, :] * lv[1:2, :], axis=-1, keepdims=True))
    e2 = jnp.exp(jnp.sum(lv[2:3, :] * lv[3:4, :], axis=-1, keepdims=True))
    lam = e1 - e2 + lam_init
    o = (acc_ref[0, :DIFF_V] / acc_ref[0, DIFF_V:DIFF_V + 1]
         - lam * (acc_ref[1, :DIFF_V] / acc_ref[1, DIFF_V:DIFF_V + 1]))
    o = o * lax.rsqrt(jnp.mean(o * o, axis=0, keepdims=True) + EPS)
    o_ref[...] = (o.T * g_ref[...] * (1.0 - lam_init)).astype(o_ref.dtype)


def diff_attn(qt, k, vt, lambda_vecs, g_subln, *, batch, tq, tk, lam_init):
    T = k.shape[0]
    S = T // batch
    assert S % tq == 0 and tq == 2 * tk
    nq, nk = S // tq, S // tk
    stat = pltpu.VMEM((2, 1, tq), F32)
    return pl.pallas_call(
        functools.partial(_diff_attn_kernel, tq=tq, tk=tk, lam_init=lam_init),
        grid=(batch, DIFF_HEADS, nq),
        in_specs=[
            pl.BlockSpec((4, DIFF_D), lambda b, h, i: (0, 0)),
            pl.BlockSpec((1, DIFF_V), lambda b, h, i: (0, 0)),
            pl.BlockSpec((1, 1, LANES, tq), lambda b, h, i: (h, b * nq + i, 0, 0)),
            pl.BlockSpec((S, LANES), lambda b, h, i: (b, h)),
            pl.BlockSpec((1, nk, V_ROWS, tk), lambda b, h, i: (h, b, 0, 0)),
        ],
        out_specs=pl.BlockSpec((tq, DIFF_V), lambda b, h, i: (b * nq + i, h)),
        out_shape=jax.ShapeDtypeStruct((T, DIFF_VW), BF16),
        scratch_shapes=[
            stat,
            pltpu.VMEM((2, V_ROWS, tq), F32),
            pltpu.VMEM((2, tk, tq), F32),
            pltpu.VMEM((2, tk, tq), F32),
            pltpu.VMEM((2, tk, tq), BF16),
            pltpu.VMEM((2, tk, tq), BF16),
            stat, stat,
            stat, stat,
        ],
        compiler_params=_compiler_params(("parallel", "parallel", "arbitrary")),
        name="diff_attn",
    )(lambda_vecs, g_subln.reshape(1, DIFF_V), qt, k, vt)


def _mem_prep_kernel(kv_ref, g_ref, mk_ref, mv_ref):
    g = g_ref[...]
    for h in range(X_HEADS):
        sl = slice(h * X_HEAD_DIM, (h + 1) * X_HEAD_DIM)
        kh = _rms_rows(kv_ref[:, sl], g) * (X_HEAD_DIM ** -0.5)
        mk_ref[:, sl] = kh.astype(mk_ref.dtype)
    mv_ref[...] = kv_ref[:, X_WIDTH:].astype(mv_ref.dtype)


def mem_prep(kv, g_k_x):
    R = kv.shape[0]
    out = jax.ShapeDtypeStruct((R, X_WIDTH), BF16)
    return pl.pallas_call(
        _mem_prep_kernel,
        grid=(1,),
        in_specs=[
            pl.BlockSpec((R, 2 * X_WIDTH), lambda i: (0, 0)),
            pl.BlockSpec((1, X_HEAD_DIM), lambda i: (0, 0)),
        ],
        out_specs=[pl.BlockSpec((R, X_WIDTH), lambda i: (0, 0))] * 2,
        out_shape=[out, out],
        compiler_params=_compiler_params(("arbitrary",)),
        name="mem_prep",
    )(kv, g_k_x.reshape(1, X_HEAD_DIM))


MERGE_CHUNK = 512


def _merge_kernel(gates_ref, cb_ref, cc_ref, cx_ref, cch_ref, cxh_ref, xq_ref, ob_ref, x_ref,
                  convw_ref, gqx_ref, mk_ref, mv_ref, wa_ref, wb_ref, wc_ref, wo_ref,
                  o_ref, ua_ref, oc_ref, mg_ref, *, tiles_per_seq):
    tm = x_ref.shape[0]
    d_model = x_ref.shape[1]
    first = pl.program_id(0) % tiles_per_seq == 0

    z = cc_ref[...].astype(F32) * cx_ref[...].astype(F32)
    zp = cch_ref[...].astype(F32) * cxh_ref[...].astype(F32)
    zp = jnp.where(first, 0.0, zp)
    w = convw_ref[...]
    row8 = lax.broadcasted_iota(jnp.int32, (SUBLANES, 1), 0)
    y = z * w[CONV_K - 1:CONV_K, :]
    for back in range(1, CONV_K):
        zs = pltpu.roll(z, back, axis=0)
        ps = pltpu.roll(zp, back, axis=0)
        head = jnp.where(row8 < back, ps, zs[:SUBLANES])
        zs = jnp.concatenate([head, zs[SUBLANES:]], axis=0)
        y = y + zs * w[CONV_K - 1 - back:CONV_K - back, :]
    ua_ref[...] = (cb_ref[...].astype(F32) * y).astype(ua_ref.dtype)

    gq = gqx_ref[...]
    for h in range(X_HEADS):
        sl = slice(h * X_HEAD_DIM, (h + 1) * X_HEAD_DIM)
        qn = _rms_rows(xq_ref[:, sl].astype(F32), gq).astype(BF16)
        s = lax.dot_general(qn, mk_ref[:, sl], (((1,), (1,)), ((), ())), preferred_element_type=F32)
        s = s - jnp.max(s, axis=-1, keepdims=True)
        p = jnp.exp(s)
        p = p / jnp.sum(p, axis=-1, keepdims=True)
        oc_ref[:, sl] = jnp.dot(p.astype(BF16), mv_ref[:, sl], preferred_element_type=F32).astype(oc_ref.dtype)

    ua = ua_ref[...]
    ob = ob_ref[...]
    oc = oc_ref[...]
    for c in range(d_model // MERGE_CHUNK):
        sl = slice(c * MERGE_CHUNK, (c + 1) * MERGE_CHUNK)
        merged = jnp.zeros((tm, MERGE_CHUNK), F32)
        for br, (act, w_ref) in enumerate(((ua, wa_ref), (ob, wb_ref), (oc, wc_ref))):
            gsl = slice(br * d_model + c * MERGE_CHUNK, br * d_model + (c + 1) * MERGE_CHUNK)
            gate = jax.nn.sigmoid(gates_ref[:, gsl].astype(F32))
            merged = merged + gate * jnp.dot(act, w_ref[:, sl], preferred_element_type=F32)
        mg_ref[:, sl] = merged.astype(mg_ref.dtype)

    o_ref[...] = x_ref[...] + jnp.dot(mg_ref[...], wo_ref[...], preferred_element_type=F32)


def merge(proj, ob, x, conv_w, g_q_x, mk, mv, w_conv_out, w_diff_out, w_x_out, w_o,
          *, batch, cols, tm):
    T, D = x.shape
    S = T // batch
    M = mk.shape[0] // batch
    assert S % tm == 0 and tm % SUBLANES == 0
    tiles_per_seq = S // tm
    halo_blocks = tm // SUBLANES

    def col_spec(name, width):
        assert cols[name] % width == 0
        return pl.BlockSpec((tm, width), lambda i: (i, cols[name] // width))

    def halo_spec(name):
        return pl.BlockSpec((SUBLANES, CONV_WIDTH),
                            lambda i: (jnp.maximum(i * halo_blocks - 1, 0), cols[name] // CONV_WIDTH))

    def resident(shape):
        return pl.BlockSpec(shape, lambda i: (0, 0), pipeline_mode=pl.Buffered(1))

    return pl.pallas_call(
        functools.partial(_merge_kernel, tiles_per_seq=tiles_per_seq),
        grid=(T // tm,),
        in_specs=[
            col_spec("gates", N_BRANCH * D),
            col_spec("cb", CONV_WIDTH),
            col_spec("cc", CONV_WIDTH),
            col_spec("cx", CONV_WIDTH),
            halo_spec("cc"),
            halo_spec("cx"),
            col_spec("xq", X_WIDTH),
            pl.BlockSpec((tm, DIFF_VW), lambda i: (i, 0)),
            pl.BlockSpec((tm, D), lambda i: (i, 0)),
            pl.BlockSpec((CONV_K, CONV_WIDTH), lambda i: (0, 0)),
            pl.BlockSpec((1, X_HEAD_DIM), lambda i: (0, 0)),
            pl.BlockSpec((M, X_WIDTH), lambda i: (i // tiles_per_seq, 0)),
            pl.BlockSpec((M, X_WIDTH), lambda i: (i // tiles_per_seq, 0)),
            resident((CONV_WIDTH, D)),
            resident((DIFF_VW, D)),
            resident((X_WIDTH, D)),
            resident((D, D)),
        ],
        out_specs=pl.BlockSpec((tm, D), lambda i: (i, 0)),
        out_shape=jax.ShapeDtypeStruct((T, D), F32),
        scratch_shapes=[
            pltpu.VMEM((tm, CONV_WIDTH), BF16),
            pltpu.VMEM((tm, X_WIDTH), BF16),
            pltpu.VMEM((tm, D), BF16),
        ],
        compiler_params=_compiler_params(("parallel",)),
        name="merge",
    )(proj, proj, proj, proj, proj, proj, proj, ob, x, conv_w, g_q_x.reshape(1, X_HEAD_DIM),
      mk, mv, w_conv_out, w_diff_out, w_x_out, w_o)


def _ffn_kernel(x_ref, g_ref, wg_ref, wu_ref, wd_ref, o_ref, h_ref):
    @pl.when(pl.program_id(1) == 0)
    def _():
        x = x_ref[...]
        h_ref[...] = _rms_rows(x, g_ref[...]).astype(h_ref.dtype)
        o_ref[...] = x

    h = h_ref[...]
    a = jnp.dot(h, wg_ref[...], preferred_element_type=F32)
    b = jnp.dot(h, wu_ref[...], preferred_element_type=F32)
    act = (a * jax.nn.sigmoid(a) * b).astype(BF16)
    o_ref[...] += jnp.dot(act, wd_ref[...], preferred_element_type=F32)


def ffn(x, g, w_gate_up, w_down, *, tm, tf):
    T, D = x.shape
    d_ff = w_down.shape[0]
    assert T % tm == 0 and d_ff % tf == 0
    nf = d_ff // tf
    return pl.pallas_call(
        _ffn_kernel,
        grid=(T // tm, nf),
        in_specs=[
            pl.BlockSpec((tm, D), lambda i, f: (i, 0)),
            pl.BlockSpec((1, D), lambda i, f: (0, 0)),
            pl.BlockSpec((D, tf), lambda i, f: (0, f)),
            pl.BlockSpec((D, tf), lambda i, f: (0, nf + f)),
            pl.BlockSpec((tf, D), lambda i, f: (f, 0)),
        ],
        out_specs=pl.BlockSpec((tm, D), lambda i, f: (i, 0)),
        out_shape=jax.ShapeDtypeStruct((T, D), F32),
        scratch_shapes=[pltpu.VMEM((tm, D), BF16)],
        compiler_params=_compiler_params(("parallel", "arbitrary")),
        name="ffn",
    )(x, g.reshape(1, D), w_gate_up, w_gate_up, w_down)


def _tile(n, want):
    t = min(n, want)
    while n % t:
        t -= SUBLANES
    return t


def kernel(x, mem, positions, g_mix, w_in, conv_w, w_conv_out, g_q_diff, g_k_diff, lambda_vecs,
           g_subln, w_diff_out, g_mem, w_mem_kv, g_q_x, g_k_x, w_x_out, w_o, g_ffn, w_gate_up, w_down):
    B, S, D = x.shape
    M = mem.shape[1]
    T = B * S
    depth = w_in.shape[0]
    d_ff = w_down.shape[1]

    n_conv = 3 * CONV_WIDTH
    n_diff = 2 * DIFF_QK + DIFF_VW
    gates_at = n_conv + n_diff + X_WIDTH
    n_gates = N_BRANCH * D
    cols = {
        "gates": 0,
        "cb": n_gates,
        "cc": n_gates + CONV_WIDTH,
        "cx": n_gates + 2 * CONV_WIDTH,
        "dq": n_gates + n_conv,
        "dk": n_gates + n_conv + DIFF_QK,
        "dv": n_gates + n_conv + 2 * DIFF_QK,
        "xq": n_gates + n_conv + n_diff,
    }

    half = ROT_HALF
    inv_freq = ROPE_THETA ** (-jnp.arange(half, dtype=F32) / half)
    lane_d = jnp.arange(LANES) % DIFF_D
    inv_freq_lanes = jnp.where(lane_d < ROT, inv_freq[lane_d % half], 0.0).astype(F32).reshape(1, LANES)

    xf = x.reshape(T, D)
    memf = mem.reshape(B * M, D)
    pos = positions.reshape(T, 1)

    tm_proj = _tile(T, 1024)
    tq = _tile(S, 512)
    tk = _tile(tq, 256)
    tm_merge = _tile(S, 256)
    tm_ffn = _tile(T, 512)
    tf = _tile(d_ff, 512)

    for l in range(depth):
        lam_init = 0.8 - 0.6 * math.exp(-0.3 * l)
        w_in_l = jnp.concatenate([w_in[l][:, gates_at:], w_in[l][:, :gates_at]], axis=1).astype(BF16)

        proj = norm_matmul(xf, g_mix[l], w_in_l, tm=tm_proj, tn=1024, out_dtype=BF16)
        qt, k, vt = qk_prep(proj, pos, inv_freq_lanes, g_q_diff[l], g_k_diff[l],
                            q_col=cols["dq"], k_col=cols["dk"], v_col=cols["dv"], tm=tq, tk=tk)
        ob = diff_attn(qt, k, vt, lambda_vecs[l], g_subln[l], batch=B, tq=tq, tk=tk, lam_init=lam_init)

        kv = norm_matmul(memf, g_mem[l], w_mem_kv[l].astype(BF16), tm=B * M, tn=1024, out_dtype=F32)
        mk, mv = mem_prep(kv, g_k_x[l])

        xf = merge(proj, ob, xf, conv_w[l], g_q_x[l], mk, mv,
                   w_conv_out[l].astype(BF16), w_diff_out[l].astype(BF16), w_x_out[l].astype(BF16),
                   w_o[l].astype(BF16), batch=B, cols=cols, tm=tm_merge)
        xf = ffn(xf, g_ffn[l], w_gate_up[l].astype(BF16), w_down[l].astype(BF16), tm=tm_ffn, tf=tf)

    return xf.reshape(B, S, D)
```

```python
import functools
import math

import jax
import jax.numpy as jnp
from jax import lax
from jax.experimental import pallas as pl
from jax.experimental.pallas import tpu as pltpu

EPS = 1e-6
CONV_WIDTH = 1024
CONV_K = 3
DIFF_HEADS = 8
DIFF_D = 64
DIFF_V = 2 * DIFF_D
DIFF_QK = DIFF_HEADS * 2 * DIFF_D
DIFF_VW = DIFF_HEADS * DIFF_V
X_HEADS = 4
X_HEAD_DIM = 256
X_WIDTH = X_HEADS * X_HEAD_DIM
N_BRANCH = 3
ROPE_THETA = 500000.0
ROT_FRAC = 4
ROT = DIFF_D // ROT_FRAC
ROT_HALF = ROT // 2

LANES = 128
SUBLANES = 8
BF16_ROWS = 16
V_ROWS = DIFF_V + BF16_ROWS
Q_SCALE = DIFF_D ** -0.5 * math.log2(math.e)
VMEM_LIMIT_BYTES = 56 * 1024 * 1024

F32 = jnp.float32
BF16 = jnp.bfloat16


def _compiler_params(semantics):
    return pltpu.CompilerParams(dimension_semantics=semantics, vmem_limit_bytes=VMEM_LIMIT_BYTES)


def _rms_rows(t, g):
    ms = jnp.mean(t * t, axis=-1, keepdims=True)
    return t * lax.rsqrt(ms + EPS) * g


def _norm_matmul_kernel(x_ref, g_ref, w_ref, o_ref, h_ref):
    @pl.when(pl.program_id(1) == 0)
    def _():
        h_ref[...] = _rms_rows(x_ref[...], g_ref[...]).astype(h_ref.dtype)

    o_ref[...] = jnp.dot(h_ref[...], w_ref[...], preferred_element_type=F32).astype(o_ref.dtype)


def norm_matmul(x, g, w, *, tm, tn, out_dtype):
    T, D = x.shape
    N = w.shape[1]
    assert T % tm == 0 and N % tn == 0
    return pl.pallas_call(
        _norm_matmul_kernel,
        grid=(T // tm, N // tn),
        in_specs=[
            pl.BlockSpec((tm, D), lambda i, j: (i, 0)),
            pl.BlockSpec((1, D), lambda i, j: (0, 0)),
            pl.BlockSpec((D, tn), lambda i, j: (0, j)),
        ],
        out_specs=pl.BlockSpec((tm, tn), lambda i, j: (i, j)),
        out_shape=jax.ShapeDtypeStruct((T, N), out_dtype),
        scratch_shapes=[pltpu.VMEM((tm, D), BF16)],
        compiler_params=_compiler_params(("parallel", "arbitrary")),
        name="norm_matmul",
    )(x, g.reshape(1, D), w)


def _group_mean_sq(t, gsum):
    sq = t * t
    hi = sq.astype(BF16)
    lo = (sq - hi.astype(F32)).astype(BF16)
    s = jnp.dot(hi, gsum, preferred_element_type=F32) + jnp.dot(lo, gsum, preferred_element_type=F32)
    return s * (1.0 / DIFF_D)


def _qk_prep_kernel(pos_ref, invf_ref, gq_ref, gk_ref, q_ref, k_ref, v_ref, qt_ref, ko_ref, vt_ref,
                    *, tk):
    tm = q_ref.shape[0]
    lane = lax.broadcasted_iota(jnp.int32, (1, LANES), 1)
    d = lane & (DIFF_D - 1)
    ang = pos_ref[...].astype(F32) * invf_ref[...]
    cos = jnp.cos(ang)
    sin = jnp.sin(ang)
    c_keep = jnp.where(d < ROT, cos, 1.0)
    s_from_hi = jnp.where(d < ROT_HALF, -sin, 0.0)
    s_from_lo = jnp.where((d >= ROT_HALF) & (d < ROT), sin, 0.0)

    row = lax.broadcasted_iota(jnp.int32, (LANES, LANES), 0)
    col = lax.broadcasted_iota(jnp.int32, (LANES, LANES), 1)
    gsum = jnp.where((row < DIFF_D) == (col < DIFF_D), 1.0, 0.0).astype(BF16)

    def norm_rot(src_ref, g, c):
        t = src_ref[:, c * LANES:(c + 1) * LANES].astype(F32)
        tn = t * lax.rsqrt(_group_mean_sq(t, gsum) + EPS) * g
        up = pltpu.roll(tn, LANES - ROT_HALF, axis=1)
        dn = pltpu.roll(tn, ROT_HALF, axis=1)
        return tn * c_keep + up * s_from_hi + dn * s_from_lo

    gq = gq_ref[...]
    gk = gk_ref[...]
    pad_row = lax.broadcasted_iota(jnp.int32, (V_ROWS - DIFF_V, tk), 0)
    ones_row = jnp.where(pad_row == 0, 1.0, 0.0).astype(vt_ref.dtype)
    for h in range(DIFF_HEADS):
        sl = slice(h * LANES, (h + 1) * LANES)
        q = norm_rot(q_ref, gq, h) * Q_SCALE
        qt_ref[h, 0] = q.T.astype(qt_ref.dtype)
        ko_ref[:, sl] = norm_rot(k_ref, gk, h).astype(ko_ref.dtype)
        vt = v_ref[:, sl].astype(F32).T
        for u in range(tm // tk):
            vt_ref[h, u, :DIFF_V] = vt[:, u * tk:(u + 1) * tk].astype(vt_ref.dtype)
            vt_ref[h, u, DIFF_V:] = ones_row


def qk_prep(proj, pos, inv_freq_lanes, g_q, g_k, *, q_col, k_col, v_col, tm, tk):
    T = proj.shape[0]
    assert T % tm == 0 and tm % tk == 0
    assert q_col % DIFF_QK == 0 and k_col % DIFF_QK == 0 and v_col % DIFF_VW == 0 and DIFF_V == LANES
    gq = jnp.tile(g_q, LANES // DIFF_D).reshape(1, LANES)
    gk = jnp.tile(g_k, LANES // DIFF_D).reshape(1, LANES)
    return pl.pallas_call(
        functools.partial(_qk_prep_kernel, tk=tk),
        grid=(T // tm,),
        in_specs=[
            pl.BlockSpec((tm, 1), lambda i: (i, 0)),
            pl.BlockSpec((1, LANES), lambda i: (0, 0)),
            pl.BlockSpec((1, LANES), lambda i: (0, 0)),
            pl.BlockSpec((1, LANES), lambda i: (0, 0)),
            pl.BlockSpec((tm, DIFF_QK), lambda i: (i, q_col // DIFF_QK)),
            pl.BlockSpec((tm, DIFF_QK), lambda i: (i, k_col // DIFF_QK)),
            pl.BlockSpec((tm, DIFF_VW), lambda i: (i, v_col // DIFF_VW)),
        ],
        out_specs=[
            pl.BlockSpec((DIFF_HEADS, 1, LANES, tm), lambda i: (0, i, 0, 0)),
            pl.BlockSpec((tm, DIFF_QK), lambda i: (i, 0)),
            pl.BlockSpec((DIFF_HEADS, tm // tk, V_ROWS, tk), lambda i: (0, i, 0, 0)),
        ],
        out_shape=[
            jax.ShapeDtypeStruct((DIFF_HEADS, T // tm, LANES, tm), BF16),
            jax.ShapeDtypeStruct((T, DIFF_QK), BF16),
            jax.ShapeDtypeStruct((DIFF_HEADS, T // tk, V_ROWS, tk), BF16),
        ],
        compiler_params=_compiler_params(("parallel",)),
        name="qk_prep",
    )(pos, inv_freq_lanes, gq, gk, proj, proj, proj)


def _diff_attn_kernel(lam_ref, g_ref, qt_ref, k_ref, vt_ref, o_ref, m_ref, acc_ref,
                      s0_ref, s1_ref, p0_ref, p1_ref, a0_ref, a1_ref, bm0_ref, bm1_ref,
                      *, tq, tk, lam_init):
    assert tq == 2 * tk
    s_bufs, p_bufs, a_bufs, bm_bufs = (s0_ref, s1_ref), (p0_ref, p1_ref), (a0_ref, a1_ref), (bm0_ref, bm1_ref)
    i = pl.program_id(2)
    n_blocks = 2 * i + 2
    qt = qt_ref[0, 0]
    zeros = jnp.zeros((DIFF_D, tq), qt.dtype)
    q_comp = (jnp.concatenate([qt[:DIFF_D], zeros], axis=0),
              jnp.concatenate([zeros, qt[DIFF_D:]], axis=0))

    m_ref[...] = jnp.full(m_ref.shape, -jnp.inf, F32)
    acc_ref[...] = jnp.zeros(acc_ref.shape, F32)

    def scores(u, slot, masked):
        k = k_ref[pl.ds(pl.multiple_of(u * tk, tk), tk), :]
        if masked:
            kpos = u * tk + lax.broadcasted_iota(jnp.int32, (tk, tq), 0)
            qpos = i * tq + lax.broadcasted_iota(jnp.int32, (tk, tq), 1)
            causal = kpos <= qpos
        for c in range(2):
            s = jnp.dot(k, q_comp[c], preferred_element_type=F32)
            if masked:
                s = jnp.where(causal, s, -jnp.inf)
            s_bufs[slot][c] = s
            bm_bufs[slot][c] = jnp.max(s, axis=0, keepdims=True)

    def softmax(slot):
        for c in range(2):
            m_prev = m_ref[c]
            m_new = jnp.maximum(m_prev, bm_bufs[slot][c])
            a_bufs[slot][c] = jnp.exp2(m_prev - m_new)
            m_ref[c] = m_new
            p_bufs[slot][c] = jnp.exp2(s_bufs[slot][c] - m_new).astype(p_bufs[slot].dtype)

    def accumulate(u, slot):
        vt = vt_ref[0, u]
        for c in range(2):
            pv = jnp.dot(vt, p_bufs[slot][c], preferred_element_type=F32)
            acc_ref[c] = a_bufs[slot][c] * acc_ref[c] + pv

    scores(0, 0, masked=True)
    scores(1, 1, masked=True)
    softmax(0)

    def step_pair(u, masked):
        scores(u, 0, masked)
        accumulate(u - 2, 0)
        softmax(1)
        scores(u + 1, 1, masked)
        accumulate(u - 1, 1)
        softmax(0)

    def full_pair(r, carry):
        step_pair(2 * r, masked=False)
        return carry

    lax.fori_loop(1, i, full_pair, 0)

    @pl.when(i >= 1)
    def _():
        step_pair(2 * i, masked=True)

    accumulate(n_blocks - 2, 0)
    softmax(1)
    accumulate(n_blocks - 1, 1)

    lv = lam_ref[...]
    e1 = jnp.exp(jnp.sum(lv[0:1, :] * lv[1:2, :], axis=-1, keepdims=True))
    e2 = jnp.exp(jnp.sum(lv[2:3, :] * lv[3:4, :], axis=-1, keepdims=True))
    lam = e1 - e2 + lam_init
    o = (acc_ref[0, :DIFF_V] / acc_ref[0, DIFF_V:DIFF_V + 1]
         - lam * (acc_ref[1, :DIFF_V] / acc_ref[1, DIFF_V:DIFF_V + 1]))
    o = o * lax.rsqrt(jnp.mean(o * o, axis=0, keepdims=True) + EPS)
    o_ref[...] = (o.T * g_ref[...] * (1.0 - lam_init)).astype(o_ref.dtype)


def diff_attn(qt, k, vt, lambda_vecs, g_subln, *, batch, tq, tk, lam_init):
    T = k.shape[0]
    S = T // batch
    assert S % tq == 0 and tq == 2 * tk
    nq, nk = S // tq, S // tk
    stat = pltpu.VMEM((2, 1, tq), F32)
    return pl.pallas_call(
        functools.partial(_diff_attn_kernel, tq=tq, tk=tk, lam_init=lam_init),
        grid=(batch, DIFF_HEADS, nq),
        in_specs=[
            pl.BlockSpec((4, DIFF_D), lambda b, h, i: (0, 0)),
            pl.BlockSpec((1, DIFF_V), lambda b, h, i: (0, 0)),
            pl.BlockSpec((1, 1, LANES, tq), lambda b, h, i: (h, b * nq + i, 0, 0)),
            pl.BlockSpec((S, LANES), lambda b, h, i: (b, h)),
            pl.BlockSpec((1, nk, V_ROWS, tk), lambda b, h, i: (h, b, 0, 0)),
        ],
        out_specs=pl.BlockSpec((tq, DIFF_V), lambda b, h, i: (b * nq + i, h)),
        out_shape=jax.ShapeDtypeStruct((T, DIFF_VW), BF16),
        scratch_shapes=[
            stat,
            pltpu.VMEM((2, V_ROWS, tq), F32),
            pltpu.VMEM((2, tk, tq), F32),
            pltpu.VMEM((2, tk, tq), F32),
            pltpu.VMEM((2, tk, tq), BF16),
            pltpu.VMEM((2, tk, tq), BF16),
            stat, stat,
            stat, stat,
        ],
        compiler_params=_compiler_params(("parallel", "parallel", "arbitrary")),
        name="diff_attn",
    )(lambda_vecs, g_subln.reshape(1, DIFF_V), qt, k, vt)


def _mem_prep_kernel(kv_ref, g_ref, mk_ref, mv_ref):
    g = g_ref[...]
    for h in range(X_HEADS):
        sl = slice(h * X_HEAD_DIM, (h + 1) * X_HEAD_DIM)
        kh = _rms_rows(kv_ref[:, sl], g) * (X_HEAD_DIM ** -0.5)
        mk_ref[:, sl] = kh.astype(mk_ref.dtype)
    mv_ref[...] = kv_ref[:, X_WIDTH:].astype(mv_ref.dtype)


def mem_prep(kv, g_k_x):
    R = kv.shape[0]
    out = jax.ShapeDtypeStruct((R, X_WIDTH), BF16)
    return pl.pallas_call(
        _mem_prep_kernel,
        grid=(1,),
        in_specs=[
            pl.BlockSpec((R, 2 * X_WIDTH), lambda i: (0, 0)),
            pl.BlockSpec((1, X_HEAD_DIM), lambda i: (0, 0)),
        ],
        out_specs=[pl.BlockSpec((R, X_WIDTH), lambda i: (0, 0))] * 2,
        out_shape=[out, out],
        compiler_params=_compiler_params(("arbitrary",)),
        name="mem_prep",
    )(kv, g_k_x.reshape(1, X_HEAD_DIM))


MERGE_CHUNK = 512


def _merge_kernel(gates_ref, cb_ref, cc_ref, cx_ref, cch_ref, cxh_ref, xq_ref, ob_ref, x_ref,
                  convw_ref, gqx_ref, mk_ref, mv_ref, wa_ref, wb_ref, wc_ref, wo_ref,
                  o_ref, ua_ref, oc_ref, mg_ref, *, tiles_per_seq):
    tm = x_ref.shape[0]
    d_model = x_ref.shape[1]
    first = pl.program_id(0) % tiles_per_seq == 0

    z = cc_ref[...].astype(F32) * cx_ref[...].astype(F32)
    zp = cch_ref[...].astype(F32) * cxh_ref[...].astype(F32)
    zp = jnp.where(first, 0.0, zp)
    w = convw_ref[...]
    row8 = lax.broadcasted_iota(jnp.int32, (SUBLANES, 1), 0)
    y = z * w[CONV_K - 1:CONV_K, :]
    for back in range(1, CONV_K):
        zs = pltpu.roll(z, back, axis=0)
        ps = pltpu.roll(zp, back, axis=0)
        head = jnp.where(row8 < back, ps, zs[:SUBLANES])
        zs = jnp.concatenate([head, zs[SUBLANES:]], axis=0)
        y = y + zs * w[CONV_K - 1 - back:CONV_K - back, :]
    ua_ref[...] = (cb_ref[...].astype(F32) * y).astype(ua_ref.dtype)

    gq = gqx_ref[...]
    for h in range(X_HEADS):
        sl = slice(h * X_HEAD_DIM, (h + 1) * X_HEAD_DIM)
        qn = _rms_rows(xq_ref[:, sl].astype(F32), gq).astype(BF16)
        s = lax.dot_general(qn, mk_ref[:, sl], (((1,), (1,)), ((), ())), preferred_element_type=F32)
        s = s - jnp.max(s, axis=-1, keepdims=True)
        p = jnp.exp(s)
        p = p / jnp.sum(p, axis=-1, keepdims=True)
        oc_ref[:, sl] = jnp.dot(p.astype(BF16), mv_ref[:, sl], preferred_element_type=F32).astype(oc_ref.dtype)

    ua = ua_ref[...]
    ob = ob_ref[...]
    oc = oc_ref[...]
    for c in range(d_model // MERGE_CHUNK):
        sl = slice(c * MERGE_CHUNK, (c + 1) * MERGE_CHUNK)
        merged = jnp.zeros((tm, MERGE_CHUNK), F32)
        for br, (act, w_ref) in enumerate(((ua, wa_ref), (ob, wb_ref), (oc, wc_ref))):
            gsl = slice(br * d_model + c * MERGE_CHUNK, br * d_model + (c + 1) * MERGE_CHUNK)
            gate = jax.nn.sigmoid(gates_ref[:, gsl].astype(F32))
            merged = merged + gate * jnp.dot(act, w_ref[:, sl], preferred_element_type=F32)
        mg_ref[:, sl] = merged.astype(mg_ref.dtype)

    o_ref[...] = x_ref[...] + jnp.dot(mg_ref[...], wo_ref[...], preferred_element_type=F32)


def merge(proj, ob, x, conv_w, g_q_x, mk, mv, w_conv_out, w_diff_out, w_x_out, w_o,
          *, batch, cols, tm):
    T, D = x.shape
    S = T // batch
    M = mk.shape[0] // batch
    assert S % tm == 0 and tm % SUBLANES == 0
    tiles_per_seq = S // tm
    halo_blocks = tm // SUBLANES

    def col_spec(name, width):
        assert cols[name] % width == 0
        return pl.BlockSpec((tm, width), lambda i: (i, cols[name] // width))

    def halo_spec(name):
        return pl.BlockSpec((SUBLANES, CONV_WIDTH),
                            lambda i: (jnp.maximum(i * halo_blocks - 1, 0), cols[name] // CONV_WIDTH))

    def resident(shape):
        return pl.BlockSpec(shape, lambda i: (0, 0), pipeline_mode=pl.Buffered(1))

    return pl.pallas_call(
        functools.partial(_merge_kernel, tiles_per_seq=tiles_per_seq),
        grid=(T // tm,),
        in_specs=[
            col_spec("gates", N_BRANCH * D),
            col_spec("cb", CONV_WIDTH),
            col_spec("cc", CONV_WIDTH),
            col_spec("cx", CONV_WIDTH),
            halo_spec("cc"),
            halo_spec("cx"),
            col_spec("xq", X_WIDTH),
            pl.BlockSpec((tm, DIFF_VW), lambda i: (i, 0)),
            pl.BlockSpec((tm, D), lambda i: (i, 0)),
            pl.BlockSpec((CONV_K, CONV_WIDTH), lambda i: (0, 0)),
            pl.BlockSpec((1, X_HEAD_DIM), lambda i: (0, 0)),
            pl.BlockSpec((M, X_WIDTH), lambda i: (i // tiles_per_seq, 0)),
            pl.BlockSpec((M, X_WIDTH), lambda i: (i // tiles_per_seq, 0)),
            resident((CONV_WIDTH, D)),
            resident((DIFF_VW, D)),
            resident((X_WIDTH, D)),
            resident((D, D)),
        ],
        out_specs=pl.BlockSpec((tm, D), lambda i: (i, 0)),
        out_shape=jax.ShapeDtypeStruct((T, D), F32),
        scratch_shapes=[
            pltpu.VMEM((tm, CONV_WIDTH), BF16),
            pltpu.VMEM((tm, X_WIDTH), BF16),
            pltpu.VMEM((tm, D), BF16),
        ],
        compiler_params=_compiler_params(("parallel",)),
        name="merge",
    )(proj, proj, proj, proj, proj, proj, proj, ob, x, conv_w, g_q_x.reshape(1, X_HEAD_DIM),
      mk, mv, w_conv_out, w_diff_out, w_x_out, w_o)


def _ffn_kernel(x_ref, g_ref, wg_ref, wu_ref, wd_ref, o_ref, h_ref):
    @pl.when(pl.program_id(1) == 0)
    def _():
        x = x_ref[...]
        h_ref[...] = _rms_rows(x, g_ref[...]).astype(h_ref.dtype)
        o_ref[...] = x

    h = h_ref[...]
    a = jnp.dot(h, wg_ref[...], preferred_element_type=F32)
    b = jnp.dot(h, wu_ref[...], preferred_element_type=F32)
    act = (a * jax.nn.sigmoid(a) * b).astype(BF16)
    o_ref[...] += jnp.dot(act, wd_ref[...], preferred_element_type=F32)


def ffn(x, g, w_gate_up, w_down, *, tm, tf):
    T, D = x.shape
    d_ff = w_down.shape[0]
    assert T % tm == 0 and d_ff % tf == 0
    nf = d_ff // tf
    return pl.pallas_call(
        _ffn_kernel,
        grid=(T // tm, nf),
        in_specs=[
            pl.BlockSpec((tm, D), lambda i, f: (i, 0)),
            pl.BlockSpec((1, D), lambda i, f: (0, 0)),
            pl.BlockSpec((D, tf), lambda i, f: (0, f)),
            pl.BlockSpec((D, tf), lambda i, f: (0, nf + f)),
            pl.BlockSpec((tf, D), lambda i, f: (f, 0)),
        ],
        out_specs=pl.BlockSpec((tm, D), lambda i, f: (i, 0)),
        out_shape=jax.ShapeDtypeStruct((T, D), F32),
        scratch_shapes=[pltpu.VMEM((tm, D), BF16)],
        compiler_params=_compiler_params(("parallel", "arbitrary")),
        name="ffn",
    )(x, g.reshape(1, D), w_gate_up, w_gate_up, w_down)


def _tile(n, want):
    t = min(n, want)
    while n % t:
        t -= SUBLANES
    return t


def kernel(x, mem, positions, g_mix, w_in, conv_w, w_conv_out, g_q_diff, g_k_diff, lambda_vecs,
           g_subln, w_diff_out, g_mem, w_mem_kv, g_q_x, g_k_x, w_x_out, w_o, g_ffn, w_gate_up, w_down):
    B, S, D = x.shape
    M = mem.shape[1]
    T = B * S
    depth = w_in.shape[0]
    d_ff = w_down.shape[1]

    n_conv = 3 * CONV_WIDTH
    n_diff = 2 * DIFF_QK + DIFF_VW
    gates_at = n_conv + n_diff + X_WIDTH
    n_gates = N_BRANCH * D
    cols = {
        "gates": 0,
        "cb": n_gates,
        "cc": n_gates + CONV_WIDTH,
        "cx": n_gates + 2 * CONV_WIDTH,
        "dq": n_gates + n_conv,
        "dk": n_gates + n_conv + DIFF_QK,
        "dv": n_gates + n_conv + 2 * DIFF_QK,
        "xq": n_gates + n_conv + n_diff,
    }

    half = ROT_HALF
    inv_freq = ROPE_THETA ** (-jnp.arange(half, dtype=F32) / half)
    lane_d = jnp.arange(LANES) % DIFF_D
    inv_freq_lanes = jnp.where(lane_d < ROT, inv_freq[lane_d % half], 0.0).astype(F32).reshape(1, LANES)

    xf = x.reshape(T, D)
    memf = mem.reshape(B * M, D)
    pos = positions.reshape(T, 1)

    tm_proj = _tile(T, 1024)
    tq = _tile(S, 1024)
    tk = tq // 2
    tm_merge = _tile(S, 256)
    tm_ffn = _tile(T, 512)
    tf = _tile(d_ff, 512)

    for l in range(depth):
        lam_init = 0.8 - 0.6 * math.exp(-0.3 * l)
        w_in_l = jnp.concatenate([w_in[l][:, gates_at:], w_in[l][:, :gates_at]], axis=1).astype(BF16)

        proj = norm_matmul(xf, g_mix[l], w_in_l, tm=tm_proj, tn=1024, out_dtype=BF16)
        qt, k, vt = qk_prep(proj, pos, inv_freq_lanes, g_q_diff[l], g_k_diff[l],
                            q_col=cols["dq"], k_col=cols["dk"], v_col=cols["dv"], tm=tq, tk=tk)
        ob = diff_attn(qt, k, vt, lambda_vecs[l], g_subln[l], batch=B, tq=tq, tk=tk, lam_init=lam_init)

        kv = norm_matmul(memf, g_mem[l], w_mem_kv[l].astype(BF16), tm=B * M, tn=1024, out_dtype=F32)
        mk, mv = mem_prep(kv, g_k_x[l])

        xf = merge(proj, ob, xf, conv_w[l], g_q_x[l], mk, mv,
                   w_conv_out[l].astype(BF16), w_diff_out[l].astype(BF16), w_x_out[l].astype(BF16),
                   w_o[l].astype(BF16), batch=B, cols=cols, tm=tm_merge)
        xf = ffn(xf, g_ffn[l], w_gate_up[l].astype(BF16), w_down[l].astype(BF16), tm=tm_ffn, tf=tf)

    return xf.reshape(B, S, D)
```

```python
import functools
import math

import jax
import jax.numpy as jnp
from jax import lax
from jax.experimental import pallas as pl
from jax.experimental.pallas import tpu as pltpu

EPS = 1e-6
CONV_WIDTH = 1024
CONV_K = 3
DIFF_HEADS = 8
DIFF_D = 64
DIFF_V = 2 * DIFF_D
DIFF_QK = DIFF_HEADS * 2 * DIFF_D
DIFF_VW = DIFF_HEADS * DIFF_V
X_HEADS = 4
X_HEAD_DIM = 256
X_WIDTH = X_HEADS * X_HEAD_DIM
N_BRANCH = 3
ROPE_THETA = 500000.0
ROT_FRAC = 4
ROT = DIFF_D // ROT_FRAC
ROT_HALF = ROT // 2

LANES = 128
SUBLANES = 8
BF16_ROWS = 16
V_ROWS = DIFF_V + BF16_ROWS
Q_SCALE = DIFF_D ** -0.5 * math.log2(math.e)
VMEM_LIMIT_BYTES = 56 * 1024 * 1024

F32 = jnp.float32
BF16 = jnp.bfloat16


def _compiler_params(semantics):
    return pltpu.CompilerParams(dimension_semantics=semantics, vmem_limit_bytes=VMEM_LIMIT_BYTES)


def _rms_rows(t, g):
    ms = jnp.mean(t * t, axis=-1, keepdims=True)
    return t * lax.rsqrt(ms + EPS) * g


def _norm_matmul_kernel(x_ref, g_ref, w_ref, o_ref, h_ref):
    @pl.when(pl.program_id(1) == 0)
    def _():
        h_ref[...] = _rms_rows(x_ref[...], g_ref[...]).astype(h_ref.dtype)

    o_ref[...] = jnp.dot(h_ref[...], w_ref[...], preferred_element_type=F32).astype(o_ref.dtype)


def norm_matmul(x, g, w, *, tm, tn, out_dtype):
    T, D = x.shape
    N = w.shape[1]
    assert T % tm == 0 and N % tn == 0
    return pl.pallas_call(
        _norm_matmul_kernel,
        grid=(T // tm, N // tn),
        in_specs=[
            pl.BlockSpec((tm, D), lambda i, j: (i, 0)),
            pl.BlockSpec((1, D), lambda i, j: (0, 0)),
            pl.BlockSpec((D, tn), lambda i, j: (0, j)),
        ],
        out_specs=pl.BlockSpec((tm, tn), lambda i, j: (i, j)),
        out_shape=jax.ShapeDtypeStruct((T, N), out_dtype),
        scratch_shapes=[pltpu.VMEM((tm, D), BF16)],
        compiler_params=_compiler_params(("parallel", "arbitrary")),
        name="norm_matmul",
    )(x, g.reshape(1, D), w)


def _group_mean_sq(t, gsum):
    sq = t * t
    hi = sq.astype(BF16)
    lo = (sq - hi.astype(F32)).astype(BF16)
    s = jnp.dot(hi, gsum, preferred_element_type=F32) + jnp.dot(lo, gsum, preferred_element_type=F32)
    return s * (1.0 / DIFF_D)


def _qk_prep_kernel(pos_ref, invf_ref, gq_ref, gk_ref, q_ref, k_ref, v_ref, qt_ref, ko_ref, vt_ref,
                    *, tk):
    tm = q_ref.shape[0]
    lane = lax.broadcasted_iota(jnp.int32, (1, LANES), 1)
    d = lane & (DIFF_D - 1)
    ang = pos_ref[...].astype(F32) * invf_ref[...]
    cos = jnp.cos(ang)
    sin = jnp.sin(ang)
    c_keep = jnp.where(d < ROT, cos, 1.0)
    s_from_hi = jnp.where(d < ROT_HALF, -sin, 0.0)
    s_from_lo = jnp.where((d >= ROT_HALF) & (d < ROT), sin, 0.0)

    row = lax.broadcasted_iota(jnp.int32, (LANES, LANES), 0)
    col = lax.broadcasted_iota(jnp.int32, (LANES, LANES), 1)
    gsum = jnp.where((row < DIFF_D) == (col < DIFF_D), 1.0, 0.0).astype(BF16)

    def norm_rot(src_ref, g, c):
        t = src_ref[:, c * LANES:(c + 1) * LANES].astype(F32)
        tn = t * lax.rsqrt(_group_mean_sq(t, gsum) + EPS) * g
        up = pltpu.roll(tn, LANES - ROT_HALF, axis=1)
        dn = pltpu.roll(tn, ROT_HALF, axis=1)
        return tn * c_keep + up * s_from_hi + dn * s_from_lo

    gq = gq_ref[...]
    gk = gk_ref[...]
    pad_row = lax.broadcasted_iota(jnp.int32, (V_ROWS - DIFF_V, tk), 0)
    ones_row = jnp.where(pad_row == 0, 1.0, 0.0).astype(vt_ref.dtype)
    for h in range(DIFF_HEADS):
        sl = slice(h * LANES, (h + 1) * LANES)
        q = norm_rot(q_ref, gq, h) * Q_SCALE
        qt_ref[h, 0] = q.T.astype(qt_ref.dtype)
        ko_ref[:, sl] = norm_rot(k_ref, gk, h).astype(ko_ref.dtype)
        vt = v_ref[:, sl].astype(F32).T
        for u in range(tm // tk):
            vt_ref[h, u, :DIFF_V] = vt[:, u * tk:(u + 1) * tk].astype(vt_ref.dtype)
            vt_ref[h, u, DIFF_V:] = ones_row


def qk_prep(proj, pos, inv_freq_lanes, g_q, g_k, *, q_col, k_col, v_col, tm, tk):
    T = proj.shape[0]
    assert T % tm == 0 and tm % tk == 0
    assert q_col % DIFF_QK == 0 and k_col % DIFF_QK == 0 and v_col % DIFF_VW == 0 and DIFF_V == LANES
    gq = jnp.tile(g_q, LANES // DIFF_D).reshape(1, LANES)
    gk = jnp.tile(g_k, LANES // DIFF_D).reshape(1, LANES)
    return pl.pallas_call(
        functools.partial(_qk_prep_kernel, tk=tk),
        grid=(T // tm,),
        in_specs=[
            pl.BlockSpec((tm, 1), lambda i: (i, 0)),
            pl.BlockSpec((1, LANES), lambda i: (0, 0)),
            pl.BlockSpec((1, LANES), lambda i: (0, 0)),
            pl.BlockSpec((1, LANES), lambda i: (0, 0)),
            pl.BlockSpec((tm, DIFF_QK), lambda i: (i, q_col // DIFF_QK)),
            pl.BlockSpec((tm, DIFF_QK), lambda i: (i, k_col // DIFF_QK)),
            pl.BlockSpec((tm, DIFF_VW), lambda i: (i, v_col // DIFF_VW)),
        ],
        out_specs=[
            pl.BlockSpec((DIFF_HEADS, 1, LANES, tm), lambda i: (0, i, 0, 0)),
            pl.BlockSpec((tm, DIFF_QK), lambda i: (i, 0)),
            pl.BlockSpec((DIFF_HEADS, tm // tk, V_ROWS, tk), lambda i: (0, i, 0, 0)),
        ],
        out_shape=[
            jax.ShapeDtypeStruct((DIFF_HEADS, T // tm, LANES, tm), BF16),
            jax.ShapeDtypeStruct((T, DIFF_QK), BF16),
            jax.ShapeDtypeStruct((DIFF_HEADS, T // tk, V_ROWS, tk), BF16),
        ],
        compiler_params=_compiler_params(("parallel",)),
        name="qk_prep",
    )(pos, inv_freq_lanes, gq, gk, proj, proj, proj)


def _diff_attn_kernel(lam_ref, g_ref, qt_ref, k_ref, vt_ref, o_ref, m_ref, acc_ref,
                      s0_ref, s1_ref, p0_ref, p1_ref, a0_ref, a1_ref, bm0_ref, bm1_ref,
                      *, tq, tk, lam_init):
    assert tq == 2 * tk
    s_bufs, p_bufs, a_bufs, bm_bufs = (s0_ref, s1_ref), (p0_ref, p1_ref), (a0_ref, a1_ref), (bm0_ref, bm1_ref)
    i = pl.program_id(2)
    n_blocks = 2 * i + 2
    qt = qt_ref[0, 0]
    zeros = jnp.zeros((DIFF_D, tq), qt.dtype)
    q_comp = (jnp.concatenate([qt[:DIFF_D], zeros], axis=0),
              jnp.concatenate([zeros, qt[DIFF_D:]], axis=0))

    m_ref[...] = jnp.full(m_ref.shape, -jnp.inf, F32)
    acc_ref[...] = jnp.zeros(acc_ref.shape, F32)

    def scores(u, slot, masked):
        k = k_ref[pl.ds(pl.multiple_of(u * tk, tk), tk), :]
        if masked:
            kpos = u * tk + lax.broadcasted_iota(jnp.int32, (tk, tq), 0)
            qpos = i * tq + lax.broadcasted_iota(jnp.int32, (tk, tq), 1)
            causal = kpos <= qpos
        for c in range(2):
            s = jnp.dot(k, q_comp[c], preferred_element_type=F32)
            if masked:
                s = jnp.where(causal, s, -jnp.inf)
            s_bufs[slot][c] = s
            bm_bufs[slot][c] = jnp.max(s, axis=0, keepdims=True)

    def softmax(slot):
        for c in range(2):
            m_prev = m_ref[c]
            m_new = jnp.maximum(m_prev, bm_bufs[slot][c])
            a_bufs[slot][c] = jnp.exp2(m_prev - m_new)
            m_ref[c] = m_new
            p_bufs[slot][c] = jnp.exp2(s_bufs[slot][c] - m_new).astype(p_bufs[slot].dtype)

    def accumulate(u, slot):
        vt = vt_ref[0, u]
        for c in range(2):
            pv = jnp.dot(vt, p_bufs[slot][c], preferred_element_type=F32)
            acc_ref[c] = a_bufs[slot][c] * acc_ref[c] + pv

    scores(0, 0, masked=True)
    scores(1, 1, masked=True)
    softmax(0)

    def step_pair(u, masked):
        scores(u, 0, masked)
        accumulate(u - 2, 0)
        softmax(1)
        scores(u + 1, 1, masked)
        accumulate(u - 1, 1)
        softmax(0)

    def full_pair(r, carry):
        step_pair(2 * r, masked=False)
        return carry

    lax.fori_loop(1, i, full_pair, 0)

    @pl.when(i >= 1)
    def _():
        step_pair(2 * i, masked=True)

    accumulate(n_blocks - 2, 0)
    softmax(1)
    accumulate(n_blocks - 1, 1)

    lv = lam_ref[...]
    e1 = jnp.exp(jnp.sum(lv[0:1, :] * lv[1:2, :], axis=-1, keepdims=True))
    e2 = jnp.exp(jnp.sum(lv[2:3, :] * lv[3:4, :], axis=-1, keepdims=True))
    lam = e1 - e2 + lam_init
    o = (acc_ref[0, :DIFF_V] / acc_ref[0, DIFF_V:DIFF_V + 1]
         - lam * (acc_ref[1, :DIFF_V] / acc_ref[1, DIFF_V:DIFF_V + 1]))
    o = o * lax.rsqrt(jnp.mean(o * o, axis=0, keepdims=True) + EPS)
    o_ref[...] = (o.T * g_ref[...] * (1.0 - lam_init)).astype(o_ref.dtype)


def diff_attn(qt, k, vt, lambda_vecs, g_subln, *, batch, tq, tk, lam_init):
    T = k.shape[0]
    S = T // batch
    assert S % tq == 0 and tq == 2 * tk
    nq, nk = S // tq, S // tk
    stat = pltpu.VMEM((2, 1, tq), F32)
    return pl.pallas_call(
        functools.partial(_diff_attn_kernel, tq=tq, tk=tk, lam_init=lam_init),
        grid=(batch, DIFF_HEADS, nq),
        in_specs=[
            pl.BlockSpec((4, DIFF_D), lambda b, h, i: (0, 0)),
            pl.BlockSpec((1, DIFF_V), lambda b, h, i: (0, 0)),
            pl.BlockSpec((1, 1, LANES, tq), lambda b, h, i: (h, b * nq + i, 0, 0)),
            pl.BlockSpec((S, LANES), lambda b, h, i: (b, h)),
            pl.BlockSpec((1, nk, V_ROWS, tk), lambda b, h, i: (h, b, 0, 0)),
        ],
        out_specs=pl.BlockSpec((tq, DIFF_V), lambda b, h, i: (b * nq + i, h)),
        out_shape=jax.ShapeDtypeStruct((T, DIFF_VW), BF16),
        scratch_shapes=[
            stat,
            pltpu.VMEM((2, V_ROWS, tq), F32),
            pltpu.VMEM((2, tk, tq), F32),
            pltpu.VMEM((2, tk, tq), F32),
            pltpu.VMEM((2, tk, tq), BF16),
            pltpu.VMEM((2, tk, tq), BF16),
            stat, stat,
            stat, stat,
        ],
        compiler_params=_compiler_params(("parallel", "parallel", "arbitrary")),
        name="diff_attn",
    )(lambda_vecs, g_subln.reshape(1, DIFF_V), qt, k, vt)


def _mem_prep_kernel(kv_ref, g_ref, mk_ref, mv_ref):
    g = g_ref[...]
    for h in range(X_HEADS):
        sl = slice(h * X_HEAD_DIM, (h + 1) * X_HEAD_DIM)
        kh = _rms_rows(kv_ref[:, sl], g) * (X_HEAD_DIM ** -0.5)
        mk_ref[:, sl] = kh.astype(mk_ref.dtype)
    mv_ref[...] = kv_ref[:, X_WIDTH:].astype(mv_ref.dtype)


def mem_prep(kv, g_k_x):
    R = kv.shape[0]
    out = jax.ShapeDtypeStruct((R, X_WIDTH), BF16)
    return pl.pallas_call(
        _mem_prep_kernel,
        grid=(1,),
        in_specs=[
            pl.BlockSpec((R, 2 * X_WIDTH), lambda i: (0, 0)),
            pl.BlockSpec((1, X_HEAD_DIM), lambda i: (0, 0)),
        ],
        out_specs=[pl.BlockSpec((R, X_WIDTH), lambda i: (0, 0))] * 2,
        out_shape=[out, out],
        compiler_params=_compiler_params(("arbitrary",)),
        name="mem_prep",
    )(kv, g_k_x.reshape(1, X_HEAD_DIM))


MERGE_CHUNK = 512
GATE_BLOCK = 1024
assert GATE_BLOCK % MERGE_CHUNK == 0


def _merge_kernel(*refs, tiles_per_seq, n_gate_blocks):
    gate_refs = refs[:n_gate_blocks]
    (cb_ref, cc_ref, cx_ref, cch_ref, cxh_ref, xq_ref, ob_ref, x_ref,
     convw_ref, gqx_ref, mk_ref, mv_ref, wa_ref, wb_ref, wc_ref, wo_ref,
     o_ref, ua_ref, oc_ref, mg_ref) = refs[n_gate_blocks:]
    tm = x_ref.shape[0]
    d_model = x_ref.shape[1]
    first = pl.program_id(0) % tiles_per_seq == 0

    z = cc_ref[...].astype(F32) * cx_ref[...].astype(F32)
    zp = cch_ref[...].astype(F32) * cxh_ref[...].astype(F32)
    zp = jnp.where(first, 0.0, zp)
    w = convw_ref[...]
    row8 = lax.broadcasted_iota(jnp.int32, (SUBLANES, 1), 0)
    y = z * w[CONV_K - 1:CONV_K, :]
    for back in range(1, CONV_K):
        zs = pltpu.roll(z, back, axis=0)
        ps = pltpu.roll(zp, back, axis=0)
        head = jnp.where(row8 < back, ps, zs[:SUBLANES])
        zs = jnp.concatenate([head, zs[SUBLANES:]], axis=0)
        y = y + zs * w[CONV_K - 1 - back:CONV_K - back, :]
    ua_ref[...] = (cb_ref[...].astype(F32) * y).astype(ua_ref.dtype)

    gq = gqx_ref[...]
    for h in range(X_HEADS):
        sl = slice(h * X_HEAD_DIM, (h + 1) * X_HEAD_DIM)
        qn = _rms_rows(xq_ref[:, sl].astype(F32), gq).astype(BF16)
        s = lax.dot_general(qn, mk_ref[:, sl], (((1,), (1,)), ((), ())), preferred_element_type=F32)
        s = s - jnp.max(s, axis=-1, keepdims=True)
        p = jnp.exp(s)
        p = p / jnp.sum(p, axis=-1, keepdims=True)
        oc_ref[:, sl] = jnp.dot(p.astype(BF16), mv_ref[:, sl], preferred_element_type=F32).astype(oc_ref.dtype)

    ua = ua_ref[...]
    ob = ob_ref[...]
    oc = oc_ref[...]
    for c in range(d_model // MERGE_CHUNK):
        sl = slice(c * MERGE_CHUNK, (c + 1) * MERGE_CHUNK)
        merged = jnp.zeros((tm, MERGE_CHUNK), F32)
        for br, (act, w_ref) in enumerate(((ua, wa_ref), (ob, wb_ref), (oc, wc_ref))):
            blk, off = divmod(br * d_model + c * MERGE_CHUNK, GATE_BLOCK)
            gate = jax.nn.sigmoid(gate_refs[blk][:, off:off + MERGE_CHUNK].astype(F32))
            merged = merged + gate * jnp.dot(act, w_ref[:, sl], preferred_element_type=F32)
        mg_ref[:, sl] = merged.astype(mg_ref.dtype)

    o_ref[...] = x_ref[...] + jnp.dot(mg_ref[...], wo_ref[...], preferred_element_type=F32)


def merge(proj, ob, x, conv_w, g_q_x, mk, mv, w_conv_out, w_diff_out, w_x_out, w_o,
          *, batch, cols, tm):
    T, D = x.shape
    S = T // batch
    M = mk.shape[0] // batch
    assert S % tm == 0 and tm % SUBLANES == 0
    tiles_per_seq = S // tm
    halo_blocks = tm // SUBLANES

    def col_spec(name, width):
        assert cols[name] % width == 0
        return pl.BlockSpec((tm, width), lambda i: (i, cols[name] // width))

    def halo_spec(name):
        return pl.BlockSpec((SUBLANES, CONV_WIDTH),
                            lambda i: (jnp.maximum(i * halo_blocks - 1, 0), cols[name] // CONV_WIDTH))

    def resident(shape):
        return pl.BlockSpec(shape, lambda i: (0, 0), pipeline_mode=pl.Buffered(1))

    assert cols["gates"] % GATE_BLOCK == 0 and (N_BRANCH * D) % GATE_BLOCK == 0 and D % MERGE_CHUNK == 0
    n_gate_blocks = N_BRANCH * D // GATE_BLOCK
    gate_specs = [pl.BlockSpec((tm, GATE_BLOCK), lambda i, g=g: (i, cols["gates"] // GATE_BLOCK + g))
                  for g in range(n_gate_blocks)]

    return pl.pallas_call(
        functools.partial(_merge_kernel, tiles_per_seq=tiles_per_seq, n_gate_blocks=n_gate_blocks),
        grid=(T // tm,),
        in_specs=gate_specs + [
            col_spec("cb", CONV_WIDTH),
            col_spec("cc", CONV_WIDTH),
            col_spec("cx", CONV_WIDTH),
            halo_spec("cc"),
            halo_spec("cx"),
            col_spec("xq", X_WIDTH),
            pl.BlockSpec((tm, DIFF_VW), lambda i: (i, 0)),
            pl.BlockSpec((tm, D), lambda i: (i, 0)),
            pl.BlockSpec((CONV_K, CONV_WIDTH), lambda i: (0, 0)),
            pl.BlockSpec((1, X_HEAD_DIM), lambda i: (0, 0)),
            pl.BlockSpec((M, X_WIDTH), lambda i: (i // tiles_per_seq, 0)),
            pl.BlockSpec((M, X_WIDTH), lambda i: (i // tiles_per_seq, 0)),
            resident((CONV_WIDTH, D)),
            resident((DIFF_VW, D)),
            resident((X_WIDTH, D)),
            resident((D, D)),
        ],
        out_specs=pl.BlockSpec((tm, D), lambda i: (i, 0)),
        out_shape=jax.ShapeDtypeStruct((T, D), F32),
        scratch_shapes=[
            pltpu.VMEM((tm, CONV_WIDTH), BF16),
            pltpu.VMEM((tm, X_WIDTH), BF16),
            pltpu.VMEM((tm, D), BF16),
        ],
        compiler_params=_compiler_params(("parallel",)),
        name="merge",
    )(*([proj] * (n_gate_blocks + 6)), ob, x, conv_w, g_q_x.reshape(1, X_HEAD_DIM),
      mk, mv, w_conv_out, w_diff_out, w_x_out, w_o)


def _ffn_kernel(x_ref, g_ref, wg_ref, wu_ref, wd_ref, o_ref, h_ref):
    @pl.when(pl.program_id(1) == 0)
    def _():
        x = x_ref[...]
        h_ref[...] = _rms_rows(x, g_ref[...]).astype(h_ref.dtype)
        o_ref[...] = x

    h = h_ref[...]
    a = jnp.dot(h, wg_ref[...], preferred_element_type=F32)
    b = jnp.dot(h, wu_ref[...], preferred_element_type=F32)
    act = (a * jax.nn.sigmoid(a) * b).astype(BF16)
    o_ref[...] += jnp.dot(act, wd_ref[...], preferred_element_type=F32)


def ffn(x, g, w_gate_up, w_down, *, tm, tf):
    T, D = x.shape
    d_ff = w_down.shape[0]
    assert T % tm == 0 and d_ff % tf == 0
    nf = d_ff // tf
    return pl.pallas_call(
        _ffn_kernel,
        grid=(T // tm, nf),
        in_specs=[
            pl.BlockSpec((tm, D), lambda i, f: (i, 0), pipeline_mode=pl.Buffered(1)),
            pl.BlockSpec((1, D), lambda i, f: (0, 0)),
            pl.BlockSpec((D, tf), lambda i, f: (0, f)),
            pl.BlockSpec((D, tf), lambda i, f: (0, nf + f)),
            pl.BlockSpec((tf, D), lambda i, f: (f, 0)),
        ],
        out_specs=pl.BlockSpec((tm, D), lambda i, f: (i, 0)),
        out_shape=jax.ShapeDtypeStruct((T, D), F32),
        scratch_shapes=[pltpu.VMEM((tm, D), BF16)],
        compiler_params=_compiler_params(("parallel", "arbitrary")),
        name="ffn",
    )(x, g.reshape(1, D), w_gate_up, w_gate_up, w_down)


def _tile(n, want):
    t = min(n, want)
    while n % t:
        t -= SUBLANES
    return t


def kernel(x, mem, positions, g_mix, w_in, conv_w, w_conv_out, g_q_diff, g_k_diff, lambda_vecs,
           g_subln, w_diff_out, g_mem, w_mem_kv, g_q_x, g_k_x, w_x_out, w_o, g_ffn, w_gate_up, w_down):
    B, S, D = x.shape
    M = mem.shape[1]
    T = B * S
    depth = w_in.shape[0]
    d_ff = w_down.shape[1]

    n_conv = 3 * CONV_WIDTH
    n_diff = 2 * DIFF_QK + DIFF_VW
    cols = {
        "cb": 0,
        "cc": CONV_WIDTH,
        "cx": 2 * CONV_WIDTH,
        "dq": n_conv,
        "dk": n_conv + DIFF_QK,
        "dv": n_conv + 2 * DIFF_QK,
        "xq": n_conv + n_diff,
        "gates": n_conv + n_diff + X_WIDTH,
    }

    half = ROT_HALF
    inv_freq = ROPE_THETA ** (-jnp.arange(half, dtype=F32) / half)
    lane_d = jnp.arange(LANES) % DIFF_D
    inv_freq_lanes = jnp.where(lane_d < ROT, inv_freq[lane_d % half], 0.0).astype(F32).reshape(1, LANES)

    xf = x.reshape(T, D)
    memf = mem.reshape(B * M, D)
    pos = positions.reshape(T, 1)

    tm_proj = _tile(T, 1024)
    tq = _tile(S, 1024)
    tk = tq // 2
    tm_merge = _tile(S, 256)
    tm_ffn = _tile(T, 1024)
    tf = _tile(d_ff, 512)

    w_in, w_mem_kv, w_conv_out, w_diff_out, w_x_out, w_o, w_gate_up, w_down = (
        w.astype(BF16) for w in (w_in, w_mem_kv, w_conv_out, w_diff_out, w_x_out, w_o, w_gate_up, w_down))

    for l in range(depth):
        lam_init = 0.8 - 0.6 * math.exp(-0.3 * l)

        proj = norm_matmul(xf, g_mix[l], w_in[l], tm=tm_proj, tn=1024, out_dtype=BF16)
        qt, k, vt = qk_prep(proj, pos, inv_freq_lanes, g_q_diff[l], g_k_diff[l],
                            q_col=cols["dq"], k_col=cols["dk"], v_col=cols["dv"], tm=tq, tk=tk)
        ob = diff_attn(qt, k, vt, lambda_vecs[l], g_subln[l], batch=B, tq=tq, tk=tk, lam_init=lam_init)

        kv = norm_matmul(memf, g_mem[l], w_mem_kv[l], tm=B * M, tn=1024, out_dtype=F32)
        mk, mv = mem_prep(kv, g_k_x[l])

        xf = merge(proj, ob, xf, conv_w[l], g_q_x[l], mk, mv,
                   w_conv_out[l], w_diff_out[l], w_x_out[l], w_o[l], batch=B, cols=cols, tm=tm_merge)
        xf = ffn(xf, g_ffn[l], w_gate_up[l], w_down[l], tm=tm_ffn, tf=tf)

    return xf.reshape(B, S, D)
```

```python
import functools
import math

import jax
import jax.numpy as jnp
from jax import lax
from jax.experimental import pallas as pl
from jax.experimental.pallas import tpu as pltpu

EPS = 1e-6
CONV_WIDTH = 1024
CONV_K = 3
DIFF_HEADS = 8
DIFF_D = 64
DIFF_V = 2 * DIFF_D
DIFF_QK = DIFF_HEADS * 2 * DIFF_D
DIFF_VW = DIFF_HEADS * DIFF_V
X_HEADS = 4
X_HEAD_DIM = 256
X_WIDTH = X_HEADS * X_HEAD_DIM
N_BRANCH = 3
ROPE_THETA = 500000.0
ROT_FRAC = 4
ROT = DIFF_D // ROT_FRAC
ROT_HALF = ROT // 2

LANES = 128
SUBLANES = 8
BF16_ROWS = 16
V_ROWS = DIFF_V + BF16_ROWS
Q_SCALE = DIFF_D ** -0.5 * math.log2(math.e)
VMEM_LIMIT_BYTES = 56 * 1024 * 1024

F32 = jnp.float32
BF16 = jnp.bfloat16


def _compiler_params(semantics):
    return pltpu.CompilerParams(dimension_semantics=semantics, vmem_limit_bytes=VMEM_LIMIT_BYTES)


def _rms_rows(t, g):
    ms = jnp.mean(t * t, axis=-1, keepdims=True)
    return t * lax.rsqrt(ms + EPS) * g


def _norm_matmul_kernel(x_ref, g_ref, w_ref, o_ref, h_ref):
    @pl.when(pl.program_id(1) == 0)
    def _():
        h_ref[...] = _rms_rows(x_ref[...], g_ref[...]).astype(h_ref.dtype)

    o_ref[...] = jnp.dot(h_ref[...], w_ref[...], preferred_element_type=F32).astype(o_ref.dtype)


def norm_matmul(x, g, w, *, tm, tn, out_dtype):
    T, D = x.shape
    N = w.shape[1]
    assert T % tm == 0 and N % tn == 0
    return pl.pallas_call(
        _norm_matmul_kernel,
        grid=(T // tm, N // tn),
        in_specs=[
            pl.BlockSpec((tm, D), lambda i, j: (i, 0)),
            pl.BlockSpec((1, D), lambda i, j: (0, 0)),
            pl.BlockSpec((D, tn), lambda i, j: (0, j)),
        ],
        out_specs=pl.BlockSpec((tm, tn), lambda i, j: (i, j)),
        out_shape=jax.ShapeDtypeStruct((T, N), out_dtype),
        scratch_shapes=[pltpu.VMEM((tm, D), BF16)],
        compiler_params=_compiler_params(("parallel", "arbitrary")),
        name="norm_matmul",
    )(x, g.reshape(1, D), w)


def _group_mean_sq(t, gsum):
    sq = t * t
    hi = sq.astype(BF16)
    lo = (sq - hi.astype(F32)).astype(BF16)
    s = jnp.dot(hi, gsum, preferred_element_type=F32) + jnp.dot(lo, gsum, preferred_element_type=F32)
    return s * (1.0 / DIFF_D)


def _qk_prep_kernel(pos_ref, invf_ref, gq_ref, gk_ref, q_ref, k_ref, v_ref, qt_ref, ko_ref, vt_ref,
                    *, tk):
    tm = q_ref.shape[0]
    lane = lax.broadcasted_iota(jnp.int32, (1, LANES), 1)
    d = lane & (DIFF_D - 1)
    ang = pos_ref[...].astype(F32) * invf_ref[...]
    cos = jnp.cos(ang)
    sin = jnp.sin(ang)
    c_keep = jnp.where(d < ROT, cos, 1.0)
    s_from_hi = jnp.where(d < ROT_HALF, -sin, 0.0)
    s_from_lo = jnp.where((d >= ROT_HALF) & (d < ROT), sin, 0.0)

    row = lax.broadcasted_iota(jnp.int32, (LANES, LANES), 0)
    col = lax.broadcasted_iota(jnp.int32, (LANES, LANES), 1)
    gsum = jnp.where((row < DIFF_D) == (col < DIFF_D), 1.0, 0.0).astype(BF16)

    def norm_rot(src_ref, g, c):
        t = src_ref[:, c * LANES:(c + 1) * LANES].astype(F32)
        tn = t * lax.rsqrt(_group_mean_sq(t, gsum) + EPS) * g
        up = pltpu.roll(tn, LANES - ROT_HALF, axis=1)
        dn = pltpu.roll(tn, ROT_HALF, axis=1)
        return tn * c_keep + up * s_from_hi + dn * s_from_lo

    gq = gq_ref[...]
    gk = gk_ref[...]
    pad_row = lax.broadcasted_iota(jnp.int32, (V_ROWS - DIFF_V, tk), 0)
    ones_row = jnp.where(pad_row == 0, 1.0, 0.0).astype(vt_ref.dtype)
    for h in range(DIFF_HEADS):
        sl = slice(h * LANES, (h + 1) * LANES)
        q = norm_rot(q_ref, gq, h) * Q_SCALE
        qt_ref[h, 0] = q.T.astype(qt_ref.dtype)
        ko_ref[:, sl] = norm_rot(k_ref, gk, h).astype(ko_ref.dtype)
        vt = v_ref[:, sl].astype(F32).T
        for u in range(tm // tk):
            vt_ref[h, u, :DIFF_V] = vt[:, u * tk:(u + 1) * tk].astype(vt_ref.dtype)
            vt_ref[h, u, DIFF_V:] = ones_row


def qk_prep(proj, pos, inv_freq_lanes, g_q, g_k, *, q_col, k_col, v_col, tm, tk):
    T = proj.shape[0]
    assert T % tm == 0 and tm % tk == 0
    assert q_col % DIFF_QK == 0 and k_col % DIFF_QK == 0 and v_col % DIFF_VW == 0 and DIFF_V == LANES
    gq = jnp.tile(g_q, LANES // DIFF_D).reshape(1, LANES)
    gk = jnp.tile(g_k, LANES // DIFF_D).reshape(1, LANES)
    return pl.pallas_call(
        functools.partial(_qk_prep_kernel, tk=tk),
        grid=(T // tm,),
        in_specs=[
            pl.BlockSpec((tm, 1), lambda i: (i, 0)),
            pl.BlockSpec((1, LANES), lambda i: (0, 0)),
            pl.BlockSpec((1, LANES), lambda i: (0, 0)),
            pl.BlockSpec((1, LANES), lambda i: (0, 0)),
            pl.BlockSpec((tm, DIFF_QK), lambda i: (i, q_col // DIFF_QK)),
            pl.BlockSpec((tm, DIFF_QK), lambda i: (i, k_col // DIFF_QK)),
            pl.BlockSpec((tm, DIFF_VW), lambda i: (i, v_col // DIFF_VW)),
        ],
        out_specs=[
            pl.BlockSpec((DIFF_HEADS, 1, LANES, tm), lambda i: (0, i, 0, 0)),
            pl.BlockSpec((tm, DIFF_QK), lambda i: (i, 0)),
            pl.BlockSpec((DIFF_HEADS, tm // tk, V_ROWS, tk), lambda i: (0, i, 0, 0)),
        ],
        out_shape=[
            jax.ShapeDtypeStruct((DIFF_HEADS, T // tm, LANES, tm), BF16),
            jax.ShapeDtypeStruct((T, DIFF_QK), BF16),
            jax.ShapeDtypeStruct((DIFF_HEADS, T // tk, V_ROWS, tk), BF16),
        ],
        compiler_params=_compiler_params(("parallel",)),
        name="qk_prep",
    )(pos, inv_freq_lanes, gq, gk, proj, proj, proj)


def _diff_attn_kernel(lam_ref, g_ref, qt_ref, k_ref, vt_ref, o_ref, m_ref, acc_ref,
                      s0_ref, s1_ref, p0_ref, p1_ref, a0_ref, a1_ref, bm0_ref, bm1_ref,
                      *, tq, tk, lam_init):
    assert tq == 2 * tk
    s_bufs, p_bufs, a_bufs, bm_bufs = (s0_ref, s1_ref), (p0_ref, p1_ref), (a0_ref, a1_ref), (bm0_ref, bm1_ref)
    n_tiles = qt_ref.shape[1]
    zeros = jnp.zeros((DIFF_D, tq), qt_ref.dtype)

    lv = lam_ref[...]
    e1 = jnp.exp(jnp.sum(lv[0:1, :] * lv[1:2, :], axis=-1, keepdims=True))
    e2 = jnp.exp(jnp.sum(lv[2:3, :] * lv[3:4, :], axis=-1, keepdims=True))
    lam = e1 - e2 + lam_init

    acc_ref[...] = jnp.zeros(acc_ref.shape, F32)

    def scores(i, u, slot, masked):
        k = k_ref[pl.ds(pl.multiple_of(u * tk, tk), tk), :]
        qt = qt_ref[0, i]
        q_comp = (jnp.concatenate([qt[:DIFF_D], zeros], axis=0),
                  jnp.concatenate([zeros, qt[DIFF_D:]], axis=0))
        if masked:
            kpos = u * tk + lax.broadcasted_iota(jnp.int32, (tk, tq), 0)
            qpos = i * tq + lax.broadcasted_iota(jnp.int32, (tk, tq), 1)
            causal = kpos <= qpos
        for c in range(2):
            s = jnp.dot(k, q_comp[c], preferred_element_type=F32)
            if masked:
                s = jnp.where(causal, s, -jnp.inf)
            s_bufs[slot][c] = s
            bm_bufs[slot][c] = jnp.max(s, axis=0, keepdims=True)

    def softmax(slot, first=False):
        for c in range(2):
            if first:
                m_new = bm_bufs[slot][c]
                a_bufs[slot][c] = jnp.zeros_like(m_new)
            else:
                m_prev = m_ref[c]
                m_new = jnp.maximum(m_prev, bm_bufs[slot][c])
                a_bufs[slot][c] = jnp.exp2(m_prev - m_new)
            m_ref[c] = m_new
            p_bufs[slot][c] = jnp.exp2(s_bufs[slot][c] - m_new).astype(p_bufs[slot].dtype)

    def accumulate(u, slot):
        vt = vt_ref[0, u]
        for c in range(2):
            pv = jnp.dot(vt, p_bufs[slot][c], preferred_element_type=F32)
            acc_ref[c] = a_bufs[slot][c] * acc_ref[c] + pv

    def finalize(i):
        o = (acc_ref[0, :DIFF_V] / acc_ref[0, DIFF_V:DIFF_V + 1]
             - lam * (acc_ref[1, :DIFF_V] / acc_ref[1, DIFF_V:DIFF_V + 1]))
        o = o * lax.rsqrt(jnp.mean(o * o, axis=0, keepdims=True) + EPS)
        rows = pl.ds(pl.multiple_of(i * tq, tq), tq)
        o_ref[rows, :] = (o.T * g_ref[...] * (1.0 - lam_init)).astype(o_ref.dtype)

    def step_pair(i, u, masked):
        scores(i, u, 0, masked)
        accumulate(u - 2, 0)
        softmax(1)
        scores(i, u + 1, 1, masked)
        accumulate(u - 1, 1)
        softmax(0)

    scores(0, 0, 0, masked=True)
    scores(0, 1, 1, masked=True)
    softmax(0, first=True)

    def q_tile(i, carry):
        last = 2 * (i - 1)
        scores(i, 0, 0, masked=False)
        accumulate(last, 0)
        softmax(1)
        scores(i, 1, 1, masked=False)
        accumulate(last + 1, 1)
        finalize(i - 1)
        softmax(0, first=True)

        def full_pair(r, c):
            step_pair(i, 2 * r, masked=False)
            return c

        lax.fori_loop(1, i, full_pair, 0)
        step_pair(i, 2 * i, masked=True)
        return carry

    lax.fori_loop(1, n_tiles, q_tile, 0)

    last = 2 * (n_tiles - 1)
    accumulate(last, 0)
    softmax(1)
    accumulate(last + 1, 1)
    finalize(n_tiles - 1)


def diff_attn(qt, k, vt, lambda_vecs, g_subln, *, batch, tq, tk, lam_init):
    T = k.shape[0]
    S = T // batch
    assert S % tq == 0 and tq == 2 * tk
    nq, nk = S // tq, S // tk
    stat = pltpu.VMEM((2, 1, tq), F32)
    return pl.pallas_call(
        functools.partial(_diff_attn_kernel, tq=tq, tk=tk, lam_init=lam_init),
        grid=(batch, DIFF_HEADS),
        in_specs=[
            pl.BlockSpec((4, DIFF_D), lambda b, h: (0, 0)),
            pl.BlockSpec((1, DIFF_V), lambda b, h: (0, 0)),
            pl.BlockSpec((1, nq, LANES, tq), lambda b, h: (h, b, 0, 0)),
            pl.BlockSpec((S, LANES), lambda b, h: (b, h)),
            pl.BlockSpec((1, nk, V_ROWS, tk), lambda b, h: (h, b, 0, 0)),
        ],
        out_specs=pl.BlockSpec((S, DIFF_V), lambda b, h: (b, h)),
        out_shape=jax.ShapeDtypeStruct((T, DIFF_VW), BF16),
        scratch_shapes=[
            stat,
            pltpu.VMEM((2, V_ROWS, tq), F32),
            pltpu.VMEM((2, tk, tq), F32),
            pltpu.VMEM((2, tk, tq), F32),
            pltpu.VMEM((2, tk, tq), BF16),
            pltpu.VMEM((2, tk, tq), BF16),
            stat, stat,
            stat, stat,
        ],
        compiler_params=_compiler_params(("parallel", "parallel")),
        name="diff_attn",
    )(lambda_vecs, g_subln.reshape(1, DIFF_V), qt, k, vt)


def _mem_prep_kernel(kv_ref, g_ref, mk_ref, mv_ref):
    g = g_ref[...]
    for h in range(X_HEADS):
        sl = slice(h * X_HEAD_DIM, (h + 1) * X_HEAD_DIM)
        kh = _rms_rows(kv_ref[:, sl], g) * (X_HEAD_DIM ** -0.5)
        mk_ref[:, sl] = kh.astype(mk_ref.dtype)
    mv_ref[...] = kv_ref[:, X_WIDTH:].astype(mv_ref.dtype)


def mem_prep(kv, g_k_x):
    R = kv.shape[0]
    out = jax.ShapeDtypeStruct((R, X_WIDTH), BF16)
    return pl.pallas_call(
        _mem_prep_kernel,
        grid=(1,),
        in_specs=[
            pl.BlockSpec((R, 2 * X_WIDTH), lambda i: (0, 0)),
            pl.BlockSpec((1, X_HEAD_DIM), lambda i: (0, 0)),
        ],
        out_specs=[pl.BlockSpec((R, X_WIDTH), lambda i: (0, 0))] * 2,
        out_shape=[out, out],
        compiler_params=_compiler_params(("arbitrary",)),
        name="mem_prep",
    )(kv, g_k_x.reshape(1, X_HEAD_DIM))


MERGE_CHUNK = 512
GATE_BLOCK = 1024
assert GATE_BLOCK % MERGE_CHUNK == 0


def _merge_kernel(*refs, tiles_per_seq, n_gate_blocks):
    gate_refs = refs[:n_gate_blocks]
    (cb_ref, cc_ref, cx_ref, cch_ref, cxh_ref, xq_ref, ob_ref, x_ref,
     convw_ref, gqx_ref, mk_ref, mv_ref, wa_ref, wb_ref, wc_ref, wo_ref,
     o_ref, ua_ref, oc_ref, mg_ref) = refs[n_gate_blocks:]
    tm = x_ref.shape[0]
    d_model = x_ref.shape[1]
    first = pl.program_id(0) % tiles_per_seq == 0

    z = cc_ref[...].astype(F32) * cx_ref[...].astype(F32)
    zp = cch_ref[...].astype(F32) * cxh_ref[...].astype(F32)
    zp = jnp.where(first, 0.0, zp)
    w = convw_ref[...]
    row8 = lax.broadcasted_iota(jnp.int32, (SUBLANES, 1), 0)
    y = z * w[CONV_K - 1:CONV_K, :]
    for back in range(1, CONV_K):
        zs = pltpu.roll(z, back, axis=0)
        ps = pltpu.roll(zp, back, axis=0)
        head = jnp.where(row8 < back, ps, zs[:SUBLANES])
        zs = jnp.concatenate([head, zs[SUBLANES:]], axis=0)
        y = y + zs * w[CONV_K - 1 - back:CONV_K - back, :]
    ua_ref[...] = (cb_ref[...].astype(F32) * y).astype(ua_ref.dtype)

    gq = gqx_ref[...]
    for h in range(X_HEADS):
        sl = slice(h * X_HEAD_DIM, (h + 1) * X_HEAD_DIM)
        qn = _rms_rows(xq_ref[:, sl].astype(F32), gq).astype(BF16)
        s = lax.dot_general(qn, mk_ref[:, sl], (((1,), (1,)), ((), ())), preferred_element_type=F32)
        s = s - jnp.max(s, axis=-1, keepdims=True)
        p = jnp.exp(s)
        p = p / jnp.sum(p, axis=-1, keepdims=True)
        oc_ref[:, sl] = jnp.dot(p.astype(BF16), mv_ref[:, sl], preferred_element_type=F32).astype(oc_ref.dtype)

    ua = ua_ref[...]
    ob = ob_ref[...]
    oc = oc_ref[...]
    for c in range(d_model // MERGE_CHUNK):
        sl = slice(c * MERGE_CHUNK, (c + 1) * MERGE_CHUNK)
        merged = jnp.zeros((tm, MERGE_CHUNK), F32)
        for br, (act, w_ref) in enumerate(((ua, wa_ref), (ob, wb_ref), (oc, wc_ref))):
            blk, off = divmod(br * d_model + c * MERGE_CHUNK, GATE_BLOCK)
            gate = jax.nn.sigmoid(gate_refs[blk][:, off:off + MERGE_CHUNK].astype(F32))
            merged = merged + gate * jnp.dot(act, w_ref[:, sl], preferred_element_type=F32)
        mg_ref[:, sl] = merged.astype(mg_ref.dtype)

    o_ref[...] = x_ref[...] + jnp.dot(mg_ref[...], wo_ref[...], preferred_element_type=F32)


def merge(proj, ob, x, conv_w, g_q_x, mk, mv, w_conv_out, w_diff_out, w_x_out, w_o,
          *, batch, cols, tm):
    T, D = x.shape
    S = T // batch
    M = mk.shape[0] // batch
    assert S % tm == 0 and tm % SUBLANES == 0
    tiles_per_seq = S // tm
    halo_blocks = tm // SUBLANES

    def col_spec(name, width):
        assert cols[name] % width == 0
        return pl.BlockSpec((tm, width), lambda i: (i, cols[name] // width))

    def halo_spec(name):
        return pl.BlockSpec((SUBLANES, CONV_WIDTH),
                            lambda i: (jnp.maximum(i * halo_blocks - 1, 0), cols[name] // CONV_WIDTH))

    def resident(shape):
        return pl.BlockSpec(shape, lambda i: (0, 0), pipeline_mode=pl.Buffered(1))

    assert cols["gates"] % GATE_BLOCK == 0 and (N_BRANCH * D) % GATE_BLOCK == 0 and D % MERGE_CHUNK == 0
    n_gate_blocks = N_BRANCH * D // GATE_BLOCK
    gate_specs = [pl.BlockSpec((tm, GATE_BLOCK), lambda i, g=g: (i, cols["gates"] // GATE_BLOCK + g))
                  for g in range(n_gate_blocks)]

    return pl.pallas_call(
        functools.partial(_merge_kernel, tiles_per_seq=tiles_per_seq, n_gate_blocks=n_gate_blocks),
        grid=(T // tm,),
        in_specs=gate_specs + [
            col_spec("cb", CONV_WIDTH),
            col_spec("cc", CONV_WIDTH),
            col_spec("cx", CONV_WIDTH),
            halo_spec("cc"),
            halo_spec("cx"),
            col_spec("xq", X_WIDTH),
            pl.BlockSpec((tm, DIFF_VW), lambda i: (i, 0)),
            pl.BlockSpec((tm, D), lambda i: (i, 0)),
            pl.BlockSpec((CONV_K, CONV_WIDTH), lambda i: (0, 0)),
            pl.BlockSpec((1, X_HEAD_DIM), lambda i: (0, 0)),
            pl.BlockSpec((M, X_WIDTH), lambda i: (i // tiles_per_seq, 0)),
            pl.BlockSpec((M, X_WIDTH), lambda i: (i // tiles_per_seq, 0)),
            resident((CONV_WIDTH, D)),
            resident((DIFF_VW, D)),
            resident((X_WIDTH, D)),
            resident((D, D)),
        ],
        out_specs=pl.BlockSpec((tm, D), lambda i: (i, 0)),
        out_shape=jax.ShapeDtypeStruct((T, D), F32),
        scratch_shapes=[
            pltpu.VMEM((tm, CONV_WIDTH), BF16),
            pltpu.VMEM((tm, X_WIDTH), BF16),
            pltpu.VMEM((tm, D), BF16),
        ],
        compiler_params=_compiler_params(("parallel",)),
        name="merge",
    )(*([proj] * (n_gate_blocks + 6)), ob, x, conv_w, g_q_x.reshape(1, X_HEAD_DIM),
      mk, mv, w_conv_out, w_diff_out, w_x_out, w_o)


def _ffn_kernel(x_ref, g_ref, wg_ref, wu_ref, wd_ref, o_ref, h_ref):
    @pl.when(pl.program_id(1) == 0)
    def _():
        x = x_ref[...]
        h_ref[...] = _rms_rows(x, g_ref[...]).astype(h_ref.dtype)
        o_ref[...] = x

    h = h_ref[...]
    a = jnp.dot(h, wg_ref[...], preferred_element_type=F32)
    b = jnp.dot(h, wu_ref[...], preferred_element_type=F32)
    act = (a * jax.nn.sigmoid(a) * b).astype(BF16)
    o_ref[...] += jnp.dot(act, wd_ref[...], preferred_element_type=F32)


def ffn(x, g, w_gate_up, w_down, *, tm, tf):
    T, D = x.shape
    d_ff = w_down.shape[0]
    assert T % tm == 0 and d_ff % tf == 0
    nf = d_ff // tf
    return pl.pallas_call(
        _ffn_kernel,
        grid=(T // tm, nf),
        in_specs=[
            pl.BlockSpec((tm, D), lambda i, f: (i, 0), pipeline_mode=pl.Buffered(1)),
            pl.BlockSpec((1, D), lambda i, f: (0, 0)),
            pl.BlockSpec((D, tf), lambda i, f: (0, f)),
            pl.BlockSpec((D, tf), lambda i, f: (0, nf + f)),
            pl.BlockSpec((tf, D), lambda i, f: (f, 0)),
        ],
        out_specs=pl.BlockSpec((tm, D), lambda i, f: (i, 0)),
        out_shape=jax.ShapeDtypeStruct((T, D), F32),
        scratch_shapes=[pltpu.VMEM((tm, D), BF16)],
        compiler_params=_compiler_params(("parallel", "arbitrary")),
        name="ffn",
    )(x, g.reshape(1, D), w_gate_up, w_gate_up, w_down)


def _tile(n, want):
    t = min(n, want)
    while n % t:
        t -= SUBLANES
    return t


def kernel(x, mem, positions, g_mix, w_in, conv_w, w_conv_out, g_q_diff, g_k_diff, lambda_vecs,
           g_subln, w_diff_out, g_mem, w_mem_kv, g_q_x, g_k_x, w_x_out, w_o, g_ffn, w_gate_up, w_down):
    B, S, D = x.shape
    M = mem.shape[1]
    T = B * S
    depth = w_in.shape[0]
    d_ff = w_down.shape[1]

    n_conv = 3 * CONV_WIDTH
    n_diff = 2 * DIFF_QK + DIFF_VW
    cols = {
        "cb": 0,
        "cc": CONV_WIDTH,
        "cx": 2 * CONV_WIDTH,
        "dq": n_conv,
        "dk": n_conv + DIFF_QK,
        "dv": n_conv + 2 * DIFF_QK,
        "xq": n_conv + n_diff,
        "gates": n_conv + n_diff + X_WIDTH,
    }

    half = ROT_HALF
    inv_freq = ROPE_THETA ** (-jnp.arange(half, dtype=F32) / half)
    lane_d = jnp.arange(LANES) % DIFF_D
    inv_freq_lanes = jnp.where(lane_d < ROT, inv_freq[lane_d % half], 0.0).astype(F32).reshape(1, LANES)

    xf = x.reshape(T, D)
    memf = mem.reshape(B * M, D)
    pos = positions.reshape(T, 1)

    tm_proj = _tile(T, 1024)
    tq = _tile(S, 1024)
    tk = tq // 2
    tm_merge = _tile(S, 256)
    tm_ffn = _tile(T, 1024)
    tf = _tile(d_ff, 512)

    w_in, w_mem_kv, w_conv_out, w_diff_out, w_x_out, w_o, w_gate_up, w_down = (
        w.astype(BF16) for w in (w_in, w_mem_kv, w_conv_out, w_diff_out, w_x_out, w_o, w_gate_up, w_down))

    for l in range(depth):
        lam_init = 0.8 - 0.6 * math.exp(-0.3 * l)

        proj = norm_matmul(xf, g_mix[l], w_in[l], tm=tm_proj, tn=1024, out_dtype=BF16)
        qt, k, vt = qk_prep(proj, pos, inv_freq_lanes, g_q_diff[l], g_k_diff[l],
                            q_col=cols["dq"], k_col=cols["dk"], v_col=cols["dv"], tm=tq, tk=tk)
        ob = diff_attn(qt, k, vt, lambda_vecs[l], g_subln[l], batch=B, tq=tq, tk=tk, lam_init=lam_init)

        kv = norm_matmul(memf, g_mem[l], w_mem_kv[l], tm=B * M, tn=1024, out_dtype=F32)
        mk, mv = mem_prep(kv, g_k_x[l])

        xf = merge(proj, ob, xf, conv_w[l], g_q_x[l], mk, mv,
                   w_conv_out[l], w_diff_out[l], w_x_out[l], w_o[l], batch=B, cols=cols, tm=tm_merge)
        xf = ffn(xf, g_ffn[l], w_gate_up[l], w_down[l], tm=tm_ffn, tf=tf)

    return xf.reshape(B, S, D)
```

```python
import functools
import math

import jax
import jax.numpy as jnp
from jax import lax
from jax.experimental import pallas as pl
from jax.experimental.pallas import tpu as pltpu

EPS = 1e-6
CONV_WIDTH = 1024
CONV_K = 3
DIFF_HEADS = 8
DIFF_D = 64
DIFF_V = 2 * DIFF_D
DIFF_QK = DIFF_HEADS * 2 * DIFF_D
DIFF_VW = DIFF_HEADS * DIFF_V
X_HEADS = 4
X_HEAD_DIM = 256
X_WIDTH = X_HEADS * X_HEAD_DIM
N_BRANCH = 3
ROPE_THETA = 500000.0
ROT_FRAC = 4
ROT = DIFF_D // ROT_FRAC
ROT_HALF = ROT // 2

LANES = 128
SUBLANES = 8
BF16_ROWS = 16
V_ROWS = DIFF_V + BF16_ROWS
Q_SCALE = DIFF_D ** -0.5 * math.log2(math.e)
VMEM_LIMIT_BYTES = 56 * 1024 * 1024

F32 = jnp.float32
BF16 = jnp.bfloat16


def _compiler_params(semantics):
    return pltpu.CompilerParams(dimension_semantics=semantics, vmem_limit_bytes=VMEM_LIMIT_BYTES)


def _rms_rows(t, g):
    ms = jnp.mean(t * t, axis=-1, keepdims=True)
    return t * lax.rsqrt(ms + EPS) * g


def _norm_matmul_kernel(x_ref, g_ref, w_ref, o_ref, h_ref):
    @pl.when(pl.program_id(1) == 0)
    def _():
        h_ref[...] = _rms_rows(x_ref[...], g_ref[...]).astype(h_ref.dtype)

    o_ref[...] = jnp.dot(h_ref[...], w_ref[...], preferred_element_type=F32).astype(o_ref.dtype)


def norm_matmul(x, g, w, *, tm, tn, out_dtype):
    T, D = x.shape
    N = w.shape[1]
    assert T % tm == 0 and N % tn == 0
    return pl.pallas_call(
        _norm_matmul_kernel,
        grid=(T // tm, N // tn),
        in_specs=[
            pl.BlockSpec((tm, D), lambda i, j: (i, 0)),
            pl.BlockSpec((1, D), lambda i, j: (0, 0)),
            pl.BlockSpec((D, tn), lambda i, j: (0, j)),
        ],
        out_specs=pl.BlockSpec((tm, tn), lambda i, j: (i, j)),
        out_shape=jax.ShapeDtypeStruct((T, N), out_dtype),
        scratch_shapes=[pltpu.VMEM((tm, D), BF16)],
        compiler_params=_compiler_params(("parallel", "arbitrary")),
        name="norm_matmul",
    )(x, g.reshape(1, D), w)


def _group_mean_sq(t, gsum):
    sq = t * t
    hi = sq.astype(BF16)
    lo = (sq - hi.astype(F32)).astype(BF16)
    s = jnp.dot(hi, gsum, preferred_element_type=F32) + jnp.dot(lo, gsum, preferred_element_type=F32)
    return s * (1.0 / DIFF_D)


PREP_ROWS = LANES


def _qk_prep_kernel(pos_ref, invf_ref, expand_ref, keep_ref, gq_ref, gk_ref, q_ref, k_ref, v_ref,
                    qt_ref, ko_ref, vt_ref, *, tk):
    tm = q_ref.shape[0]
    row = lax.broadcasted_iota(jnp.int32, (LANES, LANES), 0)
    col = lax.broadcasted_iota(jnp.int32, (LANES, LANES), 1)
    gsum = jnp.where((row < DIFF_D) == (col < DIFF_D), 1.0, 0.0).astype(BF16)
    trig_row = lax.broadcasted_iota(jnp.int32, (2 * ROT_HALF, 1), 0)
    trig_pad = jnp.zeros((LANES - 2 * ROT_HALF, PREP_ROWS), F32)
    pad_row = lax.broadcasted_iota(jnp.int32, (V_ROWS - DIFF_V, PREP_ROWS), 0)
    ones_row = jnp.where(pad_row == 0, 1.0, 0.0).astype(vt_ref.dtype)
    gq = gq_ref[...]
    gk = gk_ref[...]
    expand = expand_ref[...]

    for r0 in range(0, tm, PREP_ROWS):
        ang = invf_ref[...] * pos_ref[:, r0:r0 + PREP_ROWS].astype(F32)
        trig = jnp.where(trig_row < ROT_HALF, jnp.cos(ang), jnp.sin(ang))
        trig = jnp.concatenate([trig, trig_pad], axis=0).T
        hi = trig.astype(BF16)
        lo = (trig - hi.astype(F32)).astype(BF16)
        tab = (jnp.dot(hi, expand, preferred_element_type=F32)
               + jnp.dot(lo, expand, preferred_element_type=F32))
        c_keep = tab[:, :LANES] + keep_ref[...]
        s_from_hi = tab[:, LANES:2 * LANES]
        s_from_lo = tab[:, 2 * LANES:]

        def norm_rot(src_ref, g, c):
            t = src_ref[r0:r0 + PREP_ROWS, c * LANES:(c + 1) * LANES].astype(F32)
            tn = t * lax.rsqrt(_group_mean_sq(t, gsum) + EPS) * g
            up = pltpu.roll(tn, LANES - ROT_HALF, axis=1)
            dn = pltpu.roll(tn, ROT_HALF, axis=1)
            return tn * c_keep + up * s_from_hi + dn * s_from_lo

        u, off = divmod(r0, tk)
        for h in range(DIFF_HEADS):
            sl = slice(h * LANES, (h + 1) * LANES)
            q = norm_rot(q_ref, gq, h) * Q_SCALE
            qt_ref[h, 0, :, r0:r0 + PREP_ROWS] = q.T.astype(qt_ref.dtype)
            ko_ref[r0:r0 + PREP_ROWS, sl] = norm_rot(k_ref, gk, h).astype(ko_ref.dtype)
            vt = v_ref[r0:r0 + PREP_ROWS, sl].astype(F32).T
            vt_ref[h, u, :DIFF_V, off:off + PREP_ROWS] = vt.astype(vt_ref.dtype)
            vt_ref[h, u, DIFF_V:, off:off + PREP_ROWS] = ones_row


def _rotary_tables():
    inv_freq = ROPE_THETA ** (-jnp.arange(ROT_HALF, dtype=F32) / ROT_HALF)
    invf = jnp.concatenate([inv_freq, inv_freq]).reshape(2 * ROT_HALF, 1)
    src = jnp.arange(LANES)[:, None]
    d = jnp.arange(LANES)[None, :] % DIFF_D
    cos_part = jnp.where((d < ROT) & (src == d % ROT_HALF), 1.0, 0.0)
    hi_part = jnp.where((d < ROT_HALF) & (src == ROT_HALF + d), -1.0, 0.0)
    lo_part = jnp.where((d >= ROT_HALF) & (d < ROT) & (src == d), 1.0, 0.0)
    expand = jnp.concatenate([cos_part, hi_part, lo_part], axis=1).astype(BF16)
    keep = jnp.where(d < ROT, 0.0, 1.0).astype(F32)
    return invf, expand, keep


def qk_prep(proj, pos, g_q, g_k, *, q_col, k_col, v_col, tm, tk):
    T = proj.shape[0]
    assert T % tm == 0 and tm % tk == 0 and tk % PREP_ROWS == 0
    assert q_col % DIFF_QK == 0 and k_col % DIFF_QK == 0 and v_col % DIFF_VW == 0 and DIFF_V == LANES
    gq = jnp.tile(g_q, LANES // DIFF_D).reshape(1, LANES)
    gk = jnp.tile(g_k, LANES // DIFF_D).reshape(1, LANES)
    invf, expand, keep = _rotary_tables()
    return pl.pallas_call(
        functools.partial(_qk_prep_kernel, tk=tk),
        grid=(T // tm,),
        in_specs=[
            pl.BlockSpec((None, 1, tm), lambda i: (i, 0, 0)),
            pl.BlockSpec((2 * ROT_HALF, 1), lambda i: (0, 0)),
            pl.BlockSpec((LANES, 3 * LANES), lambda i: (0, 0)),
            pl.BlockSpec((1, LANES), lambda i: (0, 0)),
            pl.BlockSpec((1, LANES), lambda i: (0, 0)),
            pl.BlockSpec((1, LANES), lambda i: (0, 0)),
            pl.BlockSpec((tm, DIFF_QK), lambda i: (i, q_col // DIFF_QK)),
            pl.BlockSpec((tm, DIFF_QK), lambda i: (i, k_col // DIFF_QK)),
            pl.BlockSpec((tm, DIFF_VW), lambda i: (i, v_col // DIFF_VW)),
        ],
        out_specs=[
            pl.BlockSpec((DIFF_HEADS, 1, LANES, tm), lambda i: (0, i, 0, 0)),
            pl.BlockSpec((tm, DIFF_QK), lambda i: (i, 0)),
            pl.BlockSpec((DIFF_HEADS, tm // tk, V_ROWS, tk), lambda i: (0, i, 0, 0)),
        ],
        out_shape=[
            jax.ShapeDtypeStruct((DIFF_HEADS, T // tm, LANES, tm), BF16),
            jax.ShapeDtypeStruct((T, DIFF_QK), BF16),
            jax.ShapeDtypeStruct((DIFF_HEADS, T // tk, V_ROWS, tk), BF16),
        ],
        compiler_params=_compiler_params(("parallel",)),
        name="qk_prep",
    )(pos, invf, expand, keep, gq, gk, proj, proj, proj)


def _diff_attn_kernel(lam_ref, g_ref, qt_ref, k_ref, vt_ref, o_ref, m_ref, acc_ref,
                      s0_ref, s1_ref, p0_ref, p1_ref, a0_ref, a1_ref, bm0_ref, bm1_ref,
                      *, tq, tk, lam_init):
    assert tq == 2 * tk
    s_bufs, p_bufs, a_bufs, bm_bufs = (s0_ref, s1_ref), (p0_ref, p1_ref), (a0_ref, a1_ref), (bm0_ref, bm1_ref)
    n_tiles = qt_ref.shape[1]
    zeros = jnp.zeros((DIFF_D, tq), qt_ref.dtype)

    lv = lam_ref[...]
    e1 = jnp.exp(jnp.sum(lv[0:1, :] * lv[1:2, :], axis=-1, keepdims=True))
    e2 = jnp.exp(jnp.sum(lv[2:3, :] * lv[3:4, :], axis=-1, keepdims=True))
    lam = e1 - e2 + lam_init

    acc_ref[...] = jnp.zeros(acc_ref.shape, F32)

    def scores(i, u, slot, masked):
        k = k_ref[pl.ds(pl.multiple_of(u * tk, tk), tk), :]
        qt = qt_ref[0, i]
        q_comp = (jnp.concatenate([qt[:DIFF_D], zeros], axis=0),
                  jnp.concatenate([zeros, qt[DIFF_D:]], axis=0))
        if masked:
            kpos = u * tk + lax.broadcasted_iota(jnp.int32, (tk, tq), 0)
            qpos = i * tq + lax.broadcasted_iota(jnp.int32, (tk, tq), 1)
            causal = kpos <= qpos
        for c in range(2):
            s = jnp.dot(k, q_comp[c], preferred_element_type=F32)
            if masked:
                s = jnp.where(causal, s, -jnp.inf)
            s_bufs[slot][c] = s
            bm_bufs[slot][c] = jnp.max(s, axis=0, keepdims=True)

    def softmax(slot, first=False):
        for c in range(2):
            if first:
                m_new = bm_bufs[slot][c]
                a_bufs[slot][c] = jnp.zeros_like(m_new)
            else:
                m_prev = m_ref[c]
                m_new = jnp.maximum(m_prev, bm_bufs[slot][c])
                a_bufs[slot][c] = jnp.exp2(m_prev - m_new)
            m_ref[c] = m_new
            p_bufs[slot][c] = jnp.exp2(s_bufs[slot][c] - m_new).astype(p_bufs[slot].dtype)

    def accumulate(u, slot):
        vt = vt_ref[0, u]
        for c in range(2):
            pv = jnp.dot(vt, p_bufs[slot][c], preferred_element_type=F32)
            acc_ref[c] = a_bufs[slot][c] * acc_ref[c] + pv

    def finalize(i):
        o = (acc_ref[0, :DIFF_V] / acc_ref[0, DIFF_V:DIFF_V + 1]
             - lam * (acc_ref[1, :DIFF_V] / acc_ref[1, DIFF_V:DIFF_V + 1]))
        o = o * lax.rsqrt(jnp.mean(o * o, axis=0, keepdims=True) + EPS)
        rows = pl.ds(pl.multiple_of(i * tq, tq), tq)
        o_ref[rows, :] = (o.T * g_ref[...] * (1.0 - lam_init)).astype(o_ref.dtype)

    def step_pair(i, u, masked):
        scores(i, u, 0, masked)
        accumulate(u - 2, 0)
        softmax(1)
        scores(i, u + 1, 1, masked)
        accumulate(u - 1, 1)
        softmax(0)

    scores(0, 0, 0, masked=True)
    scores(0, 1, 1, masked=True)
    softmax(0, first=True)

    def q_tile(i, carry):
        last = 2 * (i - 1)
        scores(i, 0, 0, masked=False)
        accumulate(last, 0)
        softmax(1)
        scores(i, 1, 1, masked=False)
        accumulate(last + 1, 1)
        finalize(i - 1)
        softmax(0, first=True)

        def full_pair(r, c):
            step_pair(i, 2 * r, masked=False)
            return c

        lax.fori_loop(1, i, full_pair, 0)
        step_pair(i, 2 * i, masked=True)
        return carry

    lax.fori_loop(1, n_tiles, q_tile, 0)

    last = 2 * (n_tiles - 1)
    accumulate(last, 0)
    softmax(1)
    accumulate(last + 1, 1)
    finalize(n_tiles - 1)


def diff_attn(qt, k, vt, lambda_vecs, g_subln, *, batch, tq, tk, lam_init):
    T = k.shape[0]
    S = T // batch
    assert S % tq == 0 and tq == 2 * tk
    nq, nk = S // tq, S // tk
    stat = pltpu.VMEM((2, 1, tq), F32)
    return pl.pallas_call(
        functools.partial(_diff_attn_kernel, tq=tq, tk=tk, lam_init=lam_init),
        grid=(batch, DIFF_HEADS),
        in_specs=[
            pl.BlockSpec((4, DIFF_D), lambda b, h: (0, 0)),
            pl.BlockSpec((1, DIFF_V), lambda b, h: (0, 0)),
            pl.BlockSpec((1, nq, LANES, tq), lambda b, h: (h, b, 0, 0)),
            pl.BlockSpec((S, LANES), lambda b, h: (b, h)),
            pl.BlockSpec((1, nk, V_ROWS, tk), lambda b, h: (h, b, 0, 0)),
        ],
        out_specs=pl.BlockSpec((S, DIFF_V), lambda b, h: (b, h)),
        out_shape=jax.ShapeDtypeStruct((T, DIFF_VW), BF16),
        scratch_shapes=[
            stat,
            pltpu.VMEM((2, V_ROWS, tq), F32),
            pltpu.VMEM((2, tk, tq), F32),
            pltpu.VMEM((2, tk, tq), F32),
            pltpu.VMEM((2, tk, tq), BF16),
            pltpu.VMEM((2, tk, tq), BF16),
            stat, stat,
            stat, stat,
        ],
        compiler_params=_compiler_params(("parallel", "parallel")),
        name="diff_attn",
    )(lambda_vecs, g_subln.reshape(1, DIFF_V), qt, k, vt)


def _mem_prep_kernel(kv_ref, g_ref, mk_ref, mv_ref):
    g = g_ref[...]
    for h in range(X_HEADS):
        sl = slice(h * X_HEAD_DIM, (h + 1) * X_HEAD_DIM)
        kh = _rms_rows(kv_ref[:, sl], g) * (X_HEAD_DIM ** -0.5)
        mk_ref[:, sl] = kh.astype(mk_ref.dtype)
    mv_ref[...] = kv_ref[:, X_WIDTH:].astype(mv_ref.dtype)


def mem_prep(kv, g_k_x):
    R = kv.shape[0]
    out = jax.ShapeDtypeStruct((R, X_WIDTH), BF16)
    return pl.pallas_call(
        _mem_prep_kernel,
        grid=(1,),
        in_specs=[
            pl.BlockSpec((R, 2 * X_WIDTH), lambda i: (0, 0)),
            pl.BlockSpec((1, X_HEAD_DIM), lambda i: (0, 0)),
        ],
        out_specs=[pl.BlockSpec((R, X_WIDTH), lambda i: (0, 0))] * 2,
        out_shape=[out, out],
        compiler_params=_compiler_params(("arbitrary",)),
        name="mem_prep",
    )(kv, g_k_x.reshape(1, X_HEAD_DIM))


MERGE_CHUNK = 512
GATE_BLOCK = 1024
assert GATE_BLOCK % MERGE_CHUNK == 0


def _merge_kernel(*refs, tiles_per_seq, n_gate_blocks):
    gate_refs = refs[:n_gate_blocks]
    (cb_ref, cc_ref, cx_ref, cch_ref, cxh_ref, xq_ref, ob_ref, x_ref,
     convw_ref, gqx_ref, mk_ref, mv_ref, wa_ref, wb_ref, wc_ref, wo_ref,
     o_ref, ua_ref, oc_ref, mg_ref, yb_ref) = refs[n_gate_blocks:]
    tm = x_ref.shape[0]
    d_model = x_ref.shape[1]
    first = pl.program_id(0) % tiles_per_seq == 0

    def gate(branch, c):
        blk, off = divmod(branch * d_model + c * MERGE_CHUNK, GATE_BLOCK)
        return jax.nn.sigmoid(gate_refs[blk][:, off:off + MERGE_CHUNK].astype(F32))

    ob = ob_ref[...]
    for c in range(d_model // MERGE_CHUNK):
        sl = slice(c * MERGE_CHUNK, (c + 1) * MERGE_CHUNK)
        yb_ref[:, sl] = gate(1, c) * jnp.dot(ob, wb_ref[:, sl], preferred_element_type=F32)

    z = cc_ref[...].astype(F32) * cx_ref[...].astype(F32)
    zp = cch_ref[...].astype(F32) * cxh_ref[...].astype(F32)
    zp = jnp.where(first, 0.0, zp)
    w = convw_ref[...]
    row8 = lax.broadcasted_iota(jnp.int32, (SUBLANES, 1), 0)
    y = z * w[CONV_K - 1:CONV_K, :]
    for back in range(1, CONV_K):
        zs = pltpu.roll(z, back, axis=0)
        ps = pltpu.roll(zp, back, axis=0)
        head = jnp.where(row8 < back, ps, zs[:SUBLANES])
        zs = jnp.concatenate([head, zs[SUBLANES:]], axis=0)
        y = y + zs * w[CONV_K - 1 - back:CONV_K - back, :]
    ua_ref[...] = (cb_ref[...].astype(F32) * y).astype(ua_ref.dtype)

    gq = gqx_ref[...]
    for h in range(X_HEADS):
        sl = slice(h * X_HEAD_DIM, (h + 1) * X_HEAD_DIM)
        qn = _rms_rows(xq_ref[:, sl].astype(F32), gq).astype(BF16)
        s = lax.dot_general(qn, mk_ref[:, sl], (((1,), (1,)), ((), ())), preferred_element_type=F32)
        s = s - jnp.max(s, axis=-1, keepdims=True)
        p = jnp.exp(s)
        p = p / jnp.sum(p, axis=-1, keepdims=True)
        oc_ref[:, sl] = jnp.dot(p.astype(BF16), mv_ref[:, sl], preferred_element_type=F32).astype(oc_ref.dtype)

    ua = ua_ref[...]
    oc = oc_ref[...]
    for c in range(d_model // MERGE_CHUNK):
        sl = slice(c * MERGE_CHUNK, (c + 1) * MERGE_CHUNK)
        merged = yb_ref[:, sl]
        for br, act, w_ref in ((0, ua, wa_ref), (2, oc, wc_ref)):
            merged = merged + gate(br, c) * jnp.dot(act, w_ref[:, sl], preferred_element_type=F32)
        mg_ref[:, sl] = merged.astype(mg_ref.dtype)

    o_ref[...] = x_ref[...] + jnp.dot(mg_ref[...], wo_ref[...], preferred_element_type=F32)


def merge(proj, ob, x, conv_w, g_q_x, mk, mv, w_conv_out, w_diff_out, w_x_out, w_o,
          *, batch, cols, tm):
    T, D = x.shape
    S = T // batch
    M = mk.shape[0] // batch
    assert S % tm == 0 and tm % SUBLANES == 0
    tiles_per_seq = S // tm
    halo_blocks = tm // SUBLANES

    def col_spec(name, width):
        assert cols[name] % width == 0
        return pl.BlockSpec((tm, width), lambda i: (i, cols[name] // width))

    def halo_spec(name):
        return pl.BlockSpec((SUBLANES, CONV_WIDTH),
                            lambda i: (jnp.maximum(i * halo_blocks - 1, 0), cols[name] // CONV_WIDTH))

    def resident(shape):
        return pl.BlockSpec(shape, lambda i: (0, 0), pipeline_mode=pl.Buffered(1))

    assert cols["gates"] % GATE_BLOCK == 0 and (N_BRANCH * D) % GATE_BLOCK == 0 and D % MERGE_CHUNK == 0
    n_gate_blocks = N_BRANCH * D // GATE_BLOCK
    gate_specs = [pl.BlockSpec((tm, GATE_BLOCK), lambda i, g=g: (i, cols["gates"] // GATE_BLOCK + g))
                  for g in range(n_gate_blocks)]

    return pl.pallas_call(
        functools.partial(_merge_kernel, tiles_per_seq=tiles_per_seq, n_gate_blocks=n_gate_blocks),
        grid=(T // tm,),
        in_specs=gate_specs + [
            col_spec("cb", CONV_WIDTH),
            col_spec("cc", CONV_WIDTH),
            col_spec("cx", CONV_WIDTH),
            halo_spec("cc"),
            halo_spec("cx"),
            col_spec("xq", X_WIDTH),
            pl.BlockSpec((tm, DIFF_VW), lambda i: (i, 0)),
            pl.BlockSpec((tm, D), lambda i: (i, 0)),
            pl.BlockSpec((CONV_K, CONV_WIDTH), lambda i: (0, 0)),
            pl.BlockSpec((1, X_HEAD_DIM), lambda i: (0, 0)),
            pl.BlockSpec((M, X_WIDTH), lambda i: (i // tiles_per_seq, 0)),
            pl.BlockSpec((M, X_WIDTH), lambda i: (i // tiles_per_seq, 0)),
            resident((CONV_WIDTH, D)),
            resident((DIFF_VW, D)),
            resident((X_WIDTH, D)),
            resident((D, D)),
        ],
        out_specs=pl.BlockSpec((tm, D), lambda i: (i, 0)),
        out_shape=jax.ShapeDtypeStruct((T, D), F32),
        scratch_shapes=[
            pltpu.VMEM((tm, CONV_WIDTH), BF16),
            pltpu.VMEM((tm, X_WIDTH), BF16),
            pltpu.VMEM((tm, D), BF16),
            pltpu.VMEM((tm, D), F32),
        ],
        compiler_params=_compiler_params(("parallel",)),
        name="merge",
    )(*([proj] * (n_gate_blocks + 6)), ob, x, conv_w, g_q_x.reshape(1, X_HEAD_DIM),
      mk, mv, w_conv_out, w_diff_out, w_x_out, w_o)


def _ffn_kernel(x_ref, g_ref, wg_ref, wu_ref, wd_ref, o_ref, h_ref):
    @pl.when(pl.program_id(1) == 0)
    def _():
        x = x_ref[...]
        h_ref[...] = _rms_rows(x, g_ref[...]).astype(h_ref.dtype)
        o_ref[...] = x

    h = h_ref[...]
    a = jnp.dot(h, wg_ref[...], preferred_element_type=F32)
    b = jnp.dot(h, wu_ref[...], preferred_element_type=F32)
    act = (a * jax.nn.sigmoid(a) * b).astype(BF16)
    o_ref[...] += jnp.dot(act, wd_ref[...], preferred_element_type=F32)


def ffn(x, g, w_gate_up, w_down, *, tm, tf):
    T, D = x.shape
    d_ff = w_down.shape[0]
    assert T % tm == 0 and d_ff % tf == 0
    nf = d_ff // tf
    return pl.pallas_call(
        _ffn_kernel,
        grid=(T // tm, nf),
        in_specs=[
            pl.BlockSpec((tm, D), lambda i, f: (i, 0)),
            pl.BlockSpec((1, D), lambda i, f: (0, 0)),
            pl.BlockSpec((D, tf), lambda i, f: (0, f)),
            pl.BlockSpec((D, tf), lambda i, f: (0, nf + f)),
            pl.BlockSpec((tf, D), lambda i, f: (f, 0)),
        ],
        out_specs=pl.BlockSpec((tm, D), lambda i, f: (i, 0)),
        out_shape=jax.ShapeDtypeStruct((T, D), F32),
        scratch_shapes=[pltpu.VMEM((tm, D), BF16)],
        compiler_params=_compiler_params(("parallel", "arbitrary")),
        name="ffn",
    )(x, g.reshape(1, D), w_gate_up, w_gate_up, w_down)


def _tile(n, want):
    t = min(n, want)
    while n % t:
        t -= SUBLANES
    return t


def kernel(x, mem, positions, g_mix, w_in, conv_w, w_conv_out, g_q_diff, g_k_diff, lambda_vecs,
           g_subln, w_diff_out, g_mem, w_mem_kv, g_q_x, g_k_x, w_x_out, w_o, g_ffn, w_gate_up, w_down):
    B, S, D = x.shape
    M = mem.shape[1]
    T = B * S
    depth = w_in.shape[0]
    d_ff = w_down.shape[1]

    n_conv = 3 * CONV_WIDTH
    n_diff = 2 * DIFF_QK + DIFF_VW
    cols = {
        "cb": 0,
        "cc": CONV_WIDTH,
        "cx": 2 * CONV_WIDTH,
        "dq": n_conv,
        "dk": n_conv + DIFF_QK,
        "dv": n_conv + 2 * DIFF_QK,
        "xq": n_conv + n_diff,
        "gates": n_conv + n_diff + X_WIDTH,
    }

    xf = x.reshape(T, D)
    memf = mem.reshape(B * M, D)

    tm_proj = _tile(T, 1024)
    tq = _tile(S, 1024)
    tk = tq // 2
    pos = positions.reshape(T // tq, 1, tq)
    tm_merge = _tile(S, 256)
    tm_ffn = _tile(T, 512)
    tf = _tile(d_ff, 512)

    w_in, w_mem_kv, w_conv_out, w_diff_out, w_x_out, w_o, w_gate_up, w_down = (
        w.astype(BF16) for w in (w_in, w_mem_kv, w_conv_out, w_diff_out, w_x_out, w_o, w_gate_up, w_down))

    for l in range(depth):
        lam_init = 0.8 - 0.6 * math.exp(-0.3 * l)

        proj = norm_matmul(xf, g_mix[l], w_in[l], tm=tm_proj, tn=1024, out_dtype=BF16)
        qt, k, vt = qk_prep(proj, pos, g_q_diff[l], g_k_diff[l],
                            q_col=cols["dq"], k_col=cols["dk"], v_col=cols["dv"], tm=tq, tk=tk)
        ob = diff_attn(qt, k, vt, lambda_vecs[l], g_subln[l], batch=B, tq=tq, tk=tk, lam_init=lam_init)

        kv = norm_matmul(memf, g_mem[l], w_mem_kv[l], tm=B * M, tn=1024, out_dtype=F32)
        mk, mv = mem_prep(kv, g_k_x[l])

        xf = merge(proj, ob, xf, conv_w[l], g_q_x[l], mk, mv,
                   w_conv_out[l], w_diff_out[l], w_x_out[l], w_o[l], batch=B, cols=cols, tm=tm_merge)
        xf = ffn(xf, g_ffn[l], w_gate_up[l], w_down[l], tm=tm_ffn, tf=tf)

    return xf.reshape(B, S, D)
```

```python
import functools
import math

import jax
import jax.numpy as jnp
from jax import lax
from jax.experimental import pallas as pl
from jax.experimental.pallas import tpu as pltpu

EPS = 1e-6
CONV_WIDTH = 1024
CONV_K = 3
DIFF_HEADS = 8
DIFF_D = 64
DIFF_V = 2 * DIFF_D
DIFF_QK = DIFF_HEADS * 2 * DIFF_D
DIFF_VW = DIFF_HEADS * DIFF_V
X_HEADS = 4
X_HEAD_DIM = 256
X_WIDTH = X_HEADS * X_HEAD_DIM
N_BRANCH = 3
ROPE_THETA = 500000.0
ROT_FRAC = 4
ROT = DIFF_D // ROT_FRAC
ROT_HALF = ROT // 2

LANES = 128
SUBLANES = 8
BF16_ROWS = 16
V_ROWS = DIFF_V + BF16_ROWS
Q_SCALE = DIFF_D ** -0.5 * math.log2(math.e)
VMEM_LIMIT_BYTES = 56 * 1024 * 1024

F32 = jnp.float32
BF16 = jnp.bfloat16


def _compiler_params(semantics):
    return pltpu.CompilerParams(dimension_semantics=semantics, vmem_limit_bytes=VMEM_LIMIT_BYTES)


def _rms_rows(t, g):
    ms = jnp.mean(t * t, axis=-1, keepdims=True)
    return t * lax.rsqrt(ms + EPS) * g


def _norm_matmul_kernel(x_ref, g_ref, w_ref, o_ref, h_ref):
    @pl.when(pl.program_id(1) == 0)
    def _():
        h_ref[...] = _rms_rows(x_ref[...], g_ref[...]).astype(h_ref.dtype)

    o_ref[...] = jnp.dot(h_ref[...], w_ref[...], preferred_element_type=F32).astype(o_ref.dtype)


def norm_matmul(x, g, w, *, layer, tm, tn, out_dtype):
    T, D = x.shape
    N = w.shape[2]
    assert T % tm == 0 and N % tn == 0
    return pl.pallas_call(
        _norm_matmul_kernel,
        grid=(T // tm, N // tn),
        in_specs=[
            pl.BlockSpec((tm, D), lambda i, j: (i, 0)),
            pl.BlockSpec((1, D), lambda i, j: (0, 0)),
            pl.BlockSpec((None, D, tn), lambda i, j: (layer, 0, j)),
        ],
        out_specs=pl.BlockSpec((tm, tn), lambda i, j: (i, j)),
        out_shape=jax.ShapeDtypeStruct((T, N), out_dtype),
        scratch_shapes=[pltpu.VMEM((tm, D), BF16)],
        compiler_params=_compiler_params(("parallel", "arbitrary")),
        name="norm_matmul",
    )(x, g.reshape(1, D), w)


def _group_mean_sq(t, gsum):
    sq = t * t
    hi = sq.astype(BF16)
    lo = (sq - hi.astype(F32)).astype(BF16)
    s = jnp.dot(hi, gsum, preferred_element_type=F32) + jnp.dot(lo, gsum, preferred_element_type=F32)
    return s * (1.0 / DIFF_D)


PREP_ROWS = LANES


def _qk_prep_kernel(pos_ref, invf_ref, expand_ref, keep_ref, gq_ref, gk_ref, q_ref, k_ref, v_ref,
                    qt_ref, ko_ref, vt_ref, *, tk):
    tm = q_ref.shape[0]
    row = lax.broadcasted_iota(jnp.int32, (LANES, LANES), 0)
    col = lax.broadcasted_iota(jnp.int32, (LANES, LANES), 1)
    gsum = jnp.where((row < DIFF_D) == (col < DIFF_D), 1.0, 0.0).astype(BF16)
    trig_row = lax.broadcasted_iota(jnp.int32, (2 * ROT_HALF, 1), 0)
    trig_pad = jnp.zeros((LANES - 2 * ROT_HALF, PREP_ROWS), F32)
    pad_row = lax.broadcasted_iota(jnp.int32, (V_ROWS - DIFF_V, PREP_ROWS), 0)
    ones_row = jnp.where(pad_row == 0, 1.0, 0.0).astype(vt_ref.dtype)
    gq = gq_ref[...]
    gk = gk_ref[...]
    expand = expand_ref[...]

    for r0 in range(0, tm, PREP_ROWS):
        ang = invf_ref[...] * pos_ref[:, r0:r0 + PREP_ROWS].astype(F32)
        trig = jnp.where(trig_row < ROT_HALF, jnp.cos(ang), jnp.sin(ang))
        trig = jnp.concatenate([trig, trig_pad], axis=0).T
        hi = trig.astype(BF16)
        lo = (trig - hi.astype(F32)).astype(BF16)
        tab = (jnp.dot(hi, expand, preferred_element_type=F32)
               + jnp.dot(lo, expand, preferred_element_type=F32))
        c_keep = tab[:, :LANES] + keep_ref[...]
        s_from_hi = tab[:, LANES:2 * LANES]
        s_from_lo = tab[:, 2 * LANES:]

        def norm_rot(src_ref, g, c):
            t = src_ref[r0:r0 + PREP_ROWS, c * LANES:(c + 1) * LANES].astype(F32)
            tn = t * lax.rsqrt(_group_mean_sq(t, gsum) + EPS) * g
            up = pltpu.roll(tn, LANES - ROT_HALF, axis=1)
            dn = pltpu.roll(tn, ROT_HALF, axis=1)
            return tn * c_keep + up * s_from_hi + dn * s_from_lo

        u, off = divmod(r0, tk)
        for h in range(DIFF_HEADS):
            sl = slice(h * LANES, (h + 1) * LANES)
            q = norm_rot(q_ref, gq, h) * Q_SCALE
            qt_ref[h, 0, :, r0:r0 + PREP_ROWS] = q.T.astype(qt_ref.dtype)
            ko_ref[r0:r0 + PREP_ROWS, sl] = norm_rot(k_ref, gk, h).astype(ko_ref.dtype)
            vt = v_ref[r0:r0 + PREP_ROWS, sl].astype(F32).T
            vt_ref[h, u, :DIFF_V, off:off + PREP_ROWS] = vt.astype(vt_ref.dtype)
            vt_ref[h, u, DIFF_V:, off:off + PREP_ROWS] = ones_row


def _rotary_tables():
    inv_freq = ROPE_THETA ** (-jnp.arange(ROT_HALF, dtype=F32) / ROT_HALF)
    invf = jnp.concatenate([inv_freq, inv_freq]).reshape(2 * ROT_HALF, 1)
    src = jnp.arange(LANES)[:, None]
    d = jnp.arange(LANES)[None, :] % DIFF_D
    cos_part = jnp.where((d < ROT) & (src == d % ROT_HALF), 1.0, 0.0)
    hi_part = jnp.where((d < ROT_HALF) & (src == ROT_HALF + d), -1.0, 0.0)
    lo_part = jnp.where((d >= ROT_HALF) & (d < ROT) & (src == d), 1.0, 0.0)
    expand = jnp.concatenate([cos_part, hi_part, lo_part], axis=1).astype(BF16)
    keep = jnp.where(d < ROT, 0.0, 1.0).astype(F32)
    return invf, expand, keep


def qk_prep(proj, pos, g_q, g_k, *, q_col, k_col, v_col, tm, tk):
    T = proj.shape[0]
    assert T % tm == 0 and tm % tk == 0 and tk % PREP_ROWS == 0
    assert q_col % DIFF_QK == 0 and k_col % DIFF_QK == 0 and v_col % DIFF_VW == 0 and DIFF_V == LANES
    gq = jnp.tile(g_q, LANES // DIFF_D).reshape(1, LANES)
    gk = jnp.tile(g_k, LANES // DIFF_D).reshape(1, LANES)
    invf, expand, keep = _rotary_tables()
    return pl.pallas_call(
        functools.partial(_qk_prep_kernel, tk=tk),
        grid=(T // tm,),
        in_specs=[
            pl.BlockSpec((None, 1, tm), lambda i: (i, 0, 0)),
            pl.BlockSpec((2 * ROT_HALF, 1), lambda i: (0, 0)),
            pl.BlockSpec((LANES, 3 * LANES), lambda i: (0, 0)),
            pl.BlockSpec((1, LANES), lambda i: (0, 0)),
            pl.BlockSpec((1, LANES), lambda i: (0, 0)),
            pl.BlockSpec((1, LANES), lambda i: (0, 0)),
            pl.BlockSpec((tm, DIFF_QK), lambda i: (i, q_col // DIFF_QK)),
            pl.BlockSpec((tm, DIFF_QK), lambda i: (i, k_col // DIFF_QK)),
            pl.BlockSpec((tm, DIFF_VW), lambda i: (i, v_col // DIFF_VW)),
        ],
        out_specs=[
            pl.BlockSpec((DIFF_HEADS, 1, LANES, tm), lambda i: (0, i, 0, 0)),
            pl.BlockSpec((tm, DIFF_QK), lambda i: (i, 0)),
            pl.BlockSpec((DIFF_HEADS, tm // tk, V_ROWS, tk), lambda i: (0, i, 0, 0)),
        ],
        out_shape=[
            jax.ShapeDtypeStruct((DIFF_HEADS, T // tm, LANES, tm), BF16),
            jax.ShapeDtypeStruct((T, DIFF_QK), BF16),
            jax.ShapeDtypeStruct((DIFF_HEADS, T // tk, V_ROWS, tk), BF16),
        ],
        compiler_params=_compiler_params(("parallel",)),
        name="qk_prep",
    )(pos, invf, expand, keep, gq, gk, proj, proj, proj)


def _diff_attn_kernel(lam_ref, g_ref, qt_ref, k_ref, vt_ref, o_ref, m_ref, acc_ref,
                      s0_ref, s1_ref, p0_ref, p1_ref, a0_ref, a1_ref, bm0_ref, bm1_ref,
                      *, tq, tk, lam_init):
    assert tq == 2 * tk
    s_bufs, p_bufs, a_bufs, bm_bufs = (s0_ref, s1_ref), (p0_ref, p1_ref), (a0_ref, a1_ref), (bm0_ref, bm1_ref)
    n_tiles = qt_ref.shape[1]

    lv = lam_ref[...]
    e1 = jnp.exp(jnp.sum(lv[0:1, :] * lv[1:2, :], axis=-1, keepdims=True))
    e2 = jnp.exp(jnp.sum(lv[2:3, :] * lv[3:4, :], axis=-1, keepdims=True))
    lam = e1 - e2 + lam_init

    acc_ref[...] = jnp.zeros(acc_ref.shape, F32)

    def scores(i, u, slot, masked, q0=0):
        k = k_ref[pl.ds(pl.multiple_of(u * tk, tk), tk), :]
        qt = qt_ref[0, i, :, q0:]
        nq_cols = tq - q0
        zeros = jnp.zeros((DIFF_D, nq_cols), qt.dtype)
        q_comp = (jnp.concatenate([qt[:DIFF_D], zeros], axis=0),
                  jnp.concatenate([zeros, qt[DIFF_D:]], axis=0))
        if masked:
            kpos = u * tk + lax.broadcasted_iota(jnp.int32, (tk, nq_cols), 0)
            qpos = i * tq + q0 + lax.broadcasted_iota(jnp.int32, (tk, nq_cols), 1)
            causal = kpos <= qpos
        for c in range(2):
            s = jnp.dot(k, q_comp[c], preferred_element_type=F32)
            if masked:
                s = jnp.where(causal, s, -jnp.inf)
            s_bufs[slot][c, :, q0:] = s
            bm_bufs[slot][c, :, q0:] = jnp.max(s, axis=0, keepdims=True)

    def softmax(slot, first=False, q0=0):
        for c in range(2):
            if first:
                m_new = bm_bufs[slot][c]
                a_bufs[slot][c] = jnp.zeros_like(m_new)
            else:
                m_prev = m_ref[c, :, q0:]
                m_new = jnp.maximum(m_prev, bm_bufs[slot][c, :, q0:])
                a_bufs[slot][c, :, q0:] = jnp.exp2(m_prev - m_new)
            m_ref[c, :, q0:] = m_new
            p_bufs[slot][c, :, q0:] = jnp.exp2(s_bufs[slot][c, :, q0:] - m_new).astype(p_bufs[slot].dtype)

    def accumulate(u, slot, q0=0):
        vt = vt_ref[0, u]
        for c in range(2):
            pv = jnp.dot(vt, p_bufs[slot][c, :, q0:], preferred_element_type=F32)
            acc_ref[c, :, q0:] = a_bufs[slot][c, :, q0:] * acc_ref[c, :, q0:] + pv

    def finalize(i):
        o = (acc_ref[0, :DIFF_V] / acc_ref[0, DIFF_V:DIFF_V + 1]
             - lam * (acc_ref[1, :DIFF_V] / acc_ref[1, DIFF_V:DIFF_V + 1]))
        o = o * lax.rsqrt(jnp.mean(o * o, axis=0, keepdims=True) + EPS)
        rows = pl.ds(pl.multiple_of(i * tq, tq), tq)
        o_ref[rows, :] = (o.T * g_ref[...] * (1.0 - lam_init)).astype(o_ref.dtype)

    def step_pair(i, u, masked):
        scores(i, u, 0, masked)
        accumulate(u - 2, 0)
        softmax(1)
        scores(i, u + 1, 1, masked, q0=tk if masked else 0)
        accumulate(u - 1, 1)
        softmax(0)

    scores(0, 0, 0, masked=True)
    scores(0, 1, 1, masked=True, q0=tk)
    softmax(0, first=True)

    def q_tile(i, carry):
        last = 2 * (i - 1)
        scores(i, 0, 0, masked=False)
        accumulate(last, 0)
        softmax(1, q0=tk)
        scores(i, 1, 1, masked=False)
        accumulate(last + 1, 1, q0=tk)
        finalize(i - 1)
        softmax(0, first=True)

        def full_pair(r, c):
            step_pair(i, 2 * r, masked=False)
            return c

        lax.fori_loop(1, i, full_pair, 0)
        step_pair(i, 2 * i, masked=True)
        return carry

    lax.fori_loop(1, n_tiles, q_tile, 0)

    last = 2 * (n_tiles - 1)
    accumulate(last, 0)
    softmax(1, q0=tk)
    accumulate(last + 1, 1, q0=tk)
    finalize(n_tiles - 1)


def diff_attn(qt, k, vt, lambda_vecs, g_subln, *, batch, tq, tk, lam_init):
    T = k.shape[0]
    S = T // batch
    assert S % tq == 0 and tq == 2 * tk
    nq, nk = S // tq, S // tk
    stat = pltpu.VMEM((2, 1, tq), F32)
    return pl.pallas_call(
        functools.partial(_diff_attn_kernel, tq=tq, tk=tk, lam_init=lam_init),
        grid=(batch, DIFF_HEADS),
        in_specs=[
            pl.BlockSpec((4, DIFF_D), lambda b, h: (0, 0)),
            pl.BlockSpec((1, DIFF_V), lambda b, h: (0, 0)),
            pl.BlockSpec((1, nq, LANES, tq), lambda b, h: (h, b, 0, 0)),
            pl.BlockSpec((S, LANES), lambda b, h: (b, h)),
            pl.BlockSpec((1, nk, V_ROWS, tk), lambda b, h: (h, b, 0, 0)),
        ],
        out_specs=pl.BlockSpec((S, DIFF_V), lambda b, h: (b, h)),
        out_shape=jax.ShapeDtypeStruct((T, DIFF_VW), BF16),
        scratch_shapes=[
            stat,
            pltpu.VMEM((2, V_ROWS, tq), F32),
            pltpu.VMEM((2, tk, tq), F32),
            pltpu.VMEM((2, tk, tq), F32),
            pltpu.VMEM((2, tk, tq), BF16),
            pltpu.VMEM((2, tk, tq), BF16),
            stat, stat,
            stat, stat,
        ],
        compiler_params=_compiler_params(("parallel", "parallel")),
        name="diff_attn",
    )(lambda_vecs, g_subln.reshape(1, DIFF_V), qt, k, vt)


def _mem_prep_kernel(kv_ref, g_ref, mk_ref, mv_ref):
    g = g_ref[...]
    for h in range(X_HEADS):
        sl = slice(h * X_HEAD_DIM, (h + 1) * X_HEAD_DIM)
        kh = _rms_rows(kv_ref[:, sl], g) * (X_HEAD_DIM ** -0.5)
        mk_ref[:, sl] = kh.astype(mk_ref.dtype)
    mv_ref[...] = kv_ref[:, X_WIDTH:].astype(mv_ref.dtype)


def mem_prep(kv, g_k_x):
    R = kv.shape[0]
    out = jax.ShapeDtypeStruct((R, X_WIDTH), BF16)
    return pl.pallas_call(
        _mem_prep_kernel,
        grid=(1,),
        in_specs=[
            pl.BlockSpec((R, 2 * X_WIDTH), lambda i: (0, 0)),
            pl.BlockSpec((1, X_HEAD_DIM), lambda i: (0, 0)),
        ],
        out_specs=[pl.BlockSpec((R, X_WIDTH), lambda i: (0, 0))] * 2,
        out_shape=[out, out],
        compiler_params=_compiler_params(("arbitrary",)),
        name="mem_prep",
    )(kv, g_k_x.reshape(1, X_HEAD_DIM))


MERGE_CHUNK = 512
GATE_BLOCK = 1024
assert GATE_BLOCK % MERGE_CHUNK == 0


def _merge_kernel(*refs, tiles_per_seq, n_gate_blocks):
    gate_refs = refs[:n_gate_blocks]
    (cb_ref, cc_ref, cx_ref, cch_ref, cxh_ref, xq_ref, ob_ref, x_ref,
     convw_ref, gqx_ref, mk_ref, mv_ref, wa_ref, wb_ref, wc_ref, wo_ref,
     o_ref, ua_ref, oc_ref, mg_ref, yb_ref) = refs[n_gate_blocks:]
    tm = x_ref.shape[0]
    d_model = x_ref.shape[1]
    first = pl.program_id(0) % tiles_per_seq == 0

    def gate(branch, c):
        blk, off = divmod(branch * d_model + c * MERGE_CHUNK, GATE_BLOCK)
        return jax.nn.sigmoid(gate_refs[blk][:, off:off + MERGE_CHUNK].astype(F32))

    ob = ob_ref[...]
    for c in range(d_model // MERGE_CHUNK):
        sl = slice(c * MERGE_CHUNK, (c + 1) * MERGE_CHUNK)
        yb_ref[:, sl] = gate(1, c) * jnp.dot(ob, wb_ref[:, sl], preferred_element_type=F32)

    z = cc_ref[...].astype(F32) * cx_ref[...].astype(F32)
    zp = cch_ref[...].astype(F32) * cxh_ref[...].astype(F32)
    zp = jnp.where(first, 0.0, zp)
    w = convw_ref[...]
    row8 = lax.broadcasted_iota(jnp.int32, (SUBLANES, 1), 0)
    y = z * w[CONV_K - 1:CONV_K, :]
    for back in range(1, CONV_K):
        zs = pltpu.roll(z, back, axis=0)
        ps = pltpu.roll(zp, back, axis=0)
        head = jnp.where(row8 < back, ps, zs[:SUBLANES])
        zs = jnp.concatenate([head, zs[SUBLANES:]], axis=0)
        y = y + zs * w[CONV_K - 1 - back:CONV_K - back, :]
    ua_ref[...] = (cb_ref[...].astype(F32) * y).astype(ua_ref.dtype)

    gq = gqx_ref[...]
    for h in range(X_HEADS):
        sl = slice(h * X_HEAD_DIM, (h + 1) * X_HEAD_DIM)
        qn = _rms_rows(xq_ref[:, sl].astype(F32), gq).astype(BF16)
        s = lax.dot_general(qn, mk_ref[:, sl], (((1,), (1,)), ((), ())), preferred_element_type=F32)
        s = s - jnp.max(s, axis=-1, keepdims=True)
        p = jnp.exp(s)
        p = p / jnp.sum(p, axis=-1, keepdims=True)
        oc_ref[:, sl] = jnp.dot(p.astype(BF16), mv_ref[:, sl], preferred_element_type=F32).astype(oc_ref.dtype)

    ua = ua_ref[...]
    oc = oc_ref[...]
    for c in range(d_model // MERGE_CHUNK):
        sl = slice(c * MERGE_CHUNK, (c + 1) * MERGE_CHUNK)
        merged = yb_ref[:, sl]
        for br, act, w_ref in ((0, ua, wa_ref), (2, oc, wc_ref)):
            merged = merged + gate(br, c) * jnp.dot(act, w_ref[:, sl], preferred_element_type=F32)
        mg_ref[:, sl] = merged.astype(mg_ref.dtype)

    o_ref[...] = x_ref[...] + jnp.dot(mg_ref[...], wo_ref[...], preferred_element_type=F32)


def merge(proj, ob, x, conv_w, g_q_x, mk, mv, w_conv_out, w_diff_out, w_x_out, w_o,
          *, layer, batch, cols, tm):
    T, D = x.shape
    S = T // batch
    M = mk.shape[0] // batch
    assert S % tm == 0 and tm % SUBLANES == 0
    tiles_per_seq = S // tm
    halo_blocks = tm // SUBLANES

    def col_spec(name, width):
        assert cols[name] % width == 0
        return pl.BlockSpec((tm, width), lambda i: (i, cols[name] // width))

    def halo_spec(name):
        return pl.BlockSpec((SUBLANES, CONV_WIDTH),
                            lambda i: (jnp.maximum(i * halo_blocks - 1, 0), cols[name] // CONV_WIDTH))

    def resident(shape):
        return pl.BlockSpec((None,) + shape, lambda i: (layer, 0, 0), pipeline_mode=pl.Buffered(1))

    assert cols["gates"] % GATE_BLOCK == 0 and (N_BRANCH * D) % GATE_BLOCK == 0 and D % MERGE_CHUNK == 0
    n_gate_blocks = N_BRANCH * D // GATE_BLOCK
    gate_specs = [pl.BlockSpec((tm, GATE_BLOCK), lambda i, g=g: (i, cols["gates"] // GATE_BLOCK + g))
                  for g in range(n_gate_blocks)]

    return pl.pallas_call(
        functools.partial(_merge_kernel, tiles_per_seq=tiles_per_seq, n_gate_blocks=n_gate_blocks),
        grid=(T // tm,),
        in_specs=gate_specs + [
            col_spec("cb", CONV_WIDTH),
            col_spec("cc", CONV_WIDTH),
            col_spec("cx", CONV_WIDTH),
            halo_spec("cc"),
            halo_spec("cx"),
            col_spec("xq", X_WIDTH),
            pl.BlockSpec((tm, DIFF_VW), lambda i: (i, 0)),
            pl.BlockSpec((tm, D), lambda i: (i, 0)),
            pl.BlockSpec((CONV_K, CONV_WIDTH), lambda i: (0, 0)),
            pl.BlockSpec((1, X_HEAD_DIM), lambda i: (0, 0)),
            pl.BlockSpec((M, X_WIDTH), lambda i: (i // tiles_per_seq, 0)),
            pl.BlockSpec((M, X_WIDTH), lambda i: (i // tiles_per_seq, 0)),
            resident((CONV_WIDTH, D)),
            resident((DIFF_VW, D)),
            resident((X_WIDTH, D)),
            resident((D, D)),
        ],
        out_specs=pl.BlockSpec((tm, D), lambda i: (i, 0)),
        out_shape=jax.ShapeDtypeStruct((T, D), F32),
        scratch_shapes=[
            pltpu.VMEM((tm, CONV_WIDTH), BF16),
            pltpu.VMEM((tm, X_WIDTH), BF16),
            pltpu.VMEM((tm, D), BF16),
            pltpu.VMEM((tm, D), F32),
        ],
        compiler_params=_compiler_params(("parallel",)),
        name="merge",
    )(*([proj] * (n_gate_blocks + 6)), ob, x, conv_w, g_q_x.reshape(1, X_HEAD_DIM),
      mk, mv, w_conv_out, w_diff_out, w_x_out, w_o)


def _ffn_kernel(x_ref, g_ref, wg_ref, wu_ref, wd_ref, o_ref, h_ref):
    @pl.when(pl.program_id(1) == 0)
    def _():
        x = x_ref[...]
        h_ref[...] = _rms_rows(x, g_ref[...]).astype(h_ref.dtype)
        o_ref[...] = x

    h = h_ref[...]
    a = jnp.dot(h, wg_ref[...], preferred_element_type=F32)
    b = jnp.dot(h, wu_ref[...], preferred_element_type=F32)
    act = (a * jax.nn.sigmoid(a) * b).astype(BF16)
    o_ref[...] += jnp.dot(act, wd_ref[...], preferred_element_type=F32)


def ffn(x, g, w_gate_up, w_down, *, layer, tm, tf):
    T, D = x.shape
    d_ff = w_down.shape[1]
    assert T % tm == 0 and d_ff % tf == 0
    nf = d_ff // tf
    return pl.pallas_call(
        _ffn_kernel,
        grid=(T // tm, nf),
        in_specs=[
            pl.BlockSpec((tm, D), lambda i, f: (i, 0)),
            pl.BlockSpec((1, D), lambda i, f: (0, 0)),
            pl.BlockSpec((None, D, tf), lambda i, f: (layer, 0, f)),
            pl.BlockSpec((None, D, tf), lambda i, f: (layer, 0, nf + f)),
            pl.BlockSpec((None, tf, D), lambda i, f: (layer, f, 0)),
        ],
        out_specs=pl.BlockSpec((tm, D), lambda i, f: (i, 0)),
        out_shape=jax.ShapeDtypeStruct((T, D), F32),
        scratch_shapes=[pltpu.VMEM((tm, D), BF16)],
        compiler_params=_compiler_params(("parallel", "arbitrary")),
        name="ffn",
    )(x, g.reshape(1, D), w_gate_up, w_gate_up, w_down)


def _tile(n, want):
    t = min(n, want)
    while n % t:
        t -= SUBLANES
    return t


def kernel(x, mem, positions, g_mix, w_in, conv_w, w_conv_out, g_q_diff, g_k_diff, lambda_vecs,
           g_subln, w_diff_out, g_mem, w_mem_kv, g_q_x, g_k_x, w_x_out, w_o, g_ffn, w_gate_up, w_down):
    B, S, D = x.shape
    M = mem.shape[1]
    T = B * S
    depth = w_in.shape[0]
    d_ff = w_down.shape[1]

    n_conv = 3 * CONV_WIDTH
    n_diff = 2 * DIFF_QK + DIFF_VW
    cols = {
        "cb": 0,
        "cc": CONV_WIDTH,
        "cx": 2 * CONV_WIDTH,
        "dq": n_conv,
        "dk": n_conv + DIFF_QK,
        "dv": n_conv + 2 * DIFF_QK,
        "xq": n_conv + n_diff,
        "gates": n_conv + n_diff + X_WIDTH,
    }

    xf = x.reshape(T, D)
    memf = mem.reshape(B * M, D)

    tm_proj = _tile(T, 1024)
    tq = _tile(S, 1024)
    tk = tq // 2
    pos = positions.reshape(T // tq, 1, tq)
    tm_merge = _tile(S, 256)
    tm_ffn = _tile(T, 512)
    tf = _tile(d_ff, 512)

    w_in, w_mem_kv, w_conv_out, w_diff_out, w_x_out, w_o, w_gate_up, w_down = (
        w.astype(BF16) for w in (w_in, w_mem_kv, w_conv_out, w_diff_out, w_x_out, w_o, w_gate_up, w_down))

    for l in range(depth):
        lam_init = 0.8 - 0.6 * math.exp(-0.3 * l)

        proj = norm_matmul(xf, g_mix[l], w_in, layer=l, tm=tm_proj, tn=1024, out_dtype=BF16)
        qt, k, vt = qk_prep(proj, pos, g_q_diff[l], g_k_diff[l],
                            q_col=cols["dq"], k_col=cols["dk"], v_col=cols["dv"], tm=tq, tk=tk)
        ob = diff_attn(qt, k, vt, lambda_vecs[l], g_subln[l], batch=B, tq=tq, tk=tk, lam_init=lam_init)

        kv = norm_matmul(memf, g_mem[l], w_mem_kv, layer=l, tm=B * M, tn=1024, out_dtype=F32)
        mk, mv = mem_prep(kv, g_k_x[l])

        xf = merge(proj, ob, xf, conv_w[l], g_q_x[l], mk, mv, w_conv_out, w_diff_out, w_x_out, w_o,
                   layer=l, batch=B, cols=cols, tm=tm_merge)
        xf = ffn(xf, g_ffn[l], w_gate_up, w_down, layer=l, tm=tm_ffn, tf=tf)

    return xf.reshape(B, S, D)
```

```python
import functools
import math

import jax
import jax.numpy as jnp
from jax import lax
from jax.experimental import pallas as pl
from jax.experimental.pallas import tpu as pltpu

EPS = 1e-6
CONV_WIDTH = 1024
CONV_K = 3
DIFF_HEADS = 8
DIFF_D = 64
DIFF_V = 2 * DIFF_D
DIFF_QK = DIFF_HEADS * 2 * DIFF_D
DIFF_VW = DIFF_HEADS * DIFF_V
X_HEADS = 4
X_HEAD_DIM = 256
X_WIDTH = X_HEADS * X_HEAD_DIM
N_BRANCH = 3
ROPE_THETA = 500000.0
ROT_FRAC = 4
ROT = DIFF_D // ROT_FRAC
ROT_HALF = ROT // 2

LANES = 128
SUBLANES = 8
BF16_ROWS = 16
V_ROWS = DIFF_V + BF16_ROWS
Q_SCALE = DIFF_D ** -0.5 * math.log2(math.e)
VMEM_LIMIT_BYTES = 56 * 1024 * 1024

F32 = jnp.float32
BF16 = jnp.bfloat16


def _compiler_params(semantics):
    return pltpu.CompilerParams(dimension_semantics=semantics, vmem_limit_bytes=VMEM_LIMIT_BYTES)


def _rms_rows(t, g):
    ms = jnp.mean(t * t, axis=-1, keepdims=True)
    return t * lax.rsqrt(ms + EPS) * g


def _norm_matmul_kernel(x_ref, g_ref, w_ref, o_ref, h_ref):
    @pl.when(pl.program_id(1) == 0)
    def _():
        h_ref[...] = _rms_rows(x_ref[...], g_ref[...]).astype(h_ref.dtype)

    o_ref[...] = jnp.dot(h_ref[...], w_ref[...], preferred_element_type=F32).astype(o_ref.dtype)


def norm_matmul(x, g, w, *, layer, tm, tn, out_dtype):
    T, D = x.shape
    N = w.shape[2]
    assert T % tm == 0 and N % tn == 0
    return pl.pallas_call(
        _norm_matmul_kernel,
        grid=(T // tm, N // tn),
        in_specs=[
            pl.BlockSpec((tm, D), lambda i, j: (i, 0)),
            pl.BlockSpec((1, D), lambda i, j: (0, 0)),
            pl.BlockSpec((None, D, tn), lambda i, j: (layer, 0, j)),
        ],
        out_specs=pl.BlockSpec((tm, tn), lambda i, j: (i, j)),
        out_shape=jax.ShapeDtypeStruct((T, N), out_dtype),
        scratch_shapes=[pltpu.VMEM((tm, D), BF16)],
        compiler_params=_compiler_params(("parallel", "arbitrary")),
        name="norm_matmul",
    )(x, g.reshape(1, D), w)


def _group_mean_sq(t, gsum):
    sq = t * t
    hi = sq.astype(BF16)
    lo = (sq - hi.astype(F32)).astype(BF16)
    return jnp.dot(hi, gsum, preferred_element_type=F32) + jnp.dot(lo, gsum, preferred_element_type=F32)


PREP_ROWS = LANES


def _qk_prep_kernel(pos_ref, invf_ref, expand_ref, keep_ref, gq_ref, gk_ref, q_ref, k_ref, v_ref,
                    qt_ref, ko_ref, vt_ref, *, tk):
    tm = q_ref.shape[0]
    row = lax.broadcasted_iota(jnp.int32, (LANES, LANES), 0)
    col = lax.broadcasted_iota(jnp.int32, (LANES, LANES), 1)
    gsum = jnp.where((row < DIFF_D) == (col < DIFF_D), 1.0 / DIFF_D, 0.0).astype(BF16)
    trig_row = lax.broadcasted_iota(jnp.int32, (2 * ROT_HALF, 1), 0)
    trig_pad = jnp.zeros((LANES - 2 * ROT_HALF, PREP_ROWS), F32)
    pad_row = lax.broadcasted_iota(jnp.int32, (V_ROWS - DIFF_V, PREP_ROWS), 0)
    ones_row = jnp.where(pad_row == 0, 1.0, 0.0).astype(vt_ref.dtype)
    gq = gq_ref[...]
    gk = gk_ref[...]
    expand = expand_ref[...]

    for r0 in range(0, tm, PREP_ROWS):
        ang = invf_ref[...] * pos_ref[:, r0:r0 + PREP_ROWS].astype(F32)
        trig = jnp.where(trig_row < ROT_HALF, jnp.cos(ang), jnp.sin(ang))
        trig = jnp.concatenate([trig, trig_pad], axis=0).T
        hi = trig.astype(BF16)
        lo = (trig - hi.astype(F32)).astype(BF16)
        tab = (jnp.dot(hi, expand, preferred_element_type=F32)
               + jnp.dot(lo, expand, preferred_element_type=F32))
        c_keep = tab[:, :LANES] + keep_ref[...]
        s_from_hi = tab[:, LANES:2 * LANES]
        s_from_lo = tab[:, 2 * LANES:]

        def norm_rot(src_ref, g, c):
            t = src_ref[r0:r0 + PREP_ROWS, c * LANES:(c + 1) * LANES].astype(F32)
            tn = t * lax.rsqrt(_group_mean_sq(t, gsum) + EPS) * g
            up = pltpu.roll(tn, LANES - ROT_HALF, axis=1)
            dn = pltpu.roll(tn, ROT_HALF, axis=1)
            return tn * c_keep + up * s_from_hi + dn * s_from_lo

        u, off = divmod(r0, tk)
        for h in range(DIFF_HEADS):
            sl = slice(h * LANES, (h + 1) * LANES)
            q = norm_rot(q_ref, gq, h)
            qt_ref[h, 0, :, r0:r0 + PREP_ROWS] = q.T.astype(qt_ref.dtype)
            ko_ref[r0:r0 + PREP_ROWS, sl] = norm_rot(k_ref, gk, h).astype(ko_ref.dtype)
            vt = v_ref[r0:r0 + PREP_ROWS, sl].astype(F32).T
            vt_ref[h, u, :DIFF_V, off:off + PREP_ROWS] = vt.astype(vt_ref.dtype)
            vt_ref[h, u, DIFF_V:, off:off + PREP_ROWS] = ones_row


def _rotary_tables():
    inv_freq = ROPE_THETA ** (-jnp.arange(ROT_HALF, dtype=F32) / ROT_HALF)
    invf = jnp.concatenate([inv_freq, inv_freq]).reshape(2 * ROT_HALF, 1)
    src = jnp.arange(LANES)[:, None]
    d = jnp.arange(LANES)[None, :] % DIFF_D
    cos_part = jnp.where((d < ROT) & (src == d % ROT_HALF), 1.0, 0.0)
    hi_part = jnp.where((d < ROT_HALF) & (src == ROT_HALF + d), -1.0, 0.0)
    lo_part = jnp.where((d >= ROT_HALF) & (d < ROT) & (src == d), 1.0, 0.0)
    expand = jnp.concatenate([cos_part, hi_part, lo_part], axis=1).astype(BF16)
    keep = jnp.where(d < ROT, 0.0, 1.0).astype(F32)
    return invf, expand, keep


def qk_prep(proj, pos, g_q, g_k, *, q_col, k_col, v_col, tm, tk):
    T = proj.shape[0]
    assert T % tm == 0 and tm % tk == 0 and tk % PREP_ROWS == 0
    assert q_col % DIFF_QK == 0 and k_col % DIFF_QK == 0 and v_col % DIFF_VW == 0 and DIFF_V == LANES
    gq = (jnp.tile(g_q, LANES // DIFF_D) * Q_SCALE).reshape(1, LANES)
    gk = jnp.tile(g_k, LANES // DIFF_D).reshape(1, LANES)
    invf, expand, keep = _rotary_tables()
    return pl.pallas_call(
        functools.partial(_qk_prep_kernel, tk=tk),
        grid=(T // tm,),
        in_specs=[
            pl.BlockSpec((None, 1, tm), lambda i: (i, 0, 0)),
            pl.BlockSpec((2 * ROT_HALF, 1), lambda i: (0, 0)),
            pl.BlockSpec((LANES, 3 * LANES), lambda i: (0, 0)),
            pl.BlockSpec((1, LANES), lambda i: (0, 0)),
            pl.BlockSpec((1, LANES), lambda i: (0, 0)),
            pl.BlockSpec((1, LANES), lambda i: (0, 0)),
            pl.BlockSpec((tm, DIFF_QK), lambda i: (i, q_col // DIFF_QK)),
            pl.BlockSpec((tm, DIFF_QK), lambda i: (i, k_col // DIFF_QK)),
            pl.BlockSpec((tm, DIFF_VW), lambda i: (i, v_col // DIFF_VW)),
        ],
        out_specs=[
            pl.BlockSpec((DIFF_HEADS, 1, LANES, tm), lambda i: (0, i, 0, 0)),
            pl.BlockSpec((tm, DIFF_QK), lambda i: (i, 0)),
            pl.BlockSpec((DIFF_HEADS, tm // tk, V_ROWS, tk), lambda i: (0, i, 0, 0)),
        ],
        out_shape=[
            jax.ShapeDtypeStruct((DIFF_HEADS, T // tm, LANES, tm), BF16),
            jax.ShapeDtypeStruct((T, DIFF_QK), BF16),
            jax.ShapeDtypeStruct((DIFF_HEADS, T // tk, V_ROWS, tk), BF16),
        ],
        compiler_params=_compiler_params(("parallel",)),
        name="qk_prep",
    )(pos, invf, expand, keep, gq, gk, proj, proj, proj)


def _diff_attn_kernel(lam_ref, g_ref, qt_ref, k_ref, vt_ref, o_ref, m_ref, acc_ref,
                      s0_ref, s1_ref, p0_ref, p1_ref, a0_ref, a1_ref, bm0_ref, bm1_ref,
                      *, tq, tk, lam_init):
    assert tq == 2 * tk
    s_bufs, p_bufs, a_bufs, bm_bufs = (s0_ref, s1_ref), (p0_ref, p1_ref), (a0_ref, a1_ref), (bm0_ref, bm1_ref)
    n_tiles = qt_ref.shape[1]

    lv = lam_ref[...]
    e1 = jnp.exp(jnp.sum(lv[0:1, :] * lv[1:2, :], axis=-1, keepdims=True))
    e2 = jnp.exp(jnp.sum(lv[2:3, :] * lv[3:4, :], axis=-1, keepdims=True))
    lam = e1 - e2 + lam_init

    acc_ref[...] = jnp.zeros(acc_ref.shape, F32)

    def scores(i, u, slot, masked, q0=0):
        k = k_ref[pl.ds(pl.multiple_of(u * tk, tk), tk), :]
        qt = qt_ref[0, i, :, q0:]
        nq_cols = tq - q0
        zeros = jnp.zeros((DIFF_D, nq_cols), qt.dtype)
        q_comp = (jnp.concatenate([qt[:DIFF_D], zeros], axis=0),
                  jnp.concatenate([zeros, qt[DIFF_D:]], axis=0))
        if masked:
            kpos = u * tk + lax.broadcasted_iota(jnp.int32, (tk, nq_cols), 0)
            qpos = i * tq + q0 + lax.broadcasted_iota(jnp.int32, (tk, nq_cols), 1)
            causal = kpos <= qpos
        for c in range(2):
            s = jnp.dot(k, q_comp[c], preferred_element_type=F32)
            if masked:
                s = jnp.where(causal, s, -jnp.inf)
            s_bufs[slot][c, :, q0:] = s
            bm_bufs[slot][c, :, q0:] = jnp.max(s, axis=0, keepdims=True)

    def softmax(slot, first=False, q0=0):
        for c in range(2):
            if first:
                m_new = bm_bufs[slot][c]
                a_bufs[slot][c] = jnp.zeros_like(m_new)
            else:
                m_prev = m_ref[c, :, q0:]
                m_new = jnp.maximum(m_prev, bm_bufs[slot][c, :, q0:])
                a_bufs[slot][c, :, q0:] = jnp.exp2(m_prev - m_new)
            m_ref[c, :, q0:] = m_new
            p_bufs[slot][c, :, q0:] = jnp.exp2(s_bufs[slot][c, :, q0:] - m_new).astype(p_bufs[slot].dtype)

    def accumulate(u, slot, q0=0):
        vt = vt_ref[0, u]
        for c in range(2):
            pv = jnp.dot(vt, p_bufs[slot][c, :, q0:], preferred_element_type=F32)
            acc_ref[c, :, q0:] = a_bufs[slot][c, :, q0:] * acc_ref[c, :, q0:] + pv

    def finalize(i):
        o = (acc_ref[0, :DIFF_V] / acc_ref[0, DIFF_V:DIFF_V + 1]
             - lam * (acc_ref[1, :DIFF_V] / acc_ref[1, DIFF_V:DIFF_V + 1]))
        o = o * lax.rsqrt(jnp.mean(o * o, axis=0, keepdims=True) + EPS)
        rows = pl.ds(pl.multiple_of(i * tq, tq), tq)
        o_ref[rows, :] = (o.T * g_ref[...] * (1.0 - lam_init)).astype(o_ref.dtype)

    def step_pair(i, u, masked):
        scores(i, u, 0, masked)
        accumulate(u - 2, 0)
        softmax(1)
        scores(i, u + 1, 1, masked, q0=tk if masked else 0)
        accumulate(u - 1, 1)
        softmax(0)

    scores(0, 0, 0, masked=True)
    scores(0, 1, 1, masked=True, q0=tk)
    softmax(0, first=True)

    def next_tile(i):
        last = 2 * (i - 1)
        scores(i, 0, 0, masked=False)
        accumulate(last, 0)
        softmax(1, q0=tk)
        scores(i, 1, 1, masked=False)
        accumulate(last + 1, 1, q0=tk)
        finalize(i - 1)
        softmax(0, first=True)

    def full_pairs(i):
        def full_pair(r, c):
            step_pair(i, 2 * r, masked=False)
            return c

        lax.fori_loop(1, i, full_pair, 0)

    if n_tiles > 1:
        next_tile(1)

        def q_tile(i, carry):
            full_pairs(i)
            step_pair(i, 2 * i, masked=True)
            next_tile(i + 1)
            return carry

        lax.fori_loop(1, n_tiles - 1, q_tile, 0)
        full_pairs(n_tiles - 1)
        step_pair(n_tiles - 1, 2 * (n_tiles - 1), masked=True)

    last = 2 * (n_tiles - 1)
    accumulate(last, 0)
    softmax(1, q0=tk)
    accumulate(last + 1, 1, q0=tk)
    finalize(n_tiles - 1)


def diff_attn(qt, k, vt, lambda_vecs, g_subln, *, batch, tq, tk, lam_init):
    T = k.shape[0]
    S = T // batch
    assert S % tq == 0 and tq == 2 * tk
    nq, nk = S // tq, S // tk
    stat = pltpu.VMEM((2, 1, tq), F32)
    return pl.pallas_call(
        functools.partial(_diff_attn_kernel, tq=tq, tk=tk, lam_init=lam_init),
        grid=(batch, DIFF_HEADS),
        in_specs=[
            pl.BlockSpec((4, DIFF_D), lambda b, h: (0, 0)),
            pl.BlockSpec((1, DIFF_V), lambda b, h: (0, 0)),
            pl.BlockSpec((1, nq, LANES, tq), lambda b, h: (h, b, 0, 0)),
            pl.BlockSpec((S, LANES), lambda b, h: (b, h)),
            pl.BlockSpec((1, nk, V_ROWS, tk), lambda b, h: (h, b, 0, 0)),
        ],
        out_specs=pl.BlockSpec((S, DIFF_V), lambda b, h: (b, h)),
        out_shape=jax.ShapeDtypeStruct((T, DIFF_VW), BF16),
        scratch_shapes=[
            stat,
            pltpu.VMEM((2, V_ROWS, tq), F32),
            pltpu.VMEM((2, tk, tq), F32),
            pltpu.VMEM((2, tk, tq), F32),
            pltpu.VMEM((2, tk, tq), BF16),
            pltpu.VMEM((2, tk, tq), BF16),
            stat, stat,
            stat, stat,
        ],
        compiler_params=_compiler_params(("parallel", "parallel")),
        name="diff_attn",
    )(lambda_vecs, g_subln.reshape(1, DIFF_V), qt, k, vt)


def _mem_prep_kernel(kv_ref, g_ref, mk_ref, mv_ref):
    g = g_ref[...]
    for h in range(X_HEADS):
        sl = slice(h * X_HEAD_DIM, (h + 1) * X_HEAD_DIM)
        kh = _rms_rows(kv_ref[:, sl], g) * (X_HEAD_DIM ** -0.5)
        mk_ref[:, sl] = kh.astype(mk_ref.dtype)
    mv_ref[...] = kv_ref[:, X_WIDTH:].astype(mv_ref.dtype)


def mem_prep(kv, g_k_x):
    R = kv.shape[0]
    out = jax.ShapeDtypeStruct((R, X_WIDTH), BF16)
    return pl.pallas_call(
        _mem_prep_kernel,
        grid=(1,),
        in_specs=[
            pl.BlockSpec((R, 2 * X_WIDTH), lambda i: (0, 0)),
            pl.BlockSpec((1, X_HEAD_DIM), lambda i: (0, 0)),
        ],
        out_specs=[pl.BlockSpec((R, X_WIDTH), lambda i: (0, 0))] * 2,
        out_shape=[out, out],
        compiler_params=_compiler_params(("arbitrary",)),
        name="mem_prep",
    )(kv, g_k_x.reshape(1, X_HEAD_DIM))


MERGE_CHUNK = 512
GATE_BLOCK = 1024
assert GATE_BLOCK % MERGE_CHUNK == 0


def _merge_kernel(*refs, tiles_per_seq, n_gate_blocks):
    gate_refs = refs[:n_gate_blocks]
    (cb_ref, cc_ref, cx_ref, cch_ref, cxh_ref, xq_ref, ob_ref, x_ref,
     convw_ref, gqx_ref, mk_ref, mv_ref, wa_ref, wb_ref, wc_ref, wo_ref,
     o_ref, ua_ref, oc_ref, mg_ref, yb_ref) = refs[n_gate_blocks:]
    tm = x_ref.shape[0]
    d_model = x_ref.shape[1]
    first = pl.program_id(0) % tiles_per_seq == 0

    def gate(branch, c):
        blk, off = divmod(branch * d_model + c * MERGE_CHUNK, GATE_BLOCK)
        return jax.nn.sigmoid(gate_refs[blk][:, off:off + MERGE_CHUNK].astype(F32))

    ob = ob_ref[...]
    for c in range(d_model // MERGE_CHUNK):
        sl = slice(c * MERGE_CHUNK, (c + 1) * MERGE_CHUNK)
        yb_ref[:, sl] = gate(1, c) * jnp.dot(ob, wb_ref[:, sl], preferred_element_type=F32)

    z = cc_ref[...].astype(F32) * cx_ref[...].astype(F32)
    zp = cch_ref[...].astype(F32) * cxh_ref[...].astype(F32)
    zp = jnp.where(first, 0.0, zp)
    w = convw_ref[...]
    row8 = lax.broadcasted_iota(jnp.int32, (SUBLANES, 1), 0)
    y = z * w[CONV_K - 1:CONV_K, :]
    for back in range(1, CONV_K):
        zs = pltpu.roll(z, back, axis=0)
        ps = pltpu.roll(zp, back, axis=0)
        head = jnp.where(row8 < back, ps, zs[:SUBLANES])
        zs = jnp.concatenate([head, zs[SUBLANES:]], axis=0)
        y = y + zs * w[CONV_K - 1 - back:CONV_K - back, :]
    ua_ref[...] = (cb_ref[...].astype(F32) * y).astype(ua_ref.dtype)

    gq = gqx_ref[...]
    for h in range(X_HEADS):
        sl = slice(h * X_HEAD_DIM, (h + 1) * X_HEAD_DIM)
        qn = _rms_rows(xq_ref[:, sl].astype(F32), gq).astype(BF16)
        s = lax.dot_general(qn, mk_ref[:, sl], (((1,), (1,)), ((), ())), preferred_element_type=F32)
        s = s - jnp.max(s, axis=-1, keepdims=True)
        p = jnp.exp(s)
        p = p / jnp.sum(p, axis=-1, keepdims=True)
        oc_ref[:, sl] = jnp.dot(p.astype(BF16), mv_ref[:, sl], preferred_element_type=F32).astype(oc_ref.dtype)

    ua = ua_ref[...]
    oc = oc_ref[...]
    for c in range(d_model // MERGE_CHUNK):
        sl = slice(c * MERGE_CHUNK, (c + 1) * MERGE_CHUNK)
        merged = yb_ref[:, sl]
        for br, act, w_ref in ((0, ua, wa_ref), (2, oc, wc_ref)):
            merged = merged + gate(br, c) * jnp.dot(act, w_ref[:, sl], preferred_element_type=F32)
        mg_ref[:, sl] = merged.astype(mg_ref.dtype)

    o_ref[...] = x_ref[...] + jnp.dot(mg_ref[...], wo_ref[...], preferred_element_type=F32)


def merge(proj, ob, x, conv_w, g_q_x, mk, mv, w_conv_out, w_diff_out, w_x_out, w_o,
          *, layer, batch, cols, tm):
    T, D = x.shape
    S = T // batch
    M = mk.shape[0] // batch
    assert S % tm == 0 and tm % SUBLANES == 0
    tiles_per_seq = S // tm
    halo_blocks = tm // SUBLANES

    def col_spec(name, width):
        assert cols[name] % width == 0
        return pl.BlockSpec((tm, width), lambda i: (i, cols[name] // width))

    def halo_spec(name):
        return pl.BlockSpec((SUBLANES, CONV_WIDTH),
                            lambda i: (jnp.maximum(i * halo_blocks - 1, 0), cols[name] // CONV_WIDTH))

    def resident(shape):
        return pl.BlockSpec((None,) + shape, lambda i: (layer, 0, 0), pipeline_mode=pl.Buffered(1))

    assert cols["gates"] % GATE_BLOCK == 0 and (N_BRANCH * D) % GATE_BLOCK == 0 and D % MERGE_CHUNK == 0
    n_gate_blocks = N_BRANCH * D // GATE_BLOCK
    gate_specs = [pl.BlockSpec((tm, GATE_BLOCK), lambda i, g=g: (i, cols["gates"] // GATE_BLOCK + g))
                  for g in range(n_gate_blocks)]

    return pl.pallas_call(
        functools.partial(_merge_kernel, tiles_per_seq=tiles_per_seq, n_gate_blocks=n_gate_blocks),
        grid=(T // tm,),
        in_specs=gate_specs + [
            col_spec("cb", CONV_WIDTH),
            col_spec("cc", CONV_WIDTH),
            col_spec("cx", CONV_WIDTH),
            halo_spec("cc"),
            halo_spec("cx"),
            col_spec("xq", X_WIDTH),
            pl.BlockSpec((tm, DIFF_VW), lambda i: (i, 0)),
            pl.BlockSpec((tm, D), lambda i: (i, 0)),
            pl.BlockSpec((CONV_K, CONV_WIDTH), lambda i: (0, 0)),
            pl.BlockSpec((1, X_HEAD_DIM), lambda i: (0, 0)),
            pl.BlockSpec((M, X_WIDTH), lambda i: (i // tiles_per_seq, 0)),
            pl.BlockSpec((M, X_WIDTH), lambda i: (i // tiles_per_seq, 0)),
            resident((CONV_WIDTH, D)),
            resident((DIFF_VW, D)),
            resident((X_WIDTH, D)),
            resident((D, D)),
        ],
        out_specs=pl.BlockSpec((tm, D), lambda i: (i, 0)),
        out_shape=jax.ShapeDtypeStruct((T, D), F32),
        scratch_shapes=[
            pltpu.VMEM((tm, CONV_WIDTH), BF16),
            pltpu.VMEM((tm, X_WIDTH), BF16),
            pltpu.VMEM((tm, D), BF16),
            pltpu.VMEM((tm, D), F32),
        ],
        compiler_params=_compiler_params(("parallel",)),
        name="merge",
    )(*([proj] * (n_gate_blocks + 6)), ob, x, conv_w, g_q_x.reshape(1, X_HEAD_DIM),
      mk, mv, w_conv_out, w_diff_out, w_x_out, w_o)


FFN_ROW_GROUPS = 2


def _ffn_kernel(x_ref, g_ref, wg_ref, wu_ref, wd_ref, o_ref, h_ref):
    @pl.when(pl.program_id(1) == 0)
    def _():
        x = x_ref[...]
        h_ref[...] = _rms_rows(x, g_ref[...]).astype(h_ref.dtype)
        o_ref[...] = x

    rows = x_ref.shape[0] // FFN_ROW_GROUPS
    groups = [slice(r * rows, (r + 1) * rows) for r in range(FFN_ROW_GROUPS)]
    gate_up = []
    for sl in groups:
        h = h_ref[sl, :]
        gate_up.append((jnp.dot(h, wg_ref[...], preferred_element_type=F32),
                        jnp.dot(h, wu_ref[...], preferred_element_type=F32)))
    for sl, (a, b) in zip(groups, gate_up):
        act = (a * jax.nn.sigmoid(a) * b).astype(BF16)
        o_ref[sl, :] += jnp.dot(act, wd_ref[...], preferred_element_type=F32)


def ffn(x, g, w_gate_up, w_down, *, layer, tm, tf):
    T, D = x.shape
    d_ff = w_down.shape[1]
    assert T % tm == 0 and d_ff % tf == 0
    nf = d_ff // tf
    return pl.pallas_call(
        _ffn_kernel,
        grid=(T // tm, nf),
        in_specs=[
            pl.BlockSpec((tm, D), lambda i, f: (i, 0)),
            pl.BlockSpec((1, D), lambda i, f: (0, 0)),
            pl.BlockSpec((None, D, tf), lambda i, f: (layer, 0, f)),
            pl.BlockSpec((None, D, tf), lambda i, f: (layer, 0, nf + f)),
            pl.BlockSpec((None, tf, D), lambda i, f: (layer, f, 0)),
        ],
        out_specs=pl.BlockSpec((tm, D), lambda i, f: (i, 0)),
        out_shape=jax.ShapeDtypeStruct((T, D), F32),
        scratch_shapes=[pltpu.VMEM((tm, D), BF16)],
        compiler_params=_compiler_params(("parallel", "arbitrary")),
        name="ffn",
    )(x, g.reshape(1, D), w_gate_up, w_gate_up, w_down)


def _tile(n, want):
    t = min(n, want)
    while n % t:
        t -= SUBLANES
    return t


def kernel(x, mem, positions, g_mix, w_in, conv_w, w_conv_out, g_q_diff, g_k_diff, lambda_vecs,
           g_subln, w_diff_out, g_mem, w_mem_kv, g_q_x, g_k_x, w_x_out, w_o, g_ffn, w_gate_up, w_down):
    B, S, D = x.shape
    M = mem.shape[1]
    T = B * S
    depth = w_in.shape[0]
    d_ff = w_down.shape[1]

    n_conv = 3 * CONV_WIDTH
    n_diff = 2 * DIFF_QK + DIFF_VW
    cols = {
        "cb": 0,
        "cc": CONV_WIDTH,
        "cx": 2 * CONV_WIDTH,
        "dq": n_conv,
        "dk": n_conv + DIFF_QK,
        "dv": n_conv + 2 * DIFF_QK,
        "xq": n_conv + n_diff,
        "gates": n_conv + n_diff + X_WIDTH,
    }

    xf = x.reshape(T, D)
    memf = mem.reshape(B * M, D)

    tm_proj = _tile(T, 1024)
    tq = _tile(S, 1024)
    tk = tq // 2
    pos = positions.reshape(T // tq, 1, tq)
    tm_merge = _tile(S, 256)
    tm_ffn = _tile(T, 512)
    tf = _tile(d_ff, 512)

    w_in, w_mem_kv, w_conv_out, w_diff_out, w_x_out, w_o, w_gate_up, w_down = (
        w.astype(BF16) for w in (w_in, w_mem_kv, w_conv_out, w_diff_out, w_x_out, w_o, w_gate_up, w_down))

    for l in range(depth):
        lam_init = 0.8 - 0.6 * math.exp(-0.3 * l)

        proj = norm_matmul(xf, g_mix[l], w_in, layer=l, tm=tm_proj, tn=1024, out_dtype=BF16)
        qt, k, vt = qk_prep(proj, pos, g_q_diff[l], g_k_diff[l],
                            q_col=cols["dq"], k_col=cols["dk"], v_col=cols["dv"], tm=tq, tk=tk)
        ob = diff_attn(qt, k, vt, lambda_vecs[l], g_subln[l], batch=B, tq=tq, tk=tk, lam_init=lam_init)

        kv = norm_matmul(memf, g_mem[l], w_mem_kv, layer=l, tm=B * M, tn=1024, out_dtype=F32)
        mk, mv = mem_prep(kv, g_k_x[l])

        xf = merge(proj, ob, xf, conv_w[l], g_q_x[l], mk, mv, w_conv_out, w_diff_out, w_x_out, w_o,
                   layer=l, batch=B, cols=cols, tm=tm_merge)
        xf = ffn(xf, g_ffn[l], w_gate_up, w_down, layer=l, tm=tm_ffn, tf=tf)

    return xf.reshape(B, S, D)
```

```python
import functools
import math

import jax
import jax.numpy as jnp
from jax import lax
from jax.experimental import pallas as pl
from jax.experimental.pallas import tpu as pltpu

EPS = 1e-6
CONV_WIDTH = 1024
CONV_K = 3
DIFF_HEADS = 8
DIFF_D = 64
DIFF_V = 2 * DIFF_D
DIFF_QK = DIFF_HEADS * 2 * DIFF_D
DIFF_VW = DIFF_HEADS * DIFF_V
X_HEADS = 4
X_HEAD_DIM = 256
X_WIDTH = X_HEADS * X_HEAD_DIM
N_BRANCH = 3
ROPE_THETA = 500000.0
ROT_FRAC = 4
ROT = DIFF_D // ROT_FRAC
ROT_HALF = ROT // 2

LANES = 128
SUBLANES = 8
BF16_ROWS = 16
V_ROWS = DIFF_V + BF16_ROWS
Q_SCALE = DIFF_D ** -0.5 * math.log2(math.e)
VMEM_LIMIT_BYTES = 56 * 1024 * 1024

F32 = jnp.float32
BF16 = jnp.bfloat16


def _compiler_params(semantics):
    return pltpu.CompilerParams(dimension_semantics=semantics, vmem_limit_bytes=VMEM_LIMIT_BYTES)


def _rms_rows(t, g):
    ms = jnp.mean(t * t, axis=-1, keepdims=True)
    return t * lax.rsqrt(ms + EPS) * g


def _norm_matmul_kernel(x_ref, g_ref, w_ref, o_ref, h_ref):
    @pl.when(pl.program_id(1) == 0)
    def _():
        h_ref[...] = _rms_rows(x_ref[...], g_ref[...]).astype(h_ref.dtype)

    o_ref[...] = jnp.dot(h_ref[...], w_ref[...], preferred_element_type=F32).astype(o_ref.dtype)


def norm_matmul(x, g, w, *, layer, tm, tn, out_dtype):
    T, D = x.shape
    N = w.shape[2]
    assert T % tm == 0 and N % tn == 0
    return pl.pallas_call(
        _norm_matmul_kernel,
        grid=(T // tm, N // tn),
        in_specs=[
            pl.BlockSpec((tm, D), lambda i, j: (i, 0)),
            pl.BlockSpec((1, D), lambda i, j: (0, 0)),
            pl.BlockSpec((None, D, tn), lambda i, j: (layer, 0, j)),
        ],
        out_specs=pl.BlockSpec((tm, tn), lambda i, j: (i, j)),
        out_shape=jax.ShapeDtypeStruct((T, N), out_dtype),
        scratch_shapes=[pltpu.VMEM((tm, D), BF16)],
        compiler_params=_compiler_params(("parallel", "arbitrary")),
        name="norm_matmul",
    )(x, g.reshape(1, D), w)


def _rms_norm_kernel(x_ref, g_ref, o_ref):
    o_ref[...] = _rms_rows(x_ref[...], g_ref[...]).astype(o_ref.dtype)


def rms_norm_rows(x, g, *, tm):
    T, D = x.shape
    assert T % tm == 0
    return pl.pallas_call(
        _rms_norm_kernel,
        grid=(T // tm,),
        in_specs=[pl.BlockSpec((tm, D), lambda i: (i, 0)), pl.BlockSpec((1, D), lambda i: (0, 0))],
        out_specs=pl.BlockSpec((tm, D), lambda i: (i, 0)),
        out_shape=jax.ShapeDtypeStruct((T, D), BF16),
        compiler_params=_compiler_params(("parallel",)),
        name="rms_norm",
    )(x, g.reshape(1, D))


def _in_proj_kernel(h_ref, w_ref, o_ref, wb_ref):
    @pl.when(pl.program_id(1) == 0)
    def _():
        wb_ref[...] = w_ref[...].astype(wb_ref.dtype)

    o_ref[...] = jnp.dot(h_ref[...], wb_ref[...], preferred_element_type=F32).astype(o_ref.dtype)


def in_proj(h, w, *, layer, tm, tn):
    T, D = h.shape
    N = w.shape[2]
    assert T % tm == 0 and N % tn == 0
    return pl.pallas_call(
        _in_proj_kernel,
        grid=(N // tn, T // tm),
        in_specs=[
            pl.BlockSpec((tm, D), lambda j, i: (i, 0)),
            pl.BlockSpec((None, D, tn), lambda j, i: (layer, 0, j)),
        ],
        out_specs=pl.BlockSpec((tm, tn), lambda j, i: (i, j)),
        out_shape=jax.ShapeDtypeStruct((T, N), BF16),
        scratch_shapes=[pltpu.VMEM((D, tn), BF16)],
        compiler_params=_compiler_params(("parallel", "arbitrary")),
        name="in_proj",
    )(h, w)


def _group_mean_sq(t, gsum):
    sq = t * t
    hi = sq.astype(BF16)
    lo = (sq - hi.astype(F32)).astype(BF16)
    return jnp.dot(hi, gsum, preferred_element_type=F32) + jnp.dot(lo, gsum, preferred_element_type=F32)


PREP_ROWS = LANES


def _qk_prep_kernel(pos_ref, invf_ref, expand_ref, keep_ref, gq_ref, gk_ref, q_ref, k_ref, v_ref,
                    qt_ref, ko_ref, vt_ref, *, tk):
    tm = q_ref.shape[0]
    row = lax.broadcasted_iota(jnp.int32, (LANES, LANES), 0)
    col = lax.broadcasted_iota(jnp.int32, (LANES, LANES), 1)
    gsum = jnp.where((row < DIFF_D) == (col < DIFF_D), 1.0 / DIFF_D, 0.0).astype(BF16)
    trig_row = lax.broadcasted_iota(jnp.int32, (2 * ROT_HALF, 1), 0)
    trig_pad = jnp.zeros((LANES - 2 * ROT_HALF, PREP_ROWS), F32)
    pad_row = lax.broadcasted_iota(jnp.int32, (V_ROWS - DIFF_V, PREP_ROWS), 0)
    ones_row = jnp.where(pad_row == 0, 1.0, 0.0).astype(vt_ref.dtype)
    gq = gq_ref[...]
    gk = gk_ref[...]
    expand = expand_ref[...]

    for r0 in range(0, tm, PREP_ROWS):
        ang = invf_ref[...] * pos_ref[:, r0:r0 + PREP_ROWS].astype(F32)
        trig = jnp.where(trig_row < ROT_HALF, jnp.cos(ang), jnp.sin(ang))
        trig = jnp.concatenate([trig, trig_pad], axis=0).T
        hi = trig.astype(BF16)
        lo = (trig - hi.astype(F32)).astype(BF16)
        tab = (jnp.dot(hi, expand, preferred_element_type=F32)
               + jnp.dot(lo, expand, preferred_element_type=F32))
        c_keep = tab[:, :LANES] + keep_ref[...]
        s_from_hi = tab[:, LANES:2 * LANES]
        s_from_lo = tab[:, 2 * LANES:]

        def norm_rot(src_ref, g, c):
            t = src_ref[r0:r0 + PREP_ROWS, c * LANES:(c + 1) * LANES].astype(F32)
            tn = t * lax.rsqrt(_group_mean_sq(t, gsum) + EPS) * g
            up = pltpu.roll(tn, LANES - ROT_HALF, axis=1)
            dn = pltpu.roll(tn, ROT_HALF, axis=1)
            return tn * c_keep + up * s_from_hi + dn * s_from_lo

        u, off = divmod(r0, tk)
        for h in range(DIFF_HEADS):
            sl = slice(h * LANES, (h + 1) * LANES)
            q = norm_rot(q_ref, gq, h)
            qt_ref[h, 0, :, r0:r0 + PREP_ROWS] = q.T.astype(qt_ref.dtype)
            ko_ref[r0:r0 + PREP_ROWS, sl] = norm_rot(k_ref, gk, h).astype(ko_ref.dtype)
            vt = v_ref[r0:r0 + PREP_ROWS, sl].astype(F32).T
            vt_ref[h, u, :DIFF_V, off:off + PREP_ROWS] = vt.astype(vt_ref.dtype)
            vt_ref[h, u, DIFF_V:, off:off + PREP_ROWS] = ones_row


def _rotary_tables():
    inv_freq = ROPE_THETA ** (-jnp.arange(ROT_HALF, dtype=F32) / ROT_HALF)
    invf = jnp.concatenate([inv_freq, inv_freq]).reshape(2 * ROT_HALF, 1)
    src = jnp.arange(LANES)[:, None]
    d = jnp.arange(LANES)[None, :] % DIFF_D
    cos_part = jnp.where((d < ROT) & (src == d % ROT_HALF), 1.0, 0.0)
    hi_part = jnp.where((d < ROT_HALF) & (src == ROT_HALF + d), -1.0, 0.0)
    lo_part = jnp.where((d >= ROT_HALF) & (d < ROT) & (src == d), 1.0, 0.0)
    expand = jnp.concatenate([cos_part, hi_part, lo_part], axis=1).astype(BF16)
    keep = jnp.where(d < ROT, 0.0, 1.0).astype(F32)
    return invf, expand, keep


def qk_prep(proj, pos, g_q, g_k, *, q_col, k_col, v_col, tm, tk):
    T = proj.shape[0]
    assert T % tm == 0 and tm % tk == 0 and tk % PREP_ROWS == 0
    assert q_col % DIFF_QK == 0 and k_col % DIFF_QK == 0 and v_col % DIFF_VW == 0 and DIFF_V == LANES
    gq = (jnp.tile(g_q, LANES // DIFF_D) * Q_SCALE).reshape(1, LANES)
    gk = jnp.tile(g_k, LANES // DIFF_D).reshape(1, LANES)
    invf, expand, keep = _rotary_tables()
    return pl.pallas_call(
        functools.partial(_qk_prep_kernel, tk=tk),
        grid=(T // tm,),
        in_specs=[
            pl.BlockSpec((None, 1, tm), lambda i: (i, 0, 0)),
            pl.BlockSpec((2 * ROT_HALF, 1), lambda i: (0, 0)),
            pl.BlockSpec((LANES, 3 * LANES), lambda i: (0, 0)),
            pl.BlockSpec((1, LANES), lambda i: (0, 0)),
            pl.BlockSpec((1, LANES), lambda i: (0, 0)),
            pl.BlockSpec((1, LANES), lambda i: (0, 0)),
            pl.BlockSpec((tm, DIFF_QK), lambda i: (i, q_col // DIFF_QK)),
            pl.BlockSpec((tm, DIFF_QK), lambda i: (i, k_col // DIFF_QK)),
            pl.BlockSpec((tm, DIFF_VW), lambda i: (i, v_col // DIFF_VW)),
        ],
        out_specs=[
            pl.BlockSpec((DIFF_HEADS, 1, LANES, tm), lambda i: (0, i, 0, 0)),
            pl.BlockSpec((tm, DIFF_QK), lambda i: (i, 0)),
            pl.BlockSpec((DIFF_HEADS, tm // tk, V_ROWS, tk), lambda i: (0, i, 0, 0)),
        ],
        out_shape=[
            jax.ShapeDtypeStruct((DIFF_HEADS, T // tm, LANES, tm), BF16),
            jax.ShapeDtypeStruct((T, DIFF_QK), BF16),
            jax.ShapeDtypeStruct((DIFF_HEADS, T // tk, V_ROWS, tk), BF16),
        ],
        compiler_params=_compiler_params(("parallel",)),
        name="qk_prep",
    )(pos, invf, expand, keep, gq, gk, proj, proj, proj)


def _diff_attn_kernel(lam_ref, g_ref, qt_ref, k_ref, vt_ref, o_ref, m_ref, acc_ref,
                      s0_ref, s1_ref, p0_ref, p1_ref, a0_ref, a1_ref, bm0_ref, bm1_ref,
                      *, tq, tk, lam_init):
    assert tq == 2 * tk
    s_bufs, p_bufs, a_bufs, bm_bufs = (s0_ref, s1_ref), (p0_ref, p1_ref), (a0_ref, a1_ref), (bm0_ref, bm1_ref)
    n_tiles = qt_ref.shape[1]

    lv = lam_ref[...]
    e1 = jnp.exp(jnp.sum(lv[0:1, :] * lv[1:2, :], axis=-1, keepdims=True))
    e2 = jnp.exp(jnp.sum(lv[2:3, :] * lv[3:4, :], axis=-1, keepdims=True))
    lam = e1 - e2 + lam_init

    acc_ref[...] = jnp.zeros(acc_ref.shape, F32)

    def scores(i, u, slot, masked, q0=0):
        k = k_ref[pl.ds(pl.multiple_of(u * tk, tk), tk), :]
        qt = qt_ref[0, i, :, q0:]
        nq_cols = tq - q0
        zeros = jnp.zeros((DIFF_D, nq_cols), qt.dtype)
        q_comp = (jnp.concatenate([qt[:DIFF_D], zeros], axis=0),
                  jnp.concatenate([zeros, qt[DIFF_D:]], axis=0))
        if masked:
            kpos = u * tk + lax.broadcasted_iota(jnp.int32, (tk, nq_cols), 0)
            qpos = i * tq + q0 + lax.broadcasted_iota(jnp.int32, (tk, nq_cols), 1)
            causal = kpos <= qpos
        for c in range(2):
            s = jnp.dot(k, q_comp[c], preferred_element_type=F32)
            if masked:
                s = jnp.where(causal, s, -jnp.inf)
            s_bufs[slot][c, :, q0:] = s
            bm_bufs[slot][c, :, q0:] = jnp.max(s, axis=0, keepdims=True)

    def softmax(slot, first=False, q0=0):
        for c in range(2):
            if first:
                m_new = bm_bufs[slot][c]
                a_bufs[slot][c] = jnp.zeros_like(m_new)
            else:
                m_prev = m_ref[c, :, q0:]
                m_new = jnp.maximum(m_prev, bm_bufs[slot][c, :, q0:])
                a_bufs[slot][c, :, q0:] = jnp.exp2(m_prev - m_new)
            m_ref[c, :, q0:] = m_new
            p_bufs[slot][c, :, q0:] = jnp.exp2(s_bufs[slot][c, :, q0:] - m_new).astype(p_bufs[slot].dtype)

    def accumulate(u, slot, q0=0):
        vt = vt_ref[0, u]
        for c in range(2):
            pv = jnp.dot(vt, p_bufs[slot][c, :, q0:], preferred_element_type=F32)
            acc_ref[c, :, q0:] = a_bufs[slot][c, :, q0:] * acc_ref[c, :, q0:] + pv

    def finalize(i):
        o = (acc_ref[0, :DIFF_V] / acc_ref[0, DIFF_V:DIFF_V + 1]
             - lam * (acc_ref[1, :DIFF_V] / acc_ref[1, DIFF_V:DIFF_V + 1]))
        o = o * lax.rsqrt(jnp.mean(o * o, axis=0, keepdims=True) + EPS)
        rows = pl.ds(pl.multiple_of(i * tq, tq), tq)
        o_ref[rows, :] = (o.T * g_ref[...] * (1.0 - lam_init)).astype(o_ref.dtype)

    def step_pair(i, u, masked):
        scores(i, u, 0, masked)
        accumulate(u - 2, 0)
        softmax(1)
        scores(i, u + 1, 1, masked, q0=tk if masked else 0)
        accumulate(u - 1, 1)
        softmax(0)

    scores(0, 0, 0, masked=True)
    scores(0, 1, 1, masked=True, q0=tk)
    softmax(0, first=True)

    def next_tile(i):
        last = 2 * (i - 1)
        scores(i, 0, 0, masked=False)
        accumulate(last, 0)
        softmax(1, q0=tk)
        scores(i, 1, 1, masked=False)
        accumulate(last + 1, 1, q0=tk)
        finalize(i - 1)
        softmax(0, first=True)

    def full_pairs(i):
        def full_pair(r, c):
            step_pair(i, 2 * r, masked=False)
            return c

        lax.fori_loop(1, i, full_pair, 0)

    if n_tiles > 1:
        next_tile(1)

        def q_tile(i, carry):
            full_pairs(i)
            step_pair(i, 2 * i, masked=True)
            next_tile(i + 1)
            return carry

        lax.fori_loop(1, n_tiles - 1, q_tile, 0)
        full_pairs(n_tiles - 1)
        step_pair(n_tiles - 1, 2 * (n_tiles - 1), masked=True)

    last = 2 * (n_tiles - 1)
    accumulate(last, 0)
    softmax(1, q0=tk)
    accumulate(last + 1, 1, q0=tk)
    finalize(n_tiles - 1)


def diff_attn(qt, k, vt, lambda_vecs, g_subln, *, batch, tq, tk, lam_init):
    T = k.shape[0]
    S = T // batch
    assert S % tq == 0 and tq == 2 * tk
    nq, nk = S // tq, S // tk
    stat = pltpu.VMEM((2, 1, tq), F32)
    return pl.pallas_call(
        functools.partial(_diff_attn_kernel, tq=tq, tk=tk, lam_init=lam_init),
        grid=(batch, DIFF_HEADS),
        in_specs=[
            pl.BlockSpec((4, DIFF_D), lambda b, h: (0, 0)),
            pl.BlockSpec((1, DIFF_V), lambda b, h: (0, 0)),
            pl.BlockSpec((1, nq, LANES, tq), lambda b, h: (h, b, 0, 0)),
            pl.BlockSpec((S, LANES), lambda b, h: (b, h)),
            pl.BlockSpec((1, nk, V_ROWS, tk), lambda b, h: (h, b, 0, 0)),
        ],
        out_specs=pl.BlockSpec((S, DIFF_V), lambda b, h: (b, h)),
        out_shape=jax.ShapeDtypeStruct((T, DIFF_VW), BF16),
        scratch_shapes=[
            stat,
            pltpu.VMEM((2, V_ROWS, tq), F32),
            pltpu.VMEM((2, tk, tq), F32),
            pltpu.VMEM((2, tk, tq), F32),
            pltpu.VMEM((2, tk, tq), BF16),
            pltpu.VMEM((2, tk, tq), BF16),
            stat, stat,
            stat, stat,
        ],
        compiler_params=_compiler_params(("parallel", "parallel")),
        name="diff_attn",
    )(lambda_vecs, g_subln.reshape(1, DIFF_V), qt, k, vt)


def _mem_prep_kernel(kv_ref, g_ref, mk_ref, mv_ref):
    g = g_ref[...]
    for h in range(X_HEADS):
        sl = slice(h * X_HEAD_DIM, (h + 1) * X_HEAD_DIM)
        kh = _rms_rows(kv_ref[:, sl], g) * (X_HEAD_DIM ** -0.5)
        mk_ref[:, sl] = kh.astype(mk_ref.dtype)
    mv_ref[...] = kv_ref[:, X_WIDTH:].astype(mv_ref.dtype)


def mem_prep(kv, g_k_x):
    R = kv.shape[0]
    out = jax.ShapeDtypeStruct((R, X_WIDTH), BF16)
    return pl.pallas_call(
        _mem_prep_kernel,
        grid=(1,),
        in_specs=[
            pl.BlockSpec((R, 2 * X_WIDTH), lambda i: (0, 0)),
            pl.BlockSpec((1, X_HEAD_DIM), lambda i: (0, 0)),
        ],
        out_specs=[pl.BlockSpec((R, X_WIDTH), lambda i: (0, 0))] * 2,
        out_shape=[out, out],
        compiler_params=_compiler_params(("arbitrary",)),
        name="mem_prep",
    )(kv, g_k_x.reshape(1, X_HEAD_DIM))


MERGE_CHUNK = 512
GATE_BLOCK = 1024
assert GATE_BLOCK % MERGE_CHUNK == 0


def _merge_kernel(*refs, tiles_per_seq, n_gate_blocks):
    gate_refs = refs[:n_gate_blocks]
    (cb_ref, cc_ref, cx_ref, cch_ref, cxh_ref, xq_ref, ob_ref, x_ref,
     convw_ref, gqx_ref, mk_ref, mv_ref, wa_ref, wb_ref, wc_ref, wo_ref,
     o_ref, ua_ref, oc_ref, mg_ref, yb_ref) = refs[n_gate_blocks:]
    tm = x_ref.shape[0]
    d_model = x_ref.shape[1]
    first = pl.program_id(0) % tiles_per_seq == 0

    def gate(branch, c):
        blk, off = divmod(branch * d_model + c * MERGE_CHUNK, GATE_BLOCK)
        return jax.nn.sigmoid(gate_refs[blk][:, off:off + MERGE_CHUNK].astype(F32))

    ob = ob_ref[...]
    for c in range(d_model // MERGE_CHUNK):
        sl = slice(c * MERGE_CHUNK, (c + 1) * MERGE_CHUNK)
        yb_ref[:, sl] = gate(1, c) * jnp.dot(ob, wb_ref[:, sl], preferred_element_type=F32)

    z = cc_ref[...].astype(F32) * cx_ref[...].astype(F32)
    zp = cch_ref[...].astype(F32) * cxh_ref[...].astype(F32)
    zp = jnp.where(first, 0.0, zp)
    w = convw_ref[...]
    row8 = lax.broadcasted_iota(jnp.int32, (SUBLANES, 1), 0)
    y = z * w[CONV_K - 1:CONV_K, :]
    for back in range(1, CONV_K):
        zs = pltpu.roll(z, back, axis=0)
        ps = pltpu.roll(zp, back, axis=0)
        head = jnp.where(row8 < back, ps, zs[:SUBLANES])
        zs = jnp.concatenate([head, zs[SUBLANES:]], axis=0)
        y = y + zs * w[CONV_K - 1 - back:CONV_K - back, :]
    ua_ref[...] = (cb_ref[...].astype(F32) * y).astype(ua_ref.dtype)

    gq = gqx_ref[...]
    for h in range(X_HEADS):
        sl = slice(h * X_HEAD_DIM, (h + 1) * X_HEAD_DIM)
        qn = _rms_rows(xq_ref[:, sl].astype(F32), gq).astype(BF16)
        s = lax.dot_general(qn, mk_ref[:, sl], (((1,), (1,)), ((), ())), preferred_element_type=F32)
        s = s - jnp.max(s, axis=-1, keepdims=True)
        p = jnp.exp(s)
        p = p / jnp.sum(p, axis=-1, keepdims=True)
        oc_ref[:, sl] = jnp.dot(p.astype(BF16), mv_ref[:, sl], preferred_element_type=F32).astype(oc_ref.dtype)

    ua = ua_ref[...]
    oc = oc_ref[...]
    for c in range(d_model // MERGE_CHUNK):
        sl = slice(c * MERGE_CHUNK, (c + 1) * MERGE_CHUNK)
        merged = yb_ref[:, sl]
        for br, act, w_ref in ((0, ua, wa_ref), (2, oc, wc_ref)):
            merged = merged + gate(br, c) * jnp.dot(act, w_ref[:, sl], preferred_element_type=F32)
        mg_ref[:, sl] = merged.astype(mg_ref.dtype)

    o_ref[...] = x_ref[...] + jnp.dot(mg_ref[...], wo_ref[...], preferred_element_type=F32)


def merge(proj, ob, x, conv_w, g_q_x, mk, mv, w_conv_out, w_diff_out, w_x_out, w_o,
          *, layer, batch, cols, tm):
    T, D = x.shape
    S = T // batch
    M = mk.shape[0] // batch
    assert S % tm == 0 and tm % SUBLANES == 0
    tiles_per_seq = S // tm
    halo_blocks = tm // SUBLANES

    def col_spec(name, width):
        assert cols[name] % width == 0
        return pl.BlockSpec((tm, width), lambda i: (i, cols[name] // width))

    def halo_spec(name):
        return pl.BlockSpec((SUBLANES, CONV_WIDTH),
                            lambda i: (jnp.maximum(i * halo_blocks - 1, 0), cols[name] // CONV_WIDTH))

    def resident(shape):
        return pl.BlockSpec((None,) + shape, lambda i: (layer, 0, 0), pipeline_mode=pl.Buffered(1))

    assert cols["gates"] % GATE_BLOCK == 0 and (N_BRANCH * D) % GATE_BLOCK == 0 and D % MERGE_CHUNK == 0
    n_gate_blocks = N_BRANCH * D // GATE_BLOCK
    gate_specs = [pl.BlockSpec((tm, GATE_BLOCK), lambda i, g=g: (i, cols["gates"] // GATE_BLOCK + g))
                  for g in range(n_gate_blocks)]

    return pl.pallas_call(
        functools.partial(_merge_kernel, tiles_per_seq=tiles_per_seq, n_gate_blocks=n_gate_blocks),
        grid=(T // tm,),
        in_specs=gate_specs + [
            col_spec("cb", CONV_WIDTH),
            col_spec("cc", CONV_WIDTH),
            col_spec("cx", CONV_WIDTH),
            halo_spec("cc"),
            halo_spec("cx"),
            col_spec("xq", X_WIDTH),
            pl.BlockSpec((tm, DIFF_VW), lambda i: (i, 0)),
            pl.BlockSpec((tm, D), lambda i: (i, 0)),
            pl.BlockSpec((CONV_K, CONV_WIDTH), lambda i: (0, 0)),
            pl.BlockSpec((1, X_HEAD_DIM), lambda i: (0, 0)),
            pl.BlockSpec((M, X_WIDTH), lambda i: (i // tiles_per_seq, 0)),
            pl.BlockSpec((M, X_WIDTH), lambda i: (i // tiles_per_seq, 0)),
            resident((CONV_WIDTH, D)),
            resident((DIFF_VW, D)),
            resident((X_WIDTH, D)),
            resident((D, D)),
        ],
        out_specs=pl.BlockSpec((tm, D), lambda i: (i, 0)),
        out_shape=jax.ShapeDtypeStruct((T, D), F32),
        scratch_shapes=[
            pltpu.VMEM((tm, CONV_WIDTH), BF16),
            pltpu.VMEM((tm, X_WIDTH), BF16),
            pltpu.VMEM((tm, D), BF16),
            pltpu.VMEM((tm, D), F32),
        ],
        compiler_params=_compiler_params(("parallel",)),
        name="merge",
    )(*([proj] * (n_gate_blocks + 6)), ob, x, conv_w, g_q_x.reshape(1, X_HEAD_DIM),
      mk, mv, w_conv_out, w_diff_out, w_x_out, w_o)


def _ffn_kernel(x_ref, g_ref, wg_ref, wu_ref, wd_ref, *rest, normed_out):
    if normed_out:
        gn_ref, o_ref, hn_ref, h_ref = rest
    else:
        o_ref, h_ref = rest

    @pl.when(pl.program_id(1) == 0)
    def _():
        x = x_ref[...]
        h_ref[...] = _rms_rows(x, g_ref[...]).astype(h_ref.dtype)
        o_ref[...] = x

    h = h_ref[...]
    a = jnp.dot(h, wg_ref[...], preferred_element_type=F32)
    b = jnp.dot(h, wu_ref[...], preferred_element_type=F32)
    act = (a * jax.nn.sigmoid(a) * b).astype(BF16)
    o_ref[...] += jnp.dot(act, wd_ref[...], preferred_element_type=F32)

    if normed_out:
        @pl.when(pl.program_id(1) == pl.num_programs(1) - 1)
        def _():
            hn_ref[...] = _rms_rows(o_ref[...], gn_ref[...]).astype(hn_ref.dtype)


def ffn(x, g, w_gate_up, w_down, *, layer, tm, tf, g_next=None):
    T, D = x.shape
    d_ff = w_down.shape[1]
    assert T % tm == 0 and d_ff % tf == 0
    nf = d_ff // tf
    normed_out = g_next is not None
    row_spec = pl.BlockSpec((tm, D), lambda i, f: (i, 0))
    vec_spec = pl.BlockSpec((1, D), lambda i, f: (0, 0))
    in_specs = [
        row_spec,
        vec_spec,
        pl.BlockSpec((None, D, tf), lambda i, f: (layer, 0, f)),
        pl.BlockSpec((None, D, tf), lambda i, f: (layer, 0, nf + f)),
        pl.BlockSpec((None, tf, D), lambda i, f: (layer, f, 0)),
    ]
    args = [x, g.reshape(1, D), w_gate_up, w_gate_up, w_down]
    out_specs, out_shape = row_spec, jax.ShapeDtypeStruct((T, D), F32)
    if normed_out:
        in_specs.append(vec_spec)
        args.append(g_next.reshape(1, D))
        out_specs = [row_spec, row_spec]
        out_shape = [out_shape, jax.ShapeDtypeStruct((T, D), BF16)]
    return pl.pallas_call(
        functools.partial(_ffn_kernel, normed_out=normed_out),
        grid=(T // tm, nf),
        in_specs=in_specs,
        out_specs=out_specs,
        out_shape=out_shape,
        scratch_shapes=[pltpu.VMEM((tm, D), BF16)],
        compiler_params=_compiler_params(("parallel", "arbitrary")),
        name="ffn",
    )(*args)


def _tile(n, want):
    t = min(n, want)
    while n % t:
        t -= SUBLANES
    return t


def kernel(x, mem, positions, g_mix, w_in, conv_w, w_conv_out, g_q_diff, g_k_diff, lambda_vecs,
           g_subln, w_diff_out, g_mem, w_mem_kv, g_q_x, g_k_x, w_x_out, w_o, g_ffn, w_gate_up, w_down):
    B, S, D = x.shape
    M = mem.shape[1]
    T = B * S
    depth = w_in.shape[0]
    d_ff = w_down.shape[1]

    n_conv = 3 * CONV_WIDTH
    n_diff = 2 * DIFF_QK + DIFF_VW
    cols = {
        "cb": 0,
        "cc": CONV_WIDTH,
        "cx": 2 * CONV_WIDTH,
        "dq": n_conv,
        "dk": n_conv + DIFF_QK,
        "dv": n_conv + 2 * DIFF_QK,
        "xq": n_conv + n_diff,
        "gates": n_conv + n_diff + X_WIDTH,
    }

    xf = x.reshape(T, D)
    memf = mem.reshape(B * M, D)

    tm_proj = _tile(T, 2048)
    tq = _tile(S, 1024)
    tk = tq // 2
    pos = positions.reshape(T // tq, 1, tq)
    tm_merge = _tile(S, 256)
    tm_ffn = _tile(T, 512)
    tf = _tile(d_ff, 512)

    w_mem_kv, w_conv_out, w_diff_out, w_x_out, w_o, w_gate_up, w_down = (
        w.astype(BF16) for w in (w_mem_kv, w_conv_out, w_diff_out, w_x_out, w_o, w_gate_up, w_down))

    h = rms_norm_rows(xf, g_mix[0], tm=_tile(T, 1024))
    for l in range(depth):
        lam_init = 0.8 - 0.6 * math.exp(-0.3 * l)

        proj = in_proj(h, w_in, layer=l, tm=tm_proj, tn=1024)
        qt, k, vt = qk_prep(proj, pos, g_q_diff[l], g_k_diff[l],
                            q_col=cols["dq"], k_col=cols["dk"], v_col=cols["dv"], tm=tq, tk=tk)
        ob = diff_attn(qt, k, vt, lambda_vecs[l], g_subln[l], batch=B, tq=tq, tk=tk, lam_init=lam_init)

        kv = norm_matmul(memf, g_mem[l], w_mem_kv, layer=l, tm=B * M, tn=1024, out_dtype=F32)
        mk, mv = mem_prep(kv, g_k_x[l])

        xf = merge(proj, ob, xf, conv_w[l], g_q_x[l], mk, mv, w_conv_out, w_diff_out, w_x_out, w_o,
                   layer=l, batch=B, cols=cols, tm=tm_merge)
        if l + 1 < depth:
            xf, h = ffn(xf, g_ffn[l], w_gate_up, w_down, layer=l, tm=tm_ffn, tf=tf, g_next=g_mix[l + 1])
        else:
            xf = ffn(xf, g_ffn[l], w_gate_up, w_down, layer=l, tm=tm_ffn, tf=tf)

    return xf.reshape(B, S, D)
```

```python
import functools
import math

import jax
import jax.numpy as jnp
from jax import lax
from jax.experimental import pallas as pl
from jax.experimental.pallas import tpu as pltpu

EPS = 1e-6
CONV_WIDTH = 1024
CONV_K = 3
DIFF_HEADS = 8
DIFF_D = 64
DIFF_V = 2 * DIFF_D
DIFF_QK = DIFF_HEADS * 2 * DIFF_D
DIFF_VW = DIFF_HEADS * DIFF_V
X_HEADS = 4
X_HEAD_DIM = 256
X_WIDTH = X_HEADS * X_HEAD_DIM
N_BRANCH = 3
ROPE_THETA = 500000.0
ROT_FRAC = 4
ROT = DIFF_D // ROT_FRAC
ROT_HALF = ROT // 2

LANES = 128
SUBLANES = 8
BF16_ROWS = 16
V_ROWS = DIFF_V + BF16_ROWS
Q_SCALE = DIFF_D ** -0.5 * math.log2(math.e)
VMEM_LIMIT_BYTES = 56 * 1024 * 1024

F32 = jnp.float32
BF16 = jnp.bfloat16


def _compiler_params(semantics):
    return pltpu.CompilerParams(dimension_semantics=semantics, vmem_limit_bytes=VMEM_LIMIT_BYTES)


def _rms_rows(t, g):
    ms = jnp.mean(t * t, axis=-1, keepdims=True)
    return t * lax.rsqrt(ms + EPS) * g


def _norm_matmul_kernel(x_ref, g_ref, w_ref, o_ref, h_ref):
    @pl.when(pl.program_id(1) == 0)
    def _():
        h_ref[...] = _rms_rows(x_ref[...], g_ref[...]).astype(h_ref.dtype)

    o_ref[...] = jnp.dot(h_ref[...], w_ref[...], preferred_element_type=F32).astype(o_ref.dtype)


def norm_matmul(x, g, w, *, layer, tm, tn, out_dtype):
    T, D = x.shape
    N = w.shape[2]
    assert T % tm == 0 and N % tn == 0
    return pl.pallas_call(
        _norm_matmul_kernel,
        grid=(T // tm, N // tn),
        in_specs=[
            pl.BlockSpec((tm, D), lambda i, j: (i, 0)),
            pl.BlockSpec((1, D), lambda i, j: (0, 0)),
            pl.BlockSpec((None, D, tn), lambda i, j: (layer, 0, j)),
        ],
        out_specs=pl.BlockSpec((tm, tn), lambda i, j: (i, j)),
        out_shape=jax.ShapeDtypeStruct((T, N), out_dtype),
        scratch_shapes=[pltpu.VMEM((tm, D), BF16)],
        compiler_params=_compiler_params(("parallel", "arbitrary")),
        name="norm_matmul",
    )(x, g.reshape(1, D), w)


def _rms_norm_kernel(x_ref, g_ref, o_ref):
    o_ref[...] = _rms_rows(x_ref[...], g_ref[...]).astype(o_ref.dtype)


def rms_norm_rows(x, g, *, tm):
    T, D = x.shape
    assert T % tm == 0
    return pl.pallas_call(
        _rms_norm_kernel,
        grid=(T // tm,),
        in_specs=[pl.BlockSpec((tm, D), lambda i: (i, 0)), pl.BlockSpec((1, D), lambda i: (0, 0))],
        out_specs=pl.BlockSpec((tm, D), lambda i: (i, 0)),
        out_shape=jax.ShapeDtypeStruct((T, D), BF16),
        compiler_params=_compiler_params(("parallel",)),
        name="rms_norm",
    )(x, g.reshape(1, D))


def _in_proj_kernel(h_ref, w_ref, o_ref, wb_ref):
    @pl.when(pl.program_id(1) == 0)
    def _():
        wb_ref[...] = w_ref[...].astype(wb_ref.dtype)

    o_ref[...] = jnp.dot(h_ref[...], wb_ref[...], preferred_element_type=F32).astype(o_ref.dtype)


def in_proj(h, w, *, layer, tm, tn):
    T, D = h.shape
    N = w.shape[2]
    assert T % tm == 0 and N % tn == 0
    return pl.pallas_call(
        _in_proj_kernel,
        grid=(N // tn, T // tm),
        in_specs=[
            pl.BlockSpec((tm, D), lambda j, i: (i, 0)),
            pl.BlockSpec((None, D, tn), lambda j, i: (layer, 0, j)),
        ],
        out_specs=pl.BlockSpec((tm, tn), lambda j, i: (i, j)),
        out_shape=jax.ShapeDtypeStruct((T, N), BF16),
        scratch_shapes=[pltpu.VMEM((D, tn), BF16)],
        compiler_params=_compiler_params(("parallel", "arbitrary")),
        name="in_proj",
    )(h, w)


def _group_mean_sq(t, gsum):
    sq = t * t
    hi = sq.astype(BF16)
    lo = (sq - hi.astype(F32)).astype(BF16)
    return jnp.dot(hi, gsum, preferred_element_type=F32) + jnp.dot(lo, gsum, preferred_element_type=F32)


PREP_ROWS = LANES


def _qk_prep_kernel(pos_ref, invf_ref, expand_ref, keep_ref, gq_ref, gk_ref, q_ref, k_ref, v_ref,
                    qt_ref, ko_ref, vt_ref, *, tk):
    tm = q_ref.shape[0]
    row = lax.broadcasted_iota(jnp.int32, (LANES, LANES), 0)
    col = lax.broadcasted_iota(jnp.int32, (LANES, LANES), 1)
    gsum = jnp.where((row < DIFF_D) == (col < DIFF_D), 1.0 / DIFF_D, 0.0).astype(BF16)
    trig_row = lax.broadcasted_iota(jnp.int32, (2 * ROT_HALF, 1), 0)
    trig_pad = jnp.zeros((LANES - 2 * ROT_HALF, PREP_ROWS), F32)
    pad_row = lax.broadcasted_iota(jnp.int32, (V_ROWS - DIFF_V, PREP_ROWS), 0)
    ones_row = jnp.where(pad_row == 0, 1.0, 0.0).astype(vt_ref.dtype)
    gq = gq_ref[...]
    gk = gk_ref[...]
    expand = expand_ref[...]

    for r0 in range(0, tm, PREP_ROWS):
        ang = invf_ref[...] * pos_ref[:, r0:r0 + PREP_ROWS].astype(F32)
        trig = jnp.where(trig_row < ROT_HALF, jnp.cos(ang), jnp.sin(ang))
        trig = jnp.concatenate([trig, trig_pad], axis=0).T
        hi = trig.astype(BF16)
        lo = (trig - hi.astype(F32)).astype(BF16)
        tab = (jnp.dot(hi, expand, preferred_element_type=F32)
               + jnp.dot(lo, expand, preferred_element_type=F32))
        c_keep = tab[:, :LANES] + keep_ref[...]
        s_from_hi = tab[:, LANES:2 * LANES]
        s_from_lo = tab[:, 2 * LANES:]

        def norm_rot(src_ref, g, c):
            t = src_ref[r0:r0 + PREP_ROWS, c * LANES:(c + 1) * LANES].astype(F32)
            tn = t * lax.rsqrt(_group_mean_sq(t, gsum) + EPS) * g
            up = pltpu.roll(tn, LANES - ROT_HALF, axis=1)
            dn = pltpu.roll(tn, ROT_HALF, axis=1)
            return tn * c_keep + up * s_from_hi + dn * s_from_lo

        u, off = divmod(r0, tk)
        for h in range(DIFF_HEADS):
            sl = slice(h * LANES, (h + 1) * LANES)
            q = norm_rot(q_ref, gq, h)
            qt_ref[h, 0, :, r0:r0 + PREP_ROWS] = q.T.astype(qt_ref.dtype)
            ko_ref[r0:r0 + PREP_ROWS, sl] = norm_rot(k_ref, gk, h).astype(ko_ref.dtype)
            vt = v_ref[r0:r0 + PREP_ROWS, sl].astype(F32).T
            vt_ref[h, u, :DIFF_V, off:off + PREP_ROWS] = vt.astype(vt_ref.dtype)
            vt_ref[h, u, DIFF_V:, off:off + PREP_ROWS] = ones_row


def _rotary_tables():
    inv_freq = ROPE_THETA ** (-jnp.arange(ROT_HALF, dtype=F32) / ROT_HALF)
    invf = jnp.concatenate([inv_freq, inv_freq]).reshape(2 * ROT_HALF, 1)
    src = jnp.arange(LANES)[:, None]
    d = jnp.arange(LANES)[None, :] % DIFF_D
    cos_part = jnp.where((d < ROT) & (src == d % ROT_HALF), 1.0, 0.0)
    hi_part = jnp.where((d < ROT_HALF) & (src == ROT_HALF + d), -1.0, 0.0)
    lo_part = jnp.where((d >= ROT_HALF) & (d < ROT) & (src == d), 1.0, 0.0)
    expand = jnp.concatenate([cos_part, hi_part, lo_part], axis=1).astype(BF16)
    keep = jnp.where(d < ROT, 0.0, 1.0).astype(F32)
    return invf, expand, keep


def qk_prep(proj, pos, g_q, g_k, *, q_col, k_col, v_col, tm, tk):
    T = proj.shape[0]
    assert T % tm == 0 and tm % tk == 0 and tk % PREP_ROWS == 0
    assert q_col % DIFF_QK == 0 and k_col % DIFF_QK == 0 and v_col % DIFF_VW == 0 and DIFF_V == LANES
    gq = (jnp.tile(g_q, LANES // DIFF_D) * Q_SCALE).reshape(1, LANES)
    gk = jnp.tile(g_k, LANES // DIFF_D).reshape(1, LANES)
    invf, expand, keep = _rotary_tables()
    return pl.pallas_call(
        functools.partial(_qk_prep_kernel, tk=tk),
        grid=(T // tm,),
        in_specs=[
            pl.BlockSpec((None, 1, tm), lambda i: (i, 0, 0)),
            pl.BlockSpec((2 * ROT_HALF, 1), lambda i: (0, 0)),
            pl.BlockSpec((LANES, 3 * LANES), lambda i: (0, 0)),
            pl.BlockSpec((1, LANES), lambda i: (0, 0)),
            pl.BlockSpec((1, LANES), lambda i: (0, 0)),
            pl.BlockSpec((1, LANES), lambda i: (0, 0)),
            pl.BlockSpec((tm, DIFF_QK), lambda i: (i, q_col // DIFF_QK)),
            pl.BlockSpec((tm, DIFF_QK), lambda i: (i, k_col // DIFF_QK)),
            pl.BlockSpec((tm, DIFF_VW), lambda i: (i, v_col // DIFF_VW)),
        ],
        out_specs=[
            pl.BlockSpec((DIFF_HEADS, 1, LANES, tm), lambda i: (0, i, 0, 0)),
            pl.BlockSpec((tm, DIFF_QK), lambda i: (i, 0)),
            pl.BlockSpec((DIFF_HEADS, tm // tk, V_ROWS, tk), lambda i: (0, i, 0, 0)),
        ],
        out_shape=[
            jax.ShapeDtypeStruct((DIFF_HEADS, T // tm, LANES, tm), BF16),
            jax.ShapeDtypeStruct((T, DIFF_QK), BF16),
            jax.ShapeDtypeStruct((DIFF_HEADS, T // tk, V_ROWS, tk), BF16),
        ],
        compiler_params=_compiler_params(("parallel",)),
        name="qk_prep",
    )(pos, invf, expand, keep, gq, gk, proj, proj, proj)


def _diff_attn_kernel(lam_ref, g_ref, qt_ref, k_ref, vt_ref, o_ref, m_ref, acc_ref,
                      s0_ref, s1_ref, p0_ref, p1_ref, a0_ref, a1_ref, bm0_ref, bm1_ref,
                      *, tq, tk, lam_init):
    assert tq == 2 * tk
    s_bufs, p_bufs, a_bufs, bm_bufs = (s0_ref, s1_ref), (p0_ref, p1_ref), (a0_ref, a1_ref), (bm0_ref, bm1_ref)
    n_tiles = qt_ref.shape[1]

    lv = lam_ref[...]
    e1 = jnp.exp(jnp.sum(lv[0:1, :] * lv[1:2, :], axis=-1, keepdims=True))
    e2 = jnp.exp(jnp.sum(lv[2:3, :] * lv[3:4, :], axis=-1, keepdims=True))
    lam = e1 - e2 + lam_init

    acc_ref[...] = jnp.zeros(acc_ref.shape, F32)

    def scores(i, u, slot, masked, q0=0):
        k = k_ref[pl.ds(pl.multiple_of(u * tk, tk), tk), :]
        qt = qt_ref[0, i, :, q0:]
        nq_cols = tq - q0
        zeros = jnp.zeros((DIFF_D, nq_cols), qt.dtype)
        q_comp = (jnp.concatenate([qt[:DIFF_D], zeros], axis=0),
                  jnp.concatenate([zeros, qt[DIFF_D:]], axis=0))
        if masked:
            kpos = u * tk + lax.broadcasted_iota(jnp.int32, (tk, nq_cols), 0)
            qpos = i * tq + q0 + lax.broadcasted_iota(jnp.int32, (tk, nq_cols), 1)
            causal = kpos <= qpos
        for c in range(2):
            s = jnp.dot(k, q_comp[c], preferred_element_type=F32)
            if masked:
                s = jnp.where(causal, s, -jnp.inf)
            s_bufs[slot][c, :, q0:] = s
            bm_bufs[slot][c, :, q0:] = jnp.max(s, axis=0, keepdims=True)

    def softmax(slot, first=False, q0=0):
        for c in range(2):
            if first:
                m_new = bm_bufs[slot][c]
                a_bufs[slot][c] = jnp.zeros_like(m_new)
            else:
                m_prev = m_ref[c, :, q0:]
                m_new = jnp.maximum(m_prev, bm_bufs[slot][c, :, q0:])
                a_bufs[slot][c, :, q0:] = jnp.exp2(m_prev - m_new)
            m_ref[c, :, q0:] = m_new
            p_bufs[slot][c, :, q0:] = jnp.exp2(s_bufs[slot][c, :, q0:] - m_new).astype(p_bufs[slot].dtype)

    def accumulate(u, slot, q0=0):
        vt = vt_ref[0, u]
        for c in range(2):
            pv = jnp.dot(vt, p_bufs[slot][c, :, q0:], preferred_element_type=F32)
            acc_ref[c, :, q0:] = a_bufs[slot][c, :, q0:] * acc_ref[c, :, q0:] + pv

    def finalize(i):
        o = (acc_ref[0, :DIFF_V] / acc_ref[0, DIFF_V:DIFF_V + 1]
             - lam * (acc_ref[1, :DIFF_V] / acc_ref[1, DIFF_V:DIFF_V + 1]))
        o = o * lax.rsqrt(jnp.mean(o * o, axis=0, keepdims=True) + EPS)
        rows = pl.ds(pl.multiple_of(i * tq, tq), tq)
        o_ref[rows, :] = (o.T * g_ref[...] * (1.0 - lam_init)).astype(o_ref.dtype)

    def step_pair(i, u, masked):
        scores(i, u, 0, masked)
        accumulate(u - 2, 0)
        softmax(1)
        scores(i, u + 1, 1, masked, q0=tk if masked else 0)
        accumulate(u - 1, 1)
        softmax(0)

    scores(0, 0, 0, masked=True)
    scores(0, 1, 1, masked=True, q0=tk)
    softmax(0, first=True)

    def next_tile(i):
        last = 2 * (i - 1)
        scores(i, 0, 0, masked=False)
        accumulate(last, 0)
        softmax(1, q0=tk)
        scores(i, 1, 1, masked=False)
        accumulate(last + 1, 1, q0=tk)
        finalize(i - 1)
        softmax(0, first=True)

    def full_pairs(i):
        def full_pair(r, c):
            step_pair(i, 2 * r, masked=False)
            return c

        lax.fori_loop(1, i, full_pair, 0)

    if n_tiles > 1:
        next_tile(1)

        def q_tile(i, carry):
            full_pairs(i)
            step_pair(i, 2 * i, masked=True)
            next_tile(i + 1)
            return carry

        lax.fori_loop(1, n_tiles - 1, q_tile, 0)
        full_pairs(n_tiles - 1)
        step_pair(n_tiles - 1, 2 * (n_tiles - 1), masked=True)

    last = 2 * (n_tiles - 1)
    accumulate(last, 0)
    softmax(1, q0=tk)
    accumulate(last + 1, 1, q0=tk)
    finalize(n_tiles - 1)


def diff_attn(qt, k, vt, lambda_vecs, g_subln, *, batch, tq, tk, lam_init):
    T = k.shape[0]
    S = T // batch
    assert S % tq == 0 and tq == 2 * tk
    nq, nk = S // tq, S // tk
    stat = pltpu.VMEM((2, 1, tq), F32)
    return pl.pallas_call(
        functools.partial(_diff_attn_kernel, tq=tq, tk=tk, lam_init=lam_init),
        grid=(batch, DIFF_HEADS),
        in_specs=[
            pl.BlockSpec((4, DIFF_D), lambda b, h: (0, 0)),
            pl.BlockSpec((1, DIFF_V), lambda b, h: (0, 0)),
            pl.BlockSpec((1, nq, LANES, tq), lambda b, h: (h, b, 0, 0)),
            pl.BlockSpec((S, LANES), lambda b, h: (b, h)),
            pl.BlockSpec((1, nk, V_ROWS, tk), lambda b, h: (h, b, 0, 0)),
        ],
        out_specs=pl.BlockSpec((S, DIFF_V), lambda b, h: (b, h)),
        out_shape=jax.ShapeDtypeStruct((T, DIFF_VW), BF16),
        scratch_shapes=[
            stat,
            pltpu.VMEM((2, V_ROWS, tq), F32),
            pltpu.VMEM((2, tk, tq), F32),
            pltpu.VMEM((2, tk, tq), F32),
            pltpu.VMEM((2, tk, tq), BF16),
            pltpu.VMEM((2, tk, tq), BF16),
            stat, stat,
            stat, stat,
        ],
        compiler_params=_compiler_params(("parallel", "parallel")),
        name="diff_attn",
    )(lambda_vecs, g_subln.reshape(1, DIFF_V), qt, k, vt)


def _mem_prep_kernel(kv_ref, g_ref, mk_ref, mv_ref):
    g = g_ref[...]
    for h in range(X_HEADS):
        sl = slice(h * X_HEAD_DIM, (h + 1) * X_HEAD_DIM)
        kh = _rms_rows(kv_ref[:, sl], g) * (X_HEAD_DIM ** -0.5)
        mk_ref[:, sl] = kh.astype(mk_ref.dtype)
    mv_ref[...] = kv_ref[:, X_WIDTH:].astype(mv_ref.dtype)


def mem_prep(kv, g_k_x):
    R = kv.shape[0]
    out = jax.ShapeDtypeStruct((R, X_WIDTH), BF16)
    return pl.pallas_call(
        _mem_prep_kernel,
        grid=(1,),
        in_specs=[
            pl.BlockSpec((R, 2 * X_WIDTH), lambda i: (0, 0)),
            pl.BlockSpec((1, X_HEAD_DIM), lambda i: (0, 0)),
        ],
        out_specs=[pl.BlockSpec((R, X_WIDTH), lambda i: (0, 0))] * 2,
        out_shape=[out, out],
        compiler_params=_compiler_params(("arbitrary",)),
        name="mem_prep",
    )(kv, g_k_x.reshape(1, X_HEAD_DIM))


MERGE_CHUNK = 512
GATE_BLOCK = 1024
assert GATE_BLOCK % MERGE_CHUNK == 0


def _merge_kernel(*refs, tiles_per_seq, n_gate_blocks):
    gate_refs = refs[:n_gate_blocks]
    (cb_ref, cc_ref, cx_ref, cch_ref, cxh_ref, xq_ref, ob_ref, x_ref,
     convw_ref, gqx_ref, gn_ref, mk_ref, mv_ref, wa_ref, wb_ref, wc_ref, wo_ref,
     o_ref, hn_ref, ua_ref, oc_ref, mg_ref, yb_ref) = refs[n_gate_blocks:]
    tm = x_ref.shape[0]
    d_model = x_ref.shape[1]
    first = pl.program_id(0) % tiles_per_seq == 0

    def gate(branch, c):
        blk, off = divmod(branch * d_model + c * MERGE_CHUNK, GATE_BLOCK)
        return jax.nn.sigmoid(gate_refs[blk][:, off:off + MERGE_CHUNK].astype(F32))

    ob = ob_ref[...]
    for c in range(d_model // MERGE_CHUNK):
        sl = slice(c * MERGE_CHUNK, (c + 1) * MERGE_CHUNK)
        yb_ref[:, sl] = gate(1, c) * jnp.dot(ob, wb_ref[:, sl], preferred_element_type=F32)

    z = cc_ref[...].astype(F32) * cx_ref[...].astype(F32)
    zp = cch_ref[...].astype(F32) * cxh_ref[...].astype(F32)
    zp = jnp.where(first, 0.0, zp)
    w = convw_ref[...]
    row8 = lax.broadcasted_iota(jnp.int32, (SUBLANES, 1), 0)
    y = z * w[CONV_K - 1:CONV_K, :]
    for back in range(1, CONV_K):
        zs = pltpu.roll(z, back, axis=0)
        ps = pltpu.roll(zp, back, axis=0)
        head = jnp.where(row8 < back, ps, zs[:SUBLANES])
        zs = jnp.concatenate([head, zs[SUBLANES:]], axis=0)
        y = y + zs * w[CONV_K - 1 - back:CONV_K - back, :]
    ua_ref[...] = (cb_ref[...].astype(F32) * y).astype(ua_ref.dtype)

    gq = gqx_ref[...]
    for h in range(X_HEADS):
        sl = slice(h * X_HEAD_DIM, (h + 1) * X_HEAD_DIM)
        qn = _rms_rows(xq_ref[:, sl].astype(F32), gq).astype(BF16)
        s = lax.dot_general(qn, mk_ref[:, sl], (((1,), (1,)), ((), ())), preferred_element_type=F32)
        s = s - jnp.max(s, axis=-1, keepdims=True)
        p = jnp.exp(s)
        p = p / jnp.sum(p, axis=-1, keepdims=True)
        oc_ref[:, sl] = jnp.dot(p.astype(BF16), mv_ref[:, sl], preferred_element_type=F32).astype(oc_ref.dtype)

    ua = ua_ref[...]
    oc = oc_ref[...]
    for c in range(d_model // MERGE_CHUNK):
        sl = slice(c * MERGE_CHUNK, (c + 1) * MERGE_CHUNK)
        merged = yb_ref[:, sl]
        for br, act, w_ref in ((0, ua, wa_ref), (2, oc, wc_ref)):
            merged = merged + gate(br, c) * jnp.dot(act, w_ref[:, sl], preferred_element_type=F32)
        mg_ref[:, sl] = merged.astype(mg_ref.dtype)

    y = x_ref[...] + jnp.dot(mg_ref[...], wo_ref[...], preferred_element_type=F32)
    o_ref[...] = y
    hn_ref[...] = _rms_rows(y, gn_ref[...]).astype(hn_ref.dtype)


def merge(proj, ob, x, conv_w, g_q_x, g_next, mk, mv, w_conv_out, w_diff_out, w_x_out, w_o,
          *, layer, batch, cols, tm):
    T, D = x.shape
    S = T // batch
    M = mk.shape[0] // batch
    assert S % tm == 0 and tm % SUBLANES == 0
    tiles_per_seq = S // tm
    halo_blocks = tm // SUBLANES

    def col_spec(name, width):
        assert cols[name] % width == 0
        return pl.BlockSpec((tm, width), lambda i: (i, cols[name] // width))

    def halo_spec(name):
        return pl.BlockSpec((SUBLANES, CONV_WIDTH),
                            lambda i: (jnp.maximum(i * halo_blocks - 1, 0), cols[name] // CONV_WIDTH))

    def resident(shape):
        return pl.BlockSpec((None,) + shape, lambda i: (layer, 0, 0), pipeline_mode=pl.Buffered(1))

    assert cols["gates"] % GATE_BLOCK == 0 and (N_BRANCH * D) % GATE_BLOCK == 0 and D % MERGE_CHUNK == 0
    n_gate_blocks = N_BRANCH * D // GATE_BLOCK
    gate_specs = [pl.BlockSpec((tm, GATE_BLOCK), lambda i, g=g: (i, cols["gates"] // GATE_BLOCK + g))
                  for g in range(n_gate_blocks)]

    return pl.pallas_call(
        functools.partial(_merge_kernel, tiles_per_seq=tiles_per_seq, n_gate_blocks=n_gate_blocks),
        grid=(T // tm,),
        in_specs=gate_specs + [
            col_spec("cb", CONV_WIDTH),
            col_spec("cc", CONV_WIDTH),
            col_spec("cx", CONV_WIDTH),
            halo_spec("cc"),
            halo_spec("cx"),
            col_spec("xq", X_WIDTH),
            pl.BlockSpec((tm, DIFF_VW), lambda i: (i, 0)),
            pl.BlockSpec((tm, D), lambda i: (i, 0)),
            pl.BlockSpec((CONV_K, CONV_WIDTH), lambda i: (0, 0)),
            pl.BlockSpec((1, X_HEAD_DIM), lambda i: (0, 0)),
            pl.BlockSpec((1, D), lambda i: (0, 0)),
            pl.BlockSpec((M, X_WIDTH), lambda i: (i // tiles_per_seq, 0)),
            pl.BlockSpec((M, X_WIDTH), lambda i: (i // tiles_per_seq, 0)),
            resident((CONV_WIDTH, D)),
            resident((DIFF_VW, D)),
            resident((X_WIDTH, D)),
            resident((D, D)),
        ],
        out_specs=[pl.BlockSpec((tm, D), lambda i: (i, 0))] * 2,
        out_shape=[jax.ShapeDtypeStruct((T, D), F32), jax.ShapeDtypeStruct((T, D), BF16)],
        scratch_shapes=[
            pltpu.VMEM((tm, CONV_WIDTH), BF16),
            pltpu.VMEM((tm, X_WIDTH), BF16),
            pltpu.VMEM((tm, D), BF16),
            pltpu.VMEM((tm, D), F32),
        ],
        compiler_params=_compiler_params(("parallel",)),
        name="merge",
    )(*([proj] * (n_gate_blocks + 6)), ob, x, conv_w, g_q_x.reshape(1, X_HEAD_DIM), g_next.reshape(1, D),
      mk, mv, w_conv_out, w_diff_out, w_x_out, w_o)


def _ffn_up_kernel(h_ref, wg_ref, wu_ref, o_ref, wgb_ref, wub_ref):
    @pl.when(pl.program_id(1) == 0)
    def _():
        wgb_ref[...] = wg_ref[...].astype(wgb_ref.dtype)
        wub_ref[...] = wu_ref[...].astype(wub_ref.dtype)

    h = h_ref[...]
    a = jnp.dot(h, wgb_ref[...], preferred_element_type=F32)
    b = jnp.dot(h, wub_ref[...], preferred_element_type=F32)
    o_ref[...] = (a * jax.nn.sigmoid(a) * b).astype(o_ref.dtype)


def ffn_up(h, w_gate_up, *, layer, tm, tf):
    T, D = h.shape
    d_ff = w_gate_up.shape[2] // 2
    assert T % tm == 0 and d_ff % tf == 0
    nf = d_ff // tf
    return pl.pallas_call(
        _ffn_up_kernel,
        grid=(nf, T // tm),
        in_specs=[
            pl.BlockSpec((tm, D), lambda f, i: (i, 0)),
            pl.BlockSpec((None, D, tf), lambda f, i: (layer, 0, f)),
            pl.BlockSpec((None, D, tf), lambda f, i: (layer, 0, nf + f)),
        ],
        out_specs=pl.BlockSpec((tm, tf), lambda f, i: (i, f)),
        out_shape=jax.ShapeDtypeStruct((T, d_ff), BF16),
        scratch_shapes=[pltpu.VMEM((D, tf), BF16), pltpu.VMEM((D, tf), BF16)],
        compiler_params=_compiler_params(("parallel", "arbitrary")),
        name="ffn_up",
    )(h, w_gate_up, w_gate_up)


def _ffn_down_kernel(a_ref, x_ref, wd_ref, *rest, normed_out):
    if normed_out:
        gn_ref, o_ref, hn_ref = rest
    else:
        (o_ref,) = rest
    y = x_ref[...] + jnp.dot(a_ref[...], wd_ref[...], preferred_element_type=F32)
    o_ref[...] = y
    if normed_out:
        hn_ref[...] = _rms_rows(y, gn_ref[...]).astype(hn_ref.dtype)


def ffn_down(act, x, w_down, *, layer, tm, g_next=None):
    T, D = x.shape
    d_ff = act.shape[1]
    assert T % tm == 0
    normed_out = g_next is not None
    row_spec = pl.BlockSpec((tm, D), lambda i: (i, 0))
    in_specs = [
        pl.BlockSpec((tm, d_ff), lambda i: (i, 0)),
        row_spec,
        pl.BlockSpec((None, d_ff, D), lambda i: (layer, 0, 0), pipeline_mode=pl.Buffered(1)),
    ]
    args = [act, x, w_down]
    out_specs, out_shape = row_spec, jax.ShapeDtypeStruct((T, D), F32)
    if normed_out:
        in_specs.append(pl.BlockSpec((1, D), lambda i: (0, 0)))
        args.append(g_next.reshape(1, D))
        out_specs = [row_spec, row_spec]
        out_shape = [out_shape, jax.ShapeDtypeStruct((T, D), BF16)]
    return pl.pallas_call(
        functools.partial(_ffn_down_kernel, normed_out=normed_out),
        grid=(T // tm,),
        in_specs=in_specs,
        out_specs=out_specs,
        out_shape=out_shape,
        compiler_params=_compiler_params(("parallel",)),
        name="ffn_down",
    )(*args)


def _tile(n, want):
    t = min(n, want)
    while n % t:
        t -= SUBLANES
    return t


def kernel(x, mem, positions, g_mix, w_in, conv_w, w_conv_out, g_q_diff, g_k_diff, lambda_vecs,
           g_subln, w_diff_out, g_mem, w_mem_kv, g_q_x, g_k_x, w_x_out, w_o, g_ffn, w_gate_up, w_down):
    B, S, D = x.shape
    M = mem.shape[1]
    T = B * S
    depth = w_in.shape[0]
    d_ff = w_down.shape[1]

    n_conv = 3 * CONV_WIDTH
    n_diff = 2 * DIFF_QK + DIFF_VW
    cols = {
        "cb": 0,
        "cc": CONV_WIDTH,
        "cx": 2 * CONV_WIDTH,
        "dq": n_conv,
        "dk": n_conv + DIFF_QK,
        "dv": n_conv + 2 * DIFF_QK,
        "xq": n_conv + n_diff,
        "gates": n_conv + n_diff + X_WIDTH,
    }

    xf = x.reshape(T, D)
    memf = mem.reshape(B * M, D)

    tm_proj = _tile(T, 2048)
    tq = _tile(S, 1024)
    tk = tq // 2
    pos = positions.reshape(T // tq, 1, tq)
    tm_merge = _tile(S, 256)
    tm_up = _tile(T, 2048)
    tf = _tile(d_ff, 512)
    tm_down = _tile(T, 256)

    w_mem_kv, w_conv_out, w_diff_out, w_x_out, w_o, w_down = (
        w.astype(BF16) for w in (w_mem_kv, w_conv_out, w_diff_out, w_x_out, w_o, w_down))

    h = rms_norm_rows(xf, g_mix[0], tm=_tile(T, 1024))
    for l in range(depth):
        lam_init = 0.8 - 0.6 * math.exp(-0.3 * l)

        proj = in_proj(h, w_in, layer=l, tm=tm_proj, tn=1024)
        qt, k, vt = qk_prep(proj, pos, g_q_diff[l], g_k_diff[l],
                            q_col=cols["dq"], k_col=cols["dk"], v_col=cols["dv"], tm=tq, tk=tk)
        ob = diff_attn(qt, k, vt, lambda_vecs[l], g_subln[l], batch=B, tq=tq, tk=tk, lam_init=lam_init)

        kv = norm_matmul(memf, g_mem[l], w_mem_kv, layer=l, tm=B * M, tn=1024, out_dtype=F32)
        mk, mv = mem_prep(kv, g_k_x[l])

        xf, h2 = merge(proj, ob, xf, conv_w[l], g_q_x[l], g_ffn[l], mk, mv,
                       w_conv_out, w_diff_out, w_x_out, w_o, layer=l, batch=B, cols=cols, tm=tm_merge)
        act = ffn_up(h2, w_gate_up, layer=l, tm=tm_up, tf=tf)
        if l + 1 < depth:
            xf, h = ffn_down(act, xf, w_down, layer=l, tm=tm_down, g_next=g_mix[l + 1])
        else:
            xf = ffn_down(act, xf, w_down, layer=l, tm=tm_down)

    return xf.reshape(B, S, D)
```

```python
import functools
import math

import jax
import jax.numpy as jnp
from jax import lax
from jax.experimental import pallas as pl
from jax.experimental.pallas import tpu as pltpu

EPS = 1e-6
CONV_WIDTH = 1024
CONV_K = 3
DIFF_HEADS = 8
DIFF_D = 64
DIFF_V = 2 * DIFF_D
DIFF_QK = DIFF_HEADS * 2 * DIFF_D
DIFF_VW = DIFF_HEADS * DIFF_V
X_HEADS = 4
X_HEAD_DIM = 256
X_WIDTH = X_HEADS * X_HEAD_DIM
N_BRANCH = 3
ROPE_THETA = 500000.0
ROT_FRAC = 4
ROT = DIFF_D // ROT_FRAC
ROT_HALF = ROT // 2

LANES = 128
SUBLANES = 8
BF16_ROWS = 16
V_ROWS = DIFF_V + BF16_ROWS
Q_SCALE = DIFF_D ** -0.5 * math.log2(math.e)
VMEM_LIMIT_BYTES = 56 * 1024 * 1024

F32 = jnp.float32
BF16 = jnp.bfloat16


def _compiler_params(semantics):
    return pltpu.CompilerParams(dimension_semantics=semantics, vmem_limit_bytes=VMEM_LIMIT_BYTES)


def _rms_rows(t, g):
    ms = jnp.mean(t * t, axis=-1, keepdims=True)
    return t * lax.rsqrt(ms + EPS) * g


def _norm_matmul_kernel(x_ref, g_ref, w_ref, o_ref, h_ref):
    @pl.when(pl.program_id(1) == 0)
    def _():
        h_ref[...] = _rms_rows(x_ref[...], g_ref[...]).astype(h_ref.dtype)

    o_ref[...] = jnp.dot(h_ref[...], w_ref[...], preferred_element_type=F32).astype(o_ref.dtype)


def norm_matmul(x, g, w, *, layer, tm, tn, out_dtype):
    T, D = x.shape
    N = w.shape[2]
    assert T % tm == 0 and N % tn == 0
    return pl.pallas_call(
        _norm_matmul_kernel,
        grid=(T // tm, N // tn),
        in_specs=[
            pl.BlockSpec((tm, D), lambda i, j: (i, 0)),
            pl.BlockSpec((1, D), lambda i, j: (0, 0)),
            pl.BlockSpec((None, D, tn), lambda i, j: (layer, 0, j)),
        ],
        out_specs=pl.BlockSpec((tm, tn), lambda i, j: (i, j)),
        out_shape=jax.ShapeDtypeStruct((T, N), out_dtype),
        scratch_shapes=[pltpu.VMEM((tm, D), BF16)],
        compiler_params=_compiler_params(("parallel", "arbitrary")),
        name="norm_matmul",
    )(x, g.reshape(1, D), w)


def _rms_norm_kernel(x_ref, g_ref, o_ref):
    o_ref[...] = _rms_rows(x_ref[...], g_ref[...]).astype(o_ref.dtype)


def rms_norm_rows(x, g, *, tm):
    T, D = x.shape
    assert T % tm == 0
    return pl.pallas_call(
        _rms_norm_kernel,
        grid=(T // tm,),
        in_specs=[pl.BlockSpec((tm, D), lambda i: (i, 0)), pl.BlockSpec((1, D), lambda i: (0, 0))],
        out_specs=pl.BlockSpec((tm, D), lambda i: (i, 0)),
        out_shape=jax.ShapeDtypeStruct((T, D), BF16),
        compiler_params=_compiler_params(("parallel",)),
        name="rms_norm",
    )(x, g.reshape(1, D))


def _in_proj_kernel(h_ref, w_ref, o_ref, wb_ref):
    @pl.when(pl.program_id(1) == 0)
    def _():
        wb_ref[...] = w_ref[...].astype(wb_ref.dtype)

    o_ref[...] = jnp.dot(h_ref[...], wb_ref[...], preferred_element_type=F32).astype(o_ref.dtype)


def in_proj(h, w, *, layer, tm, tn):
    T, D = h.shape
    N = w.shape[2]
    assert T % tm == 0 and N % tn == 0
    return pl.pallas_call(
        _in_proj_kernel,
        grid=(N // tn, T // tm),
        in_specs=[
            pl.BlockSpec((tm, D), lambda j, i: (i, 0)),
            pl.BlockSpec((None, D, tn), lambda j, i: (layer, 0, j)),
        ],
        out_specs=pl.BlockSpec((tm, tn), lambda j, i: (i, j)),
        out_shape=jax.ShapeDtypeStruct((T, N), BF16),
        scratch_shapes=[pltpu.VMEM((D, tn), BF16)],
        compiler_params=_compiler_params(("parallel", "arbitrary")),
        name="in_proj",
    )(h, w)


def _group_mean_sq(t, gsum):
    sq = t * t
    hi = sq.astype(BF16)
    lo = (sq - hi.astype(F32)).astype(BF16)
    return jnp.dot(hi, gsum, preferred_element_type=F32) + jnp.dot(lo, gsum, preferred_element_type=F32)


PREP_ROWS = LANES


def _qk_prep_kernel(pos_ref, invf_ref, expand_ref, keep_ref, gq_ref, gk_ref, q_ref, k_ref, v_ref,
                    qt_ref, ko_ref, vt_ref, *, tk):
    tm = q_ref.shape[0]
    row = lax.broadcasted_iota(jnp.int32, (LANES, LANES), 0)
    col = lax.broadcasted_iota(jnp.int32, (LANES, LANES), 1)
    gsum = jnp.where((row < DIFF_D) == (col < DIFF_D), 1.0 / DIFF_D, 0.0).astype(BF16)
    trig_row = lax.broadcasted_iota(jnp.int32, (2 * ROT_HALF, 1), 0)
    trig_pad = jnp.zeros((LANES - 2 * ROT_HALF, PREP_ROWS), F32)
    pad_row = lax.broadcasted_iota(jnp.int32, (V_ROWS - DIFF_V, PREP_ROWS), 0)
    ones_row = jnp.where(pad_row == 0, 1.0, 0.0).astype(vt_ref.dtype)
    gq = gq_ref[...]
    gk = gk_ref[...]
    expand = expand_ref[...]

    for r0 in range(0, tm, PREP_ROWS):
        ang = invf_ref[...] * pos_ref[:, r0:r0 + PREP_ROWS].astype(F32)
        trig = jnp.where(trig_row < ROT_HALF, jnp.cos(ang), jnp.sin(ang))
        trig = jnp.concatenate([trig, trig_pad], axis=0).T
        hi = trig.astype(BF16)
        lo = (trig - hi.astype(F32)).astype(BF16)
        tab = (jnp.dot(hi, expand, preferred_element_type=F32)
               + jnp.dot(lo, expand, preferred_element_type=F32))
        c_keep = tab[:, :LANES] + keep_ref[...]
        s_from_hi = tab[:, LANES:2 * LANES]
        s_from_lo = tab[:, 2 * LANES:]

        def norm_rot(src_ref, g, c):
            t = src_ref[r0:r0 + PREP_ROWS, c * LANES:(c + 1) * LANES].astype(F32)
            tn = t * lax.rsqrt(_group_mean_sq(t, gsum) + EPS) * g
            up = pltpu.roll(tn, LANES - ROT_HALF, axis=1)
            dn = pltpu.roll(tn, ROT_HALF, axis=1)
            return tn * c_keep + up * s_from_hi + dn * s_from_lo

        u, off = divmod(r0, tk)
        for h in range(DIFF_HEADS):
            sl = slice(h * LANES, (h + 1) * LANES)
            q = norm_rot(q_ref, gq, h)
            qt_ref[h, 0, :, r0:r0 + PREP_ROWS] = q.T.astype(qt_ref.dtype)
            ko_ref[r0:r0 + PREP_ROWS, sl] = norm_rot(k_ref, gk, h).astype(ko_ref.dtype)
            vt = v_ref[r0:r0 + PREP_ROWS, sl].astype(F32).T
            vt_ref[h, u, :DIFF_V, off:off + PREP_ROWS] = vt.astype(vt_ref.dtype)
            vt_ref[h, u, DIFF_V:, off:off + PREP_ROWS] = ones_row


def _rotary_tables():
    inv_freq = ROPE_THETA ** (-jnp.arange(ROT_HALF, dtype=F32) / ROT_HALF)
    invf = jnp.concatenate([inv_freq, inv_freq]).reshape(2 * ROT_HALF, 1)
    src = jnp.arange(LANES)[:, None]
    d = jnp.arange(LANES)[None, :] % DIFF_D
    cos_part = jnp.where((d < ROT) & (src == d % ROT_HALF), 1.0, 0.0)
    hi_part = jnp.where((d < ROT_HALF) & (src == ROT_HALF + d), -1.0, 0.0)
    lo_part = jnp.where((d >= ROT_HALF) & (d < ROT) & (src == d), 1.0, 0.0)
    expand = jnp.concatenate([cos_part, hi_part, lo_part], axis=1).astype(BF16)
    keep = jnp.where(d < ROT, 0.0, 1.0).astype(F32)
    return invf, expand, keep


def qk_prep(proj, pos, g_q, g_k, *, q_col, k_col, v_col, tm, tk):
    T = proj.shape[0]
    assert T % tm == 0 and tm % tk == 0 and tk % PREP_ROWS == 0
    assert q_col % DIFF_QK == 0 and k_col % DIFF_QK == 0 and v_col % DIFF_VW == 0 and DIFF_V == LANES
    gq = (jnp.tile(g_q, LANES // DIFF_D) * Q_SCALE).reshape(1, LANES)
    gk = jnp.tile(g_k, LANES // DIFF_D).reshape(1, LANES)
    invf, expand, keep = _rotary_tables()
    return pl.pallas_call(
        functools.partial(_qk_prep_kernel, tk=tk),
        grid=(T // tm,),
        in_specs=[
            pl.BlockSpec((None, 1, tm), lambda i: (i, 0, 0)),
            pl.BlockSpec((2 * ROT_HALF, 1), lambda i: (0, 0)),
            pl.BlockSpec((LANES, 3 * LANES), lambda i: (0, 0)),
            pl.BlockSpec((1, LANES), lambda i: (0, 0)),
            pl.BlockSpec((1, LANES), lambda i: (0, 0)),
            pl.BlockSpec((1, LANES), lambda i: (0, 0)),
            pl.BlockSpec((tm, DIFF_QK), lambda i: (i, q_col // DIFF_QK)),
            pl.BlockSpec((tm, DIFF_QK), lambda i: (i, k_col // DIFF_QK)),
            pl.BlockSpec((tm, DIFF_VW), lambda i: (i, v_col // DIFF_VW)),
        ],
        out_specs=[
            pl.BlockSpec((DIFF_HEADS, 1, LANES, tm), lambda i: (0, i, 0, 0)),
            pl.BlockSpec((tm, DIFF_QK), lambda i: (i, 0)),
            pl.BlockSpec((DIFF_HEADS, tm // tk, V_ROWS, tk), lambda i: (0, i, 0, 0)),
        ],
        out_shape=[
            jax.ShapeDtypeStruct((DIFF_HEADS, T // tm, LANES, tm), BF16),
            jax.ShapeDtypeStruct((T, DIFF_QK), BF16),
            jax.ShapeDtypeStruct((DIFF_HEADS, T // tk, V_ROWS, tk), BF16),
        ],
        compiler_params=_compiler_params(("parallel",)),
        name="qk_prep",
    )(pos, invf, expand, keep, gq, gk, proj, proj, proj)


def _diff_attn_kernel(lam_ref, g_ref, qt_ref, k_ref, vt_ref, o_ref, m_ref, acc_ref,
                      s0_ref, s1_ref, p0_ref, p1_ref, a0_ref, a1_ref, bm0_ref, bm1_ref,
                      *, tq, tk, lam_init):
    assert tq == 2 * tk
    s_bufs, p_bufs, a_bufs, bm_bufs = (s0_ref, s1_ref), (p0_ref, p1_ref), (a0_ref, a1_ref), (bm0_ref, bm1_ref)
    n_tiles = qt_ref.shape[1]

    lv = lam_ref[...]
    e1 = jnp.exp(jnp.sum(lv[0:1, :] * lv[1:2, :], axis=-1, keepdims=True))
    e2 = jnp.exp(jnp.sum(lv[2:3, :] * lv[3:4, :], axis=-1, keepdims=True))
    lam = e1 - e2 + lam_init

    acc_ref[...] = jnp.zeros(acc_ref.shape, F32)

    def scores(i, u, slot, masked, q0=0):
        k = k_ref[pl.ds(pl.multiple_of(u * tk, tk), tk), :]
        qt = qt_ref[0, i, :, q0:]
        nq_cols = tq - q0
        zeros = jnp.zeros((DIFF_D, nq_cols), qt.dtype)
        q_comp = (jnp.concatenate([qt[:DIFF_D], zeros], axis=0),
                  jnp.concatenate([zeros, qt[DIFF_D:]], axis=0))
        if masked:
            kpos = u * tk + lax.broadcasted_iota(jnp.int32, (tk, nq_cols), 0)
            qpos = i * tq + q0 + lax.broadcasted_iota(jnp.int32, (tk, nq_cols), 1)
            causal = kpos <= qpos
        for c in range(2):
            s = jnp.dot(k, q_comp[c], preferred_element_type=F32)
            if masked:
                s = jnp.where(causal, s, -jnp.inf)
            s_bufs[slot][c, :, q0:] = s
            bm_bufs[slot][c, :, q0:] = jnp.max(s, axis=0, keepdims=True)

    def softmax(slot, first=False, q0=0):
        for c in range(2):
            if first:
                m_new = bm_bufs[slot][c]
                a_bufs[slot][c] = jnp.zeros_like(m_new)
            else:
                m_prev = m_ref[c, :, q0:]
                m_new = jnp.maximum(m_prev, bm_bufs[slot][c, :, q0:])
                a_bufs[slot][c, :, q0:] = jnp.exp2(m_prev - m_new)
            m_ref[c, :, q0:] = m_new
            p_bufs[slot][c, :, q0:] = jnp.exp2(s_bufs[slot][c, :, q0:] - m_new).astype(p_bufs[slot].dtype)

    def accumulate(u, slot, q0=0):
        vt = vt_ref[0, u]
        for c in range(2):
            pv = jnp.dot(vt, p_bufs[slot][c, :, q0:], preferred_element_type=F32)
            acc_ref[c, :, q0:] = a_bufs[slot][c, :, q0:] * acc_ref[c, :, q0:] + pv

    def finalize(i):
        o = (acc_ref[0, :DIFF_V] / acc_ref[0, DIFF_V:DIFF_V + 1]
             - lam * (acc_ref[1, :DIFF_V] / acc_ref[1, DIFF_V:DIFF_V + 1]))
        o = o * lax.rsqrt(jnp.mean(o * o, axis=0, keepdims=True) + EPS)
        rows = pl.ds(pl.multiple_of(i * tq, tq), tq)
        o_ref[rows, :] = (o.T * g_ref[...] * (1.0 - lam_init)).astype(o_ref.dtype)

    def step_pair(i, u, masked):
        scores(i, u, 0, masked)
        accumulate(u - 2, 0)
        softmax(1)
        scores(i, u + 1, 1, masked, q0=tk if masked else 0)
        accumulate(u - 1, 1)
        softmax(0)

    scores(0, 0, 0, masked=True)
    scores(0, 1, 1, masked=True, q0=tk)
    softmax(0, first=True)

    def next_tile(i):
        last = 2 * (i - 1)
        scores(i, 0, 0, masked=False)
        accumulate(last, 0)
        softmax(1, q0=tk)
        scores(i, 1, 1, masked=False)
        accumulate(last + 1, 1, q0=tk)
        finalize(i - 1)
        softmax(0, first=True)

    def full_pairs(i):
        def full_pair(r, c):
            step_pair(i, 2 * r, masked=False)
            return c

        lax.fori_loop(1, i, full_pair, 0)

    if n_tiles > 1:
        next_tile(1)

        def q_tile(i, carry):
            full_pairs(i)
            step_pair(i, 2 * i, masked=True)
            next_tile(i + 1)
            return carry

        lax.fori_loop(1, n_tiles - 1, q_tile, 0)
        full_pairs(n_tiles - 1)
        step_pair(n_tiles - 1, 2 * (n_tiles - 1), masked=True)

    last = 2 * (n_tiles - 1)
    accumulate(last, 0)
    softmax(1, q0=tk)
    accumulate(last + 1, 1, q0=tk)
    finalize(n_tiles - 1)


def diff_attn(qt, k, vt, lambda_vecs, g_subln, *, batch, tq, tk, lam_init):
    T = k.shape[0]
    S = T // batch
    assert S % tq == 0 and tq == 2 * tk
    nq, nk = S // tq, S // tk
    stat = pltpu.VMEM((2, 1, tq), F32)
    return pl.pallas_call(
        functools.partial(_diff_attn_kernel, tq=tq, tk=tk, lam_init=lam_init),
        grid=(batch, DIFF_HEADS),
        in_specs=[
            pl.BlockSpec((4, DIFF_D), lambda b, h: (0, 0)),
            pl.BlockSpec((1, DIFF_V), lambda b, h: (0, 0)),
            pl.BlockSpec((1, nq, LANES, tq), lambda b, h: (h, b, 0, 0)),
            pl.BlockSpec((S, LANES), lambda b, h: (b, h)),
            pl.BlockSpec((1, nk, V_ROWS, tk), lambda b, h: (h, b, 0, 0)),
        ],
        out_specs=pl.BlockSpec((S, DIFF_V), lambda b, h: (b, h)),
        out_shape=jax.ShapeDtypeStruct((T, DIFF_VW), BF16),
        scratch_shapes=[
            stat,
            pltpu.VMEM((2, V_ROWS, tq), F32),
            pltpu.VMEM((2, tk, tq), F32),
            pltpu.VMEM((2, tk, tq), F32),
            pltpu.VMEM((2, tk, tq), BF16),
            pltpu.VMEM((2, tk, tq), BF16),
            stat, stat,
            stat, stat,
        ],
        compiler_params=_compiler_params(("parallel", "parallel")),
        name="diff_attn",
    )(lambda_vecs, g_subln.reshape(1, DIFF_V), qt, k, vt)


def _mem_prep_kernel(kv_ref, g_ref, mk_ref, mv_ref):
    g = g_ref[...]
    for h in range(X_HEADS):
        sl = slice(h * X_HEAD_DIM, (h + 1) * X_HEAD_DIM)
        kh = _rms_rows(kv_ref[:, sl], g) * (X_HEAD_DIM ** -0.5)
        mk_ref[:, sl] = kh.astype(mk_ref.dtype)
    mv_ref[...] = kv_ref[:, X_WIDTH:].astype(mv_ref.dtype)


def mem_prep(kv, g_k_x):
    R = kv.shape[0]
    out = jax.ShapeDtypeStruct((R, X_WIDTH), BF16)
    return pl.pallas_call(
        _mem_prep_kernel,
        grid=(1,),
        in_specs=[
            pl.BlockSpec((R, 2 * X_WIDTH), lambda i: (0, 0)),
            pl.BlockSpec((1, X_HEAD_DIM), lambda i: (0, 0)),
        ],
        out_specs=[pl.BlockSpec((R, X_WIDTH), lambda i: (0, 0))] * 2,
        out_shape=[out, out],
        compiler_params=_compiler_params(("arbitrary",)),
        name="mem_prep",
    )(kv, g_k_x.reshape(1, X_HEAD_DIM))


MERGE_CHUNK = 512
GATE_BLOCK = 1024
assert GATE_BLOCK % MERGE_CHUNK == 0


def _merge_kernel(*refs, tiles_per_seq, n_gate_blocks):
    gate_refs = refs[:n_gate_blocks]
    (cb_ref, cc_ref, cx_ref, cch_ref, cxh_ref, xq_ref, ob_ref, x_ref,
     convw_ref, gqx_ref, gn_ref, mk_ref, mv_ref, wa_ref, wb_ref, wc_ref, wo_ref,
     o_ref, hn_ref, ua_ref, oc_ref, mg_ref, yb_ref) = refs[n_gate_blocks:]
    tm = x_ref.shape[0]
    d_model = x_ref.shape[1]
    first = pl.program_id(0) % tiles_per_seq == 0

    def gate(branch, c):
        blk, off = divmod(branch * d_model + c * MERGE_CHUNK, GATE_BLOCK)
        return jax.nn.sigmoid(gate_refs[blk][:, off:off + MERGE_CHUNK].astype(F32))

    ob = ob_ref[...]
    n_chunks = d_model // MERGE_CHUNK

    def diff_branch(c):
        sl = slice(c * MERGE_CHUNK, (c + 1) * MERGE_CHUNK)
        yb_ref[:, sl] = gate(1, c) * jnp.dot(ob, wb_ref[:, sl], preferred_element_type=F32)

    diff_branch(0)

    gq = gqx_ref[...]
    scores = []
    for h in range(X_HEADS):
        sl = slice(h * X_HEAD_DIM, (h + 1) * X_HEAD_DIM)
        qn = _rms_rows(xq_ref[:, sl].astype(F32), gq).astype(BF16)
        scores.append(lax.dot_general(qn, mk_ref[:, sl], (((1,), (1,)), ((), ())),
                                      preferred_element_type=F32))
    diff_branch(1)
    for h, s in enumerate(scores):
        sl = slice(h * X_HEAD_DIM, (h + 1) * X_HEAD_DIM)
        s = s - jnp.max(s, axis=-1, keepdims=True)
        p = jnp.exp(s)
        p = p / jnp.sum(p, axis=-1, keepdims=True)
        oc_ref[:, sl] = jnp.dot(p.astype(BF16), mv_ref[:, sl], preferred_element_type=F32).astype(oc_ref.dtype)
    for c in range(2, n_chunks):
        diff_branch(c)

    z = cc_ref[...].astype(F32) * cx_ref[...].astype(F32)
    zp = cch_ref[...].astype(F32) * cxh_ref[...].astype(F32)
    zp = jnp.where(first, 0.0, zp)
    w = convw_ref[...]
    row8 = lax.broadcasted_iota(jnp.int32, (SUBLANES, 1), 0)
    y = z * w[CONV_K - 1:CONV_K, :]
    for back in range(1, CONV_K):
        zs = pltpu.roll(z, back, axis=0)
        ps = pltpu.roll(zp, back, axis=0)
        head = jnp.where(row8 < back, ps, zs[:SUBLANES])
        zs = jnp.concatenate([head, zs[SUBLANES:]], axis=0)
        y = y + zs * w[CONV_K - 1 - back:CONV_K - back, :]
    ua_ref[...] = (cb_ref[...].astype(F32) * y).astype(ua_ref.dtype)

    ua = ua_ref[...]
    oc = oc_ref[...]
    for c in range(n_chunks):
        sl = slice(c * MERGE_CHUNK, (c + 1) * MERGE_CHUNK)
        merged = yb_ref[:, sl]
        for br, act, w_ref in ((0, ua, wa_ref), (2, oc, wc_ref)):
            merged = merged + gate(br, c) * jnp.dot(act, w_ref[:, sl], preferred_element_type=F32)
        mg_ref[:, sl] = merged.astype(mg_ref.dtype)

    y = x_ref[...] + jnp.dot(mg_ref[...], wo_ref[...], preferred_element_type=F32)
    o_ref[...] = y
    hn_ref[...] = _rms_rows(y, gn_ref[...]).astype(hn_ref.dtype)


def merge(proj, ob, x, conv_w, g_q_x, g_next, mk, mv, w_conv_out, w_diff_out, w_x_out, w_o,
          *, layer, batch, cols, tm):
    T, D = x.shape
    S = T // batch
    M = mk.shape[0] // batch
    assert S % tm == 0 and tm % SUBLANES == 0
    tiles_per_seq = S // tm
    halo_blocks = tm // SUBLANES

    def col_spec(name, width):
        assert cols[name] % width == 0
        return pl.BlockSpec((tm, width), lambda i: (i, cols[name] // width))

    def halo_spec(name):
        return pl.BlockSpec((SUBLANES, CONV_WIDTH),
                            lambda i: (jnp.maximum(i * halo_blocks - 1, 0), cols[name] // CONV_WIDTH))

    def resident(shape):
        return pl.BlockSpec((None,) + shape, lambda i: (layer, 0, 0), pipeline_mode=pl.Buffered(1))

    assert cols["gates"] % GATE_BLOCK == 0 and (N_BRANCH * D) % GATE_BLOCK == 0 and D % MERGE_CHUNK == 0
    n_gate_blocks = N_BRANCH * D // GATE_BLOCK
    gate_specs = [pl.BlockSpec((tm, GATE_BLOCK), lambda i, g=g: (i, cols["gates"] // GATE_BLOCK + g))
                  for g in range(n_gate_blocks)]

    return pl.pallas_call(
        functools.partial(_merge_kernel, tiles_per_seq=tiles_per_seq, n_gate_blocks=n_gate_blocks),
        grid=(T // tm,),
        in_specs=gate_specs + [
            col_spec("cb", CONV_WIDTH),
            col_spec("cc", CONV_WIDTH),
            col_spec("cx", CONV_WIDTH),
            halo_spec("cc"),
            halo_spec("cx"),
            col_spec("xq", X_WIDTH),
            pl.BlockSpec((tm, DIFF_VW), lambda i: (i, 0)),
            pl.BlockSpec((tm, D), lambda i: (i, 0)),
            pl.BlockSpec((CONV_K, CONV_WIDTH), lambda i: (0, 0)),
            pl.BlockSpec((1, X_HEAD_DIM), lambda i: (0, 0)),
            pl.BlockSpec((1, D), lambda i: (0, 0)),
            pl.BlockSpec((M, X_WIDTH), lambda i: (i // tiles_per_seq, 0)),
            pl.BlockSpec((M, X_WIDTH), lambda i: (i // tiles_per_seq, 0)),
            resident((CONV_WIDTH, D)),
            resident((DIFF_VW, D)),
            resident((X_WIDTH, D)),
            resident((D, D)),
        ],
        out_specs=[pl.BlockSpec((tm, D), lambda i: (i, 0))] * 2,
        out_shape=[jax.ShapeDtypeStruct((T, D), F32), jax.ShapeDtypeStruct((T, D), BF16)],
        scratch_shapes=[
            pltpu.VMEM((tm, CONV_WIDTH), BF16),
            pltpu.VMEM((tm, X_WIDTH), BF16),
            pltpu.VMEM((tm, D), BF16),
            pltpu.VMEM((tm, D), F32),
        ],
        compiler_params=_compiler_params(("parallel",)),
        name="merge",
    )(*([proj] * (n_gate_blocks + 6)), ob, x, conv_w, g_q_x.reshape(1, X_HEAD_DIM), g_next.reshape(1, D),
      mk, mv, w_conv_out, w_diff_out, w_x_out, w_o)


def _ffn_up_kernel(h_ref, wg_ref, wu_ref, o_ref, wgb_ref, wub_ref):
    @pl.when(pl.program_id(1) == 0)
    def _():
        wgb_ref[...] = wg_ref[...].astype(wgb_ref.dtype)
        wub_ref[...] = wu_ref[...].astype(wub_ref.dtype)

    h = h_ref[...]
    a = jnp.dot(h, wgb_ref[...], preferred_element_type=F32)
    b = jnp.dot(h, wub_ref[...], preferred_element_type=F32)
    o_ref[...] = (a * jax.nn.sigmoid(a) * b).astype(o_ref.dtype)


def ffn_up(h, w_gate_up, *, layer, tm, tf):
    T, D = h.shape
    d_ff = w_gate_up.shape[2] // 2
    assert T % tm == 0 and d_ff % tf == 0
    nf = d_ff // tf
    return pl.pallas_call(
        _ffn_up_kernel,
        grid=(nf, T // tm),
        in_specs=[
            pl.BlockSpec((tm, D), lambda f, i: (i, 0)),
            pl.BlockSpec((None, D, tf), lambda f, i: (layer, 0, f)),
            pl.BlockSpec((None, D, tf), lambda f, i: (layer, 0, nf + f)),
        ],
        out_specs=pl.BlockSpec((tm, tf), lambda f, i: (i, f)),
        out_shape=jax.ShapeDtypeStruct((T, d_ff), BF16),
        scratch_shapes=[pltpu.VMEM((D, tf), BF16), pltpu.VMEM((D, tf), BF16)],
        compiler_params=_compiler_params(("parallel", "arbitrary")),
        name="ffn_up",
    )(h, w_gate_up, w_gate_up)


def _ffn_down_kernel(a_ref, x_ref, wd_ref, *rest, normed_out):
    if normed_out:
        gn_ref, o_ref, hn_ref = rest
    else:
        (o_ref,) = rest
    y = x_ref[...] + jnp.dot(a_ref[...], wd_ref[...], preferred_element_type=F32)
    o_ref[...] = y
    if normed_out:
        hn_ref[...] = _rms_rows(y, gn_ref[...]).astype(hn_ref.dtype)


def ffn_down(act, x, w_down, *, layer, tm, g_next=None):
    T, D = x.shape
    d_ff = act.shape[1]
    assert T % tm == 0
    normed_out = g_next is not None
    row_spec = pl.BlockSpec((tm, D), lambda i: (i, 0))
    in_specs = [
        pl.BlockSpec((tm, d_ff), lambda i: (i, 0)),
        row_spec,
        pl.BlockSpec((None, d_ff, D), lambda i: (layer, 0, 0), pipeline_mode=pl.Buffered(1)),
    ]
    args = [act, x, w_down]
    out_specs, out_shape = row_spec, jax.ShapeDtypeStruct((T, D), F32)
    if normed_out:
        in_specs.append(pl.BlockSpec((1, D), lambda i: (0, 0)))
        args.append(g_next.reshape(1, D))
        out_specs = [row_spec, row_spec]
        out_shape = [out_shape, jax.ShapeDtypeStruct((T, D), BF16)]
    return pl.pallas_call(
        functools.partial(_ffn_down_kernel, normed_out=normed_out),
        grid=(T // tm,),
        in_specs=in_specs,
        out_specs=out_specs,
        out_shape=out_shape,
        compiler_params=_compiler_params(("parallel",)),
        name="ffn_down",
    )(*args)


def _tile(n, want):
    t = min(n, want)
    while n % t:
        t -= SUBLANES
    return t


def kernel(x, mem, positions, g_mix, w_in, conv_w, w_conv_out, g_q_diff, g_k_diff, lambda_vecs,
           g_subln, w_diff_out, g_mem, w_mem_kv, g_q_x, g_k_x, w_x_out, w_o, g_ffn, w_gate_up, w_down):
    B, S, D = x.shape
    M = mem.shape[1]
    T = B * S
    depth = w_in.shape[0]
    d_ff = w_down.shape[1]

    n_conv = 3 * CONV_WIDTH
    n_diff = 2 * DIFF_QK + DIFF_VW
    cols = {
        "cb": 0,
        "cc": CONV_WIDTH,
        "cx": 2 * CONV_WIDTH,
        "dq": n_conv,
        "dk": n_conv + DIFF_QK,
        "dv": n_conv + 2 * DIFF_QK,
        "xq": n_conv + n_diff,
        "gates": n_conv + n_diff + X_WIDTH,
    }

    xf = x.reshape(T, D)
    memf = mem.reshape(B * M, D)

    tm_proj = _tile(T, 2048)
    tq = _tile(S, 1024)
    tk = tq // 2
    pos = positions.reshape(T // tq, 1, tq)
    tm_merge = _tile(S, 256)
    tm_up = _tile(T, 2048)
    tf = _tile(d_ff, 512)
    tm_down = _tile(T, 256)

    w_mem_kv, w_conv_out, w_diff_out, w_x_out, w_o, w_down = (
        w.astype(BF16) for w in (w_mem_kv, w_conv_out, w_diff_out, w_x_out, w_o, w_down))

    h = rms_norm_rows(xf, g_mix[0], tm=_tile(T, 1024))
    for l in range(depth):
        lam_init = 0.8 - 0.6 * math.exp(-0.3 * l)

        proj = in_proj(h, w_in, layer=l, tm=tm_proj, tn=1024)
        qt, k, vt = qk_prep(proj, pos, g_q_diff[l], g_k_diff[l],
                            q_col=cols["dq"], k_col=cols["dk"], v_col=cols["dv"], tm=tq, tk=tk)
        ob = diff_attn(qt, k, vt, lambda_vecs[l], g_subln[l], batch=B, tq=tq, tk=tk, lam_init=lam_init)

        kv = norm_matmul(memf, g_mem[l], w_mem_kv, layer=l, tm=B * M, tn=1024, out_dtype=F32)
        mk, mv = mem_prep(kv, g_k_x[l])

        xf, h2 = merge(proj, ob, xf, conv_w[l], g_q_x[l], g_ffn[l], mk, mv,
                       w_conv_out, w_diff_out, w_x_out, w_o, layer=l, batch=B, cols=cols, tm=tm_merge)
        act = ffn_up(h2, w_gate_up, layer=l, tm=tm_up, tf=tf)
        if l + 1 < depth:
            xf, h = ffn_down(act, xf, w_down, layer=l, tm=tm_down, g_next=g_mix[l + 1])
        else:
            xf = ffn_down(act, xf, w_down, layer=l, tm=tm_down)

    return xf.reshape(B, S, D)
```

```python
import functools
import math

import jax
import jax.numpy as jnp
from jax import lax
from jax.experimental import pallas as pl
from jax.experimental.pallas import tpu as pltpu

EPS = 1e-6
CONV_WIDTH = 1024
CONV_K = 3
DIFF_HEADS = 8
DIFF_D = 64
DIFF_V = 2 * DIFF_D
DIFF_QK = DIFF_HEADS * 2 * DIFF_D
DIFF_VW = DIFF_HEADS * DIFF_V
X_HEADS = 4
X_HEAD_DIM = 256
X_WIDTH = X_HEADS * X_HEAD_DIM
N_BRANCH = 3
ROPE_THETA = 500000.0
ROT_FRAC = 4
ROT = DIFF_D // ROT_FRAC
ROT_HALF = ROT // 2

LANES = 128
SUBLANES = 8
BF16_ROWS = 16
V_ROWS = DIFF_V + BF16_ROWS
Q_SCALE = DIFF_D ** -0.5 * math.log2(math.e)
VMEM_LIMIT_BYTES = 56 * 1024 * 1024

F32 = jnp.float32
BF16 = jnp.bfloat16


def _compiler_params(semantics):
    return pltpu.CompilerParams(dimension_semantics=semantics, vmem_limit_bytes=VMEM_LIMIT_BYTES)


def _rms_rows(t, g):
    ms = jnp.mean(t * t, axis=-1, keepdims=True)
    return t * lax.rsqrt(ms + EPS) * g


def _norm_matmul_kernel(x_ref, g_ref, w_ref, o_ref, h_ref):
    @pl.when(pl.program_id(1) == 0)
    def _():
        h_ref[...] = _rms_rows(x_ref[...], g_ref[...]).astype(h_ref.dtype)

    o_ref[...] = jnp.dot(h_ref[...], w_ref[...], preferred_element_type=F32).astype(o_ref.dtype)


def norm_matmul(x, g, w, *, layer, tm, tn, out_dtype):
    T, D = x.shape
    N = w.shape[2]
    assert T % tm == 0 and N % tn == 0
    return pl.pallas_call(
        _norm_matmul_kernel,
        grid=(T // tm, N // tn),
        in_specs=[
            pl.BlockSpec((tm, D), lambda i, j: (i, 0)),
            pl.BlockSpec((1, D), lambda i, j: (0, 0)),
            pl.BlockSpec((None, D, tn), lambda i, j: (layer, 0, j)),
        ],
        out_specs=pl.BlockSpec((tm, tn), lambda i, j: (i, j)),
        out_shape=jax.ShapeDtypeStruct((T, N), out_dtype),
        scratch_shapes=[pltpu.VMEM((tm, D), BF16)],
        compiler_params=_compiler_params(("parallel", "arbitrary")),
        name="norm_matmul",
    )(x, g.reshape(1, D), w)


def _rms_norm_kernel(x_ref, g_ref, o_ref):
    o_ref[...] = _rms_rows(x_ref[...], g_ref[...]).astype(o_ref.dtype)


def rms_norm_rows(x, g, *, tm):
    T, D = x.shape
    assert T % tm == 0
    return pl.pallas_call(
        _rms_norm_kernel,
        grid=(T // tm,),
        in_specs=[pl.BlockSpec((tm, D), lambda i: (i, 0)), pl.BlockSpec((1, D), lambda i: (0, 0))],
        out_specs=pl.BlockSpec((tm, D), lambda i: (i, 0)),
        out_shape=jax.ShapeDtypeStruct((T, D), BF16),
        compiler_params=_compiler_params(("parallel",)),
        name="rms_norm",
    )(x, g.reshape(1, D))


def _in_proj_kernel(h_ref, w_ref, o_ref, wb_ref):
    @pl.when(pl.program_id(1) == 0)
    def _():
        wb_ref[...] = w_ref[...].astype(wb_ref.dtype)

    o_ref[...] = jnp.dot(h_ref[...], wb_ref[...], preferred_element_type=F32).astype(o_ref.dtype)


def in_proj(h, w, *, layer, tm, tn):
    T, D = h.shape
    N = w.shape[2]
    assert T % tm == 0 and N % tn == 0
    return pl.pallas_call(
        _in_proj_kernel,
        grid=(N // tn, T // tm),
        in_specs=[
            pl.BlockSpec((tm, D), lambda j, i: (i, 0)),
            pl.BlockSpec((None, D, tn), lambda j, i: (layer, 0, j)),
        ],
        out_specs=pl.BlockSpec((tm, tn), lambda j, i: (i, j)),
        out_shape=jax.ShapeDtypeStruct((T, N), BF16),
        scratch_shapes=[pltpu.VMEM((D, tn), BF16)],
        compiler_params=_compiler_params(("parallel", "arbitrary")),
        name="in_proj",
    )(h, w)


def _group_mean_sq(t, gsum):
    sq = t * t
    hi = sq.astype(BF16)
    lo = (sq - hi.astype(F32)).astype(BF16)
    return jnp.dot(hi, gsum, preferred_element_type=F32) + jnp.dot(lo, gsum, preferred_element_type=F32)


PREP_ROWS = LANES


def _qk_prep_kernel(pos_ref, invf_ref, expand_ref, keep_ref, gq_ref, gk_ref, q_ref, k_ref, v_ref,
                    qt_ref, ko_ref, vt_ref, *, tk):
    tm = q_ref.shape[0]
    row = lax.broadcasted_iota(jnp.int32, (LANES, LANES), 0)
    col = lax.broadcasted_iota(jnp.int32, (LANES, LANES), 1)
    gsum = jnp.where((row < DIFF_D) == (col < DIFF_D), 1.0 / DIFF_D, 0.0).astype(BF16)
    trig_row = lax.broadcasted_iota(jnp.int32, (2 * ROT_HALF, 1), 0)
    trig_pad = jnp.zeros((LANES - 2 * ROT_HALF, PREP_ROWS), F32)
    pad_row = lax.broadcasted_iota(jnp.int32, (V_ROWS - DIFF_V, PREP_ROWS), 0)
    ones_row = jnp.where(pad_row == 0, 1.0, 0.0).astype(vt_ref.dtype)
    gq = gq_ref[...]
    gk = gk_ref[...]
    expand = expand_ref[...]

    for r0 in range(0, tm, PREP_ROWS):
        ang = invf_ref[...] * pos_ref[:, r0:r0 + PREP_ROWS].astype(F32)
        trig_t = jnp.where(trig_row < ROT_HALF, jnp.cos(ang), jnp.sin(ang))
        cos_t, sin_t = trig_t[:ROT_HALF], trig_t[ROT_HALF:]
        trig = jnp.concatenate([trig_t, trig_pad], axis=0).T
        hi = trig.astype(BF16)
        lo = (trig - hi.astype(F32)).astype(BF16)
        tab = (jnp.dot(hi, expand, preferred_element_type=F32)
               + jnp.dot(lo, expand, preferred_element_type=F32))
        c_keep = tab[:, :LANES] + keep_ref[...]
        s_from_hi = tab[:, LANES:2 * LANES]
        s_from_lo = tab[:, 2 * LANES:]

        def norm(src_ref, g, c):
            t = src_ref[r0:r0 + PREP_ROWS, c * LANES:(c + 1) * LANES].astype(F32)
            return t * lax.rsqrt(_group_mean_sq(t, gsum) + EPS) * g

        def rot_lanes(tn):
            up = pltpu.roll(tn, LANES - ROT_HALF, axis=1)
            dn = pltpu.roll(tn, ROT_HALF, axis=1)
            return tn * c_keep + up * s_from_hi + dn * s_from_lo

        def rot_rows(tt):
            parts = []
            for base in range(0, LANES, DIFF_D):
                x1 = tt[base:base + ROT_HALF]
                x2 = tt[base + ROT_HALF:base + ROT]
                parts += [x1 * cos_t - x2 * sin_t, x2 * cos_t + x1 * sin_t, tt[base + ROT:base + DIFF_D]]
            return jnp.concatenate(parts, axis=0)

        u, off = divmod(r0, tk)
        for h in range(DIFF_HEADS):
            sl = slice(h * LANES, (h + 1) * LANES)
            qt = rot_rows(norm(q_ref, gq, h).T)
            qt_ref[h, 0, :, r0:r0 + PREP_ROWS] = qt.astype(qt_ref.dtype)
            ko_ref[r0:r0 + PREP_ROWS, sl] = rot_lanes(norm(k_ref, gk, h)).astype(ko_ref.dtype)
            vt_ref[h, u, :DIFF_V, off:off + PREP_ROWS] = v_ref[r0:r0 + PREP_ROWS, sl].T
            vt_ref[h, u, DIFF_V:, off:off + PREP_ROWS] = ones_row


def _rotary_tables():
    inv_freq = ROPE_THETA ** (-jnp.arange(ROT_HALF, dtype=F32) / ROT_HALF)
    invf = jnp.concatenate([inv_freq, inv_freq]).reshape(2 * ROT_HALF, 1)
    src = jnp.arange(LANES)[:, None]
    d = jnp.arange(LANES)[None, :] % DIFF_D
    cos_part = jnp.where((d < ROT) & (src == d % ROT_HALF), 1.0, 0.0)
    hi_part = jnp.where((d < ROT_HALF) & (src == ROT_HALF + d), -1.0, 0.0)
    lo_part = jnp.where((d >= ROT_HALF) & (d < ROT) & (src == d), 1.0, 0.0)
    expand = jnp.concatenate([cos_part, hi_part, lo_part], axis=1).astype(BF16)
    keep = jnp.where(d < ROT, 0.0, 1.0).astype(F32)
    return invf, expand, keep


def qk_prep(proj, pos, g_q, g_k, *, q_col, k_col, v_col, tm, tk):
    T = proj.shape[0]
    assert T % tm == 0 and tm % tk == 0 and tk % PREP_ROWS == 0
    assert q_col % DIFF_QK == 0 and k_col % DIFF_QK == 0 and v_col % DIFF_VW == 0 and DIFF_V == LANES
    gq = (jnp.tile(g_q, LANES // DIFF_D) * Q_SCALE).reshape(1, LANES)
    gk = jnp.tile(g_k, LANES // DIFF_D).reshape(1, LANES)
    invf, expand, keep = _rotary_tables()
    return pl.pallas_call(
        functools.partial(_qk_prep_kernel, tk=tk),
        grid=(T // tm,),
        in_specs=[
            pl.BlockSpec((None, 1, tm), lambda i: (i, 0, 0)),
            pl.BlockSpec((2 * ROT_HALF, 1), lambda i: (0, 0)),
            pl.BlockSpec((LANES, 3 * LANES), lambda i: (0, 0)),
            pl.BlockSpec((1, LANES), lambda i: (0, 0)),
            pl.BlockSpec((1, LANES), lambda i: (0, 0)),
            pl.BlockSpec((1, LANES), lambda i: (0, 0)),
            pl.BlockSpec((tm, DIFF_QK), lambda i: (i, q_col // DIFF_QK)),
            pl.BlockSpec((tm, DIFF_QK), lambda i: (i, k_col // DIFF_QK)),
            pl.BlockSpec((tm, DIFF_VW), lambda i: (i, v_col // DIFF_VW)),
        ],
        out_specs=[
            pl.BlockSpec((DIFF_HEADS, 1, LANES, tm), lambda i: (0, i, 0, 0)),
            pl.BlockSpec((tm, DIFF_QK), lambda i: (i, 0)),
            pl.BlockSpec((DIFF_HEADS, tm // tk, V_ROWS, tk), lambda i: (0, i, 0, 0)),
        ],
        out_shape=[
            jax.ShapeDtypeStruct((DIFF_HEADS, T // tm, LANES, tm), BF16),
            jax.ShapeDtypeStruct((T, DIFF_QK), BF16),
            jax.ShapeDtypeStruct((DIFF_HEADS, T // tk, V_ROWS, tk), BF16),
        ],
        compiler_params=_compiler_params(("parallel",)),
        name="qk_prep",
    )(pos, invf, expand, keep, gq, gk, proj, proj, proj)


def _diff_attn_kernel(lam_ref, g_ref, qt_ref, k_ref, vt_ref, o_ref, m_ref, acc_ref,
                      s0_ref, s1_ref, p0_ref, p1_ref, a0_ref, a1_ref, bm0_ref, bm1_ref,
                      *, tq, tk, lam_init):
    assert tq == 2 * tk
    s_bufs, p_bufs, a_bufs, bm_bufs = (s0_ref, s1_ref), (p0_ref, p1_ref), (a0_ref, a1_ref), (bm0_ref, bm1_ref)
    n_tiles = qt_ref.shape[1]

    lv = lam_ref[...]
    e1 = jnp.exp(jnp.sum(lv[0:1, :] * lv[1:2, :], axis=-1, keepdims=True))
    e2 = jnp.exp(jnp.sum(lv[2:3, :] * lv[3:4, :], axis=-1, keepdims=True))
    lam = e1 - e2 + lam_init

    acc_ref[...] = jnp.zeros(acc_ref.shape, F32)

    def scores(i, u, slot, masked, q0=0):
        k = k_ref[pl.ds(pl.multiple_of(u * tk, tk), tk), :]
        qt = qt_ref[0, i, :, q0:]
        nq_cols = tq - q0
        zeros = jnp.zeros((DIFF_D, nq_cols), qt.dtype)
        q_comp = (jnp.concatenate([qt[:DIFF_D], zeros], axis=0),
                  jnp.concatenate([zeros, qt[DIFF_D:]], axis=0))
        if masked:
            kpos = u * tk + lax.broadcasted_iota(jnp.int32, (tk, nq_cols), 0)
            qpos = i * tq + q0 + lax.broadcasted_iota(jnp.int32, (tk, nq_cols), 1)
            causal = kpos <= qpos
        for c in range(2):
            s = jnp.dot(k, q_comp[c], preferred_element_type=F32)
            if masked:
                s = jnp.where(causal, s, -jnp.inf)
            s_bufs[slot][c, :, q0:] = s
            bm_bufs[slot][c, :, q0:] = jnp.max(s, axis=0, keepdims=True)

    def softmax(slot, first=False, q0=0):
        for c in range(2):
            if first:
                m_new = bm_bufs[slot][c]
                a_bufs[slot][c] = jnp.zeros_like(m_new)
            else:
                m_prev = m_ref[c, :, q0:]
                m_new = jnp.maximum(m_prev, bm_bufs[slot][c, :, q0:])
                a_bufs[slot][c, :, q0:] = jnp.exp2(m_prev - m_new)
            m_ref[c, :, q0:] = m_new
            p_bufs[slot][c, :, q0:] = jnp.exp2(s_bufs[slot][c, :, q0:] - m_new).astype(p_bufs[slot].dtype)

    def accumulate(u, slot, q0=0):
        vt = vt_ref[0, u]
        for c in range(2):
            pv = jnp.dot(vt, p_bufs[slot][c, :, q0:], preferred_element_type=F32)
            acc_ref[c, :, q0:] = a_bufs[slot][c, :, q0:] * acc_ref[c, :, q0:] + pv

    def finalize(i):
        o = (acc_ref[0, :DIFF_V] / acc_ref[0, DIFF_V:DIFF_V + 1]
             - lam * (acc_ref[1, :DIFF_V] / acc_ref[1, DIFF_V:DIFF_V + 1]))
        o = o * lax.rsqrt(jnp.mean(o * o, axis=0, keepdims=True) + EPS)
        rows = pl.ds(pl.multiple_of(i * tq, tq), tq)
        o_ref[rows, :] = (o.T * g_ref[...] * (1.0 - lam_init)).astype(o_ref.dtype)

    def step_pair(i, u, masked):
        scores(i, u, 0, masked)
        accumulate(u - 2, 0)
        softmax(1)
        scores(i, u + 1, 1, masked, q0=tk if masked else 0)
        accumulate(u - 1, 1)
        softmax(0)

    scores(0, 0, 0, masked=True)
    scores(0, 1, 1, masked=True, q0=tk)
    softmax(0, first=True)

    def next_tile(i):
        last = 2 * (i - 1)
        scores(i, 0, 0, masked=False)
        accumulate(last, 0)
        softmax(1, q0=tk)
        scores(i, 1, 1, masked=False)
        accumulate(last + 1, 1, q0=tk)
        finalize(i - 1)
        softmax(0, first=True)

    def full_pairs(i):
        def full_pair(r, c):
            step_pair(i, 2 * r, masked=False)
            return c

        lax.fori_loop(1, i, full_pair, 0)

    if n_tiles > 1:
        next_tile(1)

        def q_tile(i, carry):
            full_pairs(i)
            step_pair(i, 2 * i, masked=True)
            next_tile(i + 1)
            return carry

        lax.fori_loop(1, n_tiles - 1, q_tile, 0)
        full_pairs(n_tiles - 1)
        step_pair(n_tiles - 1, 2 * (n_tiles - 1), masked=True)

    last = 2 * (n_tiles - 1)
    accumulate(last, 0)
    softmax(1, q0=tk)
    accumulate(last + 1, 1, q0=tk)
    finalize(n_tiles - 1)


def diff_attn(qt, k, vt, lambda_vecs, g_subln, *, batch, tq, tk, lam_init):
    T = k.shape[0]
    S = T // batch
    assert S % tq == 0 and tq == 2 * tk
    nq, nk = S // tq, S // tk
    stat = pltpu.VMEM((2, 1, tq), F32)
    return pl.pallas_call(
        functools.partial(_diff_attn_kernel, tq=tq, tk=tk, lam_init=lam_init),
        grid=(batch, DIFF_HEADS),
        in_specs=[
            pl.BlockSpec((4, DIFF_D), lambda b, h: (0, 0)),
            pl.BlockSpec((1, DIFF_V), lambda b, h: (0, 0)),
            pl.BlockSpec((1, nq, LANES, tq), lambda b, h: (h, b, 0, 0)),
            pl.BlockSpec((S, LANES), lambda b, h: (b, h)),
            pl.BlockSpec((1, nk, V_ROWS, tk), lambda b, h: (h, b, 0, 0)),
        ],
        out_specs=pl.BlockSpec((S, DIFF_V), lambda b, h: (b, h)),
        out_shape=jax.ShapeDtypeStruct((T, DIFF_VW), BF16),
        scratch_shapes=[
            stat,
            pltpu.VMEM((2, V_ROWS, tq), F32),
            pltpu.VMEM((2, tk, tq), F32),
            pltpu.VMEM((2, tk, tq), F32),
            pltpu.VMEM((2, tk, tq), BF16),
            pltpu.VMEM((2, tk, tq), BF16),
            stat, stat,
            stat, stat,
        ],
        compiler_params=_compiler_params(("parallel", "parallel")),
        name="diff_attn",
    )(lambda_vecs, g_subln.reshape(1, DIFF_V), qt, k, vt)


def _mem_prep_kernel(kv_ref, g_ref, mk_ref, mv_ref):
    g = g_ref[...]
    for h in range(X_HEADS):
        sl = slice(h * X_HEAD_DIM, (h + 1) * X_HEAD_DIM)
        kh = _rms_rows(kv_ref[:, sl], g) * (X_HEAD_DIM ** -0.5)
        mk_ref[:, sl] = kh.astype(mk_ref.dtype)
    mv_ref[...] = kv_ref[:, X_WIDTH:].astype(mv_ref.dtype)


def mem_prep(kv, g_k_x):
    R = kv.shape[0]
    out = jax.ShapeDtypeStruct((R, X_WIDTH), BF16)
    return pl.pallas_call(
        _mem_prep_kernel,
        grid=(1,),
        in_specs=[
            pl.BlockSpec((R, 2 * X_WIDTH), lambda i: (0, 0)),
            pl.BlockSpec((1, X_HEAD_DIM), lambda i: (0, 0)),
        ],
        out_specs=[pl.BlockSpec((R, X_WIDTH), lambda i: (0, 0))] * 2,
        out_shape=[out, out],
        compiler_params=_compiler_params(("arbitrary",)),
        name="mem_prep",
    )(kv, g_k_x.reshape(1, X_HEAD_DIM))


MERGE_CHUNK = 512
GATE_BLOCK = 1024
assert GATE_BLOCK % MERGE_CHUNK == 0


def _merge_kernel(*refs, tiles_per_seq, n_gate_blocks):
    gate_refs = refs[:n_gate_blocks]
    (cb_ref, cc_ref, cx_ref, cch_ref, cxh_ref, xq_ref, ob_ref, x_ref,
     convw_ref, gqx_ref, gn_ref, mk_ref, mv_ref, wa_ref, wb_ref, wc_ref, wo_ref,
     o_ref, hn_ref, ua_ref, oc_ref, mg_ref, yb_ref) = refs[n_gate_blocks:]
    tm = x_ref.shape[0]
    d_model = x_ref.shape[1]
    first = pl.program_id(0) % tiles_per_seq == 0

    def gate(branch, c):
        blk, off = divmod(branch * d_model + c * MERGE_CHUNK, GATE_BLOCK)
        return jax.nn.sigmoid(gate_refs[blk][:, off:off + MERGE_CHUNK].astype(F32))

    ob = ob_ref[...]
    n_chunks = d_model // MERGE_CHUNK

    def diff_branch(c):
        sl = slice(c * MERGE_CHUNK, (c + 1) * MERGE_CHUNK)
        yb_ref[:, sl] = gate(1, c) * jnp.dot(ob, wb_ref[:, sl], preferred_element_type=F32)

    diff_branch(0)

    gq = gqx_ref[...]
    scores = []
    for h in range(X_HEADS):
        sl = slice(h * X_HEAD_DIM, (h + 1) * X_HEAD_DIM)
        qn = _rms_rows(xq_ref[:, sl].astype(F32), gq).astype(BF16)
        scores.append(lax.dot_general(qn, mk_ref[:, sl], (((1,), (1,)), ((), ())),
                                      preferred_element_type=F32))
    diff_branch(1)
    for h, s in enumerate(scores):
        sl = slice(h * X_HEAD_DIM, (h + 1) * X_HEAD_DIM)
        s = s - jnp.max(s, axis=-1, keepdims=True)
        p = jnp.exp(s)
        p = p / jnp.sum(p, axis=-1, keepdims=True)
        oc_ref[:, sl] = jnp.dot(p.astype(BF16), mv_ref[:, sl], preferred_element_type=F32).astype(oc_ref.dtype)
    for c in range(2, n_chunks):
        diff_branch(c)

    z = cc_ref[...].astype(F32) * cx_ref[...].astype(F32)
    zp = cch_ref[...].astype(F32) * cxh_ref[...].astype(F32)
    zp = jnp.where(first, 0.0, zp)
    w = convw_ref[...]
    row8 = lax.broadcasted_iota(jnp.int32, (SUBLANES, 1), 0)
    y = z * w[CONV_K - 1:CONV_K, :]
    for back in range(1, CONV_K):
        zs = pltpu.roll(z, back, axis=0)
        ps = pltpu.roll(zp, back, axis=0)
        head = jnp.where(row8 < back, ps, zs[:SUBLANES])
        zs = jnp.concatenate([head, zs[SUBLANES:]], axis=0)
        y = y + zs * w[CONV_K - 1 - back:CONV_K - back, :]
    ua_ref[...] = (cb_ref[...].astype(F32) * y).astype(ua_ref.dtype)

    ua = ua_ref[...]
    oc = oc_ref[...]
    for c in range(n_chunks):
        sl = slice(c * MERGE_CHUNK, (c + 1) * MERGE_CHUNK)
        merged = yb_ref[:, sl]
        for br, act, w_ref in ((0, ua, wa_ref), (2, oc, wc_ref)):
            merged = merged + gate(br, c) * jnp.dot(act, w_ref[:, sl], preferred_element_type=F32)
        mg_ref[:, sl] = merged.astype(mg_ref.dtype)

    y = x_ref[...] + jnp.dot(mg_ref[...], wo_ref[...], preferred_element_type=F32)
    o_ref[...] = y
    hn_ref[...] = _rms_rows(y, gn_ref[...]).astype(hn_ref.dtype)


def merge(proj, ob, x, conv_w, g_q_x, g_next, mk, mv, w_conv_out, w_diff_out, w_x_out, w_o,
          *, layer, batch, cols, tm):
    T, D = x.shape
    S = T // batch
    M = mk.shape[0] // batch
    assert S % tm == 0 and tm % SUBLANES == 0
    tiles_per_seq = S // tm
    halo_blocks = tm // SUBLANES

    def col_spec(name, width):
        assert cols[name] % width == 0
        return pl.BlockSpec((tm, width), lambda i: (i, cols[name] // width))

    def halo_spec(name):
        return pl.BlockSpec((SUBLANES, CONV_WIDTH),
                            lambda i: (jnp.maximum(i * halo_blocks - 1, 0), cols[name] // CONV_WIDTH))

    def resident(shape):
        return pl.BlockSpec((None,) + shape, lambda i: (layer, 0, 0), pipeline_mode=pl.Buffered(1))

    assert cols["gates"] % GATE_BLOCK == 0 and (N_BRANCH * D) % GATE_BLOCK == 0 and D % MERGE_CHUNK == 0
    n_gate_blocks = N_BRANCH * D // GATE_BLOCK
    gate_specs = [pl.BlockSpec((tm, GATE_BLOCK), lambda i, g=g: (i, cols["gates"] // GATE_BLOCK + g))
                  for g in range(n_gate_blocks)]

    return pl.pallas_call(
        functools.partial(_merge_kernel, tiles_per_seq=tiles_per_seq, n_gate_blocks=n_gate_blocks),
        grid=(T // tm,),
        in_specs=gate_specs + [
            col_spec("cb", CONV_WIDTH),
            col_spec("cc", CONV_WIDTH),
            col_spec("cx", CONV_WIDTH),
            halo_spec("cc"),
            halo_spec("cx"),
            col_spec("xq", X_WIDTH),
            pl.BlockSpec((tm, DIFF_VW), lambda i: (i, 0)),
            pl.BlockSpec((tm, D), lambda i: (i, 0)),
            pl.BlockSpec((CONV_K, CONV_WIDTH), lambda i: (0, 0)),
            pl.BlockSpec((1, X_HEAD_DIM), lambda i: (0, 0)),
            pl.BlockSpec((1, D), lambda i: (0, 0)),
            pl.BlockSpec((M, X_WIDTH), lambda i: (i // tiles_per_seq, 0)),
            pl.BlockSpec((M, X_WIDTH), lambda i: (i // tiles_per_seq, 0)),
            resident((CONV_WIDTH, D)),
            resident((DIFF_VW, D)),
            resident((X_WIDTH, D)),
            resident((D, D)),
        ],
        out_specs=[pl.BlockSpec((tm, D), lambda i: (i, 0))] * 2,
        out_shape=[jax.ShapeDtypeStruct((T, D), F32), jax.ShapeDtypeStruct((T, D), BF16)],
        scratch_shapes=[
            pltpu.VMEM((tm, CONV_WIDTH), BF16),
            pltpu.VMEM((tm, X_WIDTH), BF16),
            pltpu.VMEM((tm, D), BF16),
            pltpu.VMEM((tm, D), F32),
        ],
        compiler_params=_compiler_params(("parallel",)),
        name="merge",
    )(*([proj] * (n_gate_blocks + 6)), ob, x, conv_w, g_q_x.reshape(1, X_HEAD_DIM), g_next.reshape(1, D),
      mk, mv, w_conv_out, w_diff_out, w_x_out, w_o)


def _ffn_up_kernel(h_ref, wg_ref, wu_ref, o_ref, wgb_ref, wub_ref):
    @pl.when(pl.program_id(1) == 0)
    def _():
        wgb_ref[...] = wg_ref[...].astype(wgb_ref.dtype)
        wub_ref[...] = wu_ref[...].astype(wub_ref.dtype)

    h = h_ref[...]
    a = jnp.dot(h, wgb_ref[...], preferred_element_type=F32)
    b = jnp.dot(h, wub_ref[...], preferred_element_type=F32)
    o_ref[...] = (a * jax.nn.sigmoid(a) * b).astype(o_ref.dtype)


def ffn_up(h, w_gate_up, *, layer, tm, tf):
    T, D = h.shape
    d_ff = w_gate_up.shape[2] // 2
    assert T % tm == 0 and d_ff % tf == 0
    nf = d_ff // tf
    return pl.pallas_call(
        _ffn_up_kernel,
        grid=(nf, T // tm),
        in_specs=[
            pl.BlockSpec((tm, D), lambda f, i: (i, 0)),
            pl.BlockSpec((None, D, tf), lambda f, i: (layer, 0, f)),
            pl.BlockSpec((None, D, tf), lambda f, i: (layer, 0, nf + f)),
        ],
        out_specs=pl.BlockSpec((tm, tf), lambda f, i: (i, f)),
        out_shape=jax.ShapeDtypeStruct((T, d_ff), BF16),
        scratch_shapes=[pltpu.VMEM((D, tf), BF16), pltpu.VMEM((D, tf), BF16)],
        compiler_params=_compiler_params(("parallel", "arbitrary")),
        name="ffn_up",
    )(h, w_gate_up, w_gate_up)


def _ffn_down_kernel(a_ref, x_ref, wd_ref, *rest, normed_out):
    if normed_out:
        gn_ref, o_ref, hn_ref = rest
    else:
        (o_ref,) = rest
    y = x_ref[...] + jnp.dot(a_ref[...], wd_ref[...], preferred_element_type=F32)
    o_ref[...] = y
    if normed_out:
        hn_ref[...] = _rms_rows(y, gn_ref[...]).astype(hn_ref.dtype)


def ffn_down(act, x, w_down, *, layer, tm, g_next=None):
    T, D = x.shape
    d_ff = act.shape[1]
    assert T % tm == 0
    normed_out = g_next is not None
    row_spec = pl.BlockSpec((tm, D), lambda i: (i, 0))
    in_specs = [
        pl.BlockSpec((tm, d_ff), lambda i: (i, 0)),
        row_spec,
        pl.BlockSpec((None, d_ff, D), lambda i: (layer, 0, 0), pipeline_mode=pl.Buffered(1)),
    ]
    args = [act, x, w_down]
    out_specs, out_shape = row_spec, jax.ShapeDtypeStruct((T, D), F32)
    if normed_out:
        in_specs.append(pl.BlockSpec((1, D), lambda i: (0, 0)))
        args.append(g_next.reshape(1, D))
        out_specs = [row_spec, row_spec]
        out_shape = [out_shape, jax.ShapeDtypeStruct((T, D), BF16)]
    return pl.pallas_call(
        functools.partial(_ffn_down_kernel, normed_out=normed_out),
        grid=(T // tm,),
        in_specs=in_specs,
        out_specs=out_specs,
        out_shape=out_shape,
        compiler_params=_compiler_params(("parallel",)),
        name="ffn_down",
    )(*args)


def _tile(n, want):
    t = min(n, want)
    while n % t:
        t -= SUBLANES
    return t


def kernel(x, mem, positions, g_mix, w_in, conv_w, w_conv_out, g_q_diff, g_k_diff, lambda_vecs,
           g_subln, w_diff_out, g_mem, w_mem_kv, g_q_x, g_k_x, w_x_out, w_o, g_ffn, w_gate_up, w_down):
    B, S, D = x.shape
    M = mem.shape[1]
    T = B * S
    depth = w_in.shape[0]
    d_ff = w_down.shape[1]

    n_conv = 3 * CONV_WIDTH
    n_diff = 2 * DIFF_QK + DIFF_VW
    cols = {
        "cb": 0,
        "cc": CONV_WIDTH,
        "cx": 2 * CONV_WIDTH,
        "dq": n_conv,
        "dk": n_conv + DIFF_QK,
        "dv": n_conv + 2 * DIFF_QK,
        "xq": n_conv + n_diff,
        "gates": n_conv + n_diff + X_WIDTH,
    }

    xf = x.reshape(T, D)
    memf = mem.reshape(B * M, D)

    tm_proj = _tile(T, 2048)
    tq = _tile(S, 1024)
    tk = tq // 2
    pos = positions.reshape(T // tq, 1, tq)
    tm_merge = _tile(S, 256)
    tm_up = _tile(T, 2048)
    tf = _tile(d_ff, 512)
    tm_down = _tile(T, 256)

    w_mem_kv, w_conv_out, w_diff_out, w_x_out, w_o, w_down = (
        w.astype(BF16) for w in (w_mem_kv, w_conv_out, w_diff_out, w_x_out, w_o, w_down))

    h = rms_norm_rows(xf, g_mix[0], tm=_tile(T, 1024))
    for l in range(depth):
        lam_init = 0.8 - 0.6 * math.exp(-0.3 * l)

        proj = in_proj(h, w_in, layer=l, tm=tm_proj, tn=1024)
        qt, k, vt = qk_prep(proj, pos, g_q_diff[l], g_k_diff[l],
                            q_col=cols["dq"], k_col=cols["dk"], v_col=cols["dv"], tm=tq, tk=tk)
        ob = diff_attn(qt, k, vt, lambda_vecs[l], g_subln[l], batch=B, tq=tq, tk=tk, lam_init=lam_init)

        kv = norm_matmul(memf, g_mem[l], w_mem_kv, layer=l, tm=B * M, tn=1024, out_dtype=F32)
        mk, mv = mem_prep(kv, g_k_x[l])

        xf, h2 = merge(proj, ob, xf, conv_w[l], g_q_x[l], g_ffn[l], mk, mv,
                       w_conv_out, w_diff_out, w_x_out, w_o, layer=l, batch=B, cols=cols, tm=tm_merge)
        act = ffn_up(h2, w_gate_up, layer=l, tm=tm_up, tf=tf)
        if l + 1 < depth:
            xf, h = ffn_down(act, xf, w_down, layer=l, tm=tm_down, g_next=g_mix[l + 1])
        else:
            xf = ffn_down(act, xf, w_down, layer=l, tm=tm_down)

    return xf.reshape(B, S, D)
```

```python
import functools
import math

import jax
import jax.numpy as jnp
from jax import lax
from jax.experimental import pallas as pl
from jax.experimental.pallas import tpu as pltpu

EPS = 1e-6
CONV_WIDTH = 1024
CONV_K = 3
DIFF_HEADS = 8
DIFF_D = 64
DIFF_V = 2 * DIFF_D
DIFF_QK = DIFF_HEADS * 2 * DIFF_D
DIFF_VW = DIFF_HEADS * DIFF_V
X_HEADS = 4
X_HEAD_DIM = 256
X_WIDTH = X_HEADS * X_HEAD_DIM
N_BRANCH = 3
ROPE_THETA = 500000.0
ROT_FRAC = 4
ROT = DIFF_D // ROT_FRAC
ROT_HALF = ROT // 2

LANES = 128
SUBLANES = 8
BF16_ROWS = 16
V_ROWS = DIFF_V + BF16_ROWS
Q_SCALE = DIFF_D ** -0.5 * math.log2(math.e)
VMEM_LIMIT_BYTES = 56 * 1024 * 1024

F32 = jnp.float32
BF16 = jnp.bfloat16


def _compiler_params(semantics):
    return pltpu.CompilerParams(dimension_semantics=semantics, vmem_limit_bytes=VMEM_LIMIT_BYTES)


def _rms_rows(t, g):
    ms = jnp.mean(t * t, axis=-1, keepdims=True)
    return t * lax.rsqrt(ms + EPS) * g


def _norm_matmul_kernel(x_ref, g_ref, w_ref, o_ref, h_ref):
    @pl.when(pl.program_id(1) == 0)
    def _():
        h_ref[...] = _rms_rows(x_ref[...], g_ref[...]).astype(h_ref.dtype)

    o_ref[...] = jnp.dot(h_ref[...], w_ref[...], preferred_element_type=F32).astype(o_ref.dtype)


def norm_matmul(x, g, w, *, layer, tm, tn, out_dtype):
    T, D = x.shape
    N = w.shape[2]
    assert T % tm == 0 and N % tn == 0
    return pl.pallas_call(
        _norm_matmul_kernel,
        grid=(T // tm, N // tn),
        in_specs=[
            pl.BlockSpec((tm, D), lambda i, j: (i, 0)),
            pl.BlockSpec((1, D), lambda i, j: (0, 0)),
            pl.BlockSpec((None, D, tn), lambda i, j: (layer, 0, j)),
        ],
        out_specs=pl.BlockSpec((tm, tn), lambda i, j: (i, j)),
        out_shape=jax.ShapeDtypeStruct((T, N), out_dtype),
        scratch_shapes=[pltpu.VMEM((tm, D), BF16)],
        compiler_params=_compiler_params(("parallel", "arbitrary")),
        name="norm_matmul",
    )(x, g.reshape(1, D), w)


def _rms_norm_kernel(x_ref, g_ref, o_ref):
    o_ref[...] = _rms_rows(x_ref[...], g_ref[...]).astype(o_ref.dtype)


def rms_norm_rows(x, g, *, tm):
    T, D = x.shape
    assert T % tm == 0
    return pl.pallas_call(
        _rms_norm_kernel,
        grid=(T // tm,),
        in_specs=[pl.BlockSpec((tm, D), lambda i: (i, 0)), pl.BlockSpec((1, D), lambda i: (0, 0))],
        out_specs=pl.BlockSpec((tm, D), lambda i: (i, 0)),
        out_shape=jax.ShapeDtypeStruct((T, D), BF16),
        compiler_params=_compiler_params(("parallel",)),
        name="rms_norm",
    )(x, g.reshape(1, D))


def _in_proj_kernel(h_ref, w_ref, o_ref, wb_ref):
    @pl.when(pl.program_id(1) == 0)
    def _():
        wb_ref[...] = w_ref[...].astype(wb_ref.dtype)

    o_ref[...] = jnp.dot(h_ref[...], wb_ref[...], preferred_element_type=F32).astype(o_ref.dtype)


def in_proj(h, w, *, layer, tm, tn):
    T, D = h.shape
    N = w.shape[2]
    assert T % tm == 0 and N % tn == 0
    return pl.pallas_call(
        _in_proj_kernel,
        grid=(N // tn, T // tm),
        in_specs=[
            pl.BlockSpec((tm, D), lambda j, i: (i, 0)),
            pl.BlockSpec((None, D, tn), lambda j, i: (layer, 0, j)),
        ],
        out_specs=pl.BlockSpec((tm, tn), lambda j, i: (i, j)),
        out_shape=jax.ShapeDtypeStruct((T, N), BF16),
        scratch_shapes=[pltpu.VMEM((D, tn), BF16)],
        compiler_params=_compiler_params(("parallel", "arbitrary")),
        name="in_proj",
    )(h, w)


def _group_mean_sq(t, gsum):
    sq = t * t
    hi = sq.astype(BF16)
    lo = (sq - hi.astype(F32)).astype(BF16)
    return jnp.dot(hi, gsum, preferred_element_type=F32) + jnp.dot(lo, gsum, preferred_element_type=F32)


PREP_ROWS = LANES


def _qk_prep_kernel(pos_ref, invf_ref, expand_ref, keep_ref, gq_ref, gk_ref, q_ref, k_ref, v_ref,
                    qt_ref, ko_ref, vt_ref, *, tk):
    tm = q_ref.shape[0]
    row = lax.broadcasted_iota(jnp.int32, (LANES, LANES), 0)
    col = lax.broadcasted_iota(jnp.int32, (LANES, LANES), 1)
    gsum = jnp.where((row < DIFF_D) == (col < DIFF_D), 1.0 / DIFF_D, 0.0).astype(BF16)
    trig_row = lax.broadcasted_iota(jnp.int32, (2 * ROT_HALF, 1), 0)
    trig_pad = jnp.zeros((LANES - 2 * ROT_HALF, PREP_ROWS), F32)
    pad_row = lax.broadcasted_iota(jnp.int32, (V_ROWS - DIFF_V, PREP_ROWS), 0)
    ones_row = jnp.where(pad_row == 0, 1.0, 0.0).astype(vt_ref.dtype)
    gq = gq_ref[...]
    gk = gk_ref[...]
    expand = expand_ref[...]

    for r0 in range(0, tm, PREP_ROWS):
        ang = invf_ref[...] * pos_ref[:, r0:r0 + PREP_ROWS].astype(F32)
        trig_t = jnp.where(trig_row < ROT_HALF, jnp.cos(ang), jnp.sin(ang))
        cos_t, sin_t = trig_t[:ROT_HALF], trig_t[ROT_HALF:]
        trig = jnp.concatenate([trig_t, trig_pad], axis=0).T
        hi = trig.astype(BF16)
        lo = (trig - hi.astype(F32)).astype(BF16)
        tab = (jnp.dot(hi, expand, preferred_element_type=F32)
               + jnp.dot(lo, expand, preferred_element_type=F32))
        c_keep = tab[:, :LANES] + keep_ref[...]
        s_from_hi = tab[:, LANES:2 * LANES]
        s_from_lo = tab[:, 2 * LANES:]

        def norm(src_ref, g, c):
            t = src_ref[r0:r0 + PREP_ROWS, c * LANES:(c + 1) * LANES].astype(F32)
            return t * lax.rsqrt(_group_mean_sq(t, gsum) + EPS) * g

        def rot_lanes(tn):
            up = pltpu.roll(tn, LANES - ROT_HALF, axis=1)
            dn = pltpu.roll(tn, ROT_HALF, axis=1)
            return tn * c_keep + up * s_from_hi + dn * s_from_lo

        def rot_rows(tt):
            parts = []
            for base in range(0, LANES, DIFF_D):
                x1 = tt[base:base + ROT_HALF]
                x2 = tt[base + ROT_HALF:base + ROT]
                parts += [x1 * cos_t - x2 * sin_t, x2 * cos_t + x1 * sin_t, tt[base + ROT:base + DIFF_D]]
            return jnp.concatenate(parts, axis=0)

        u, off = divmod(r0, tk)
        for h in range(DIFF_HEADS):
            sl = slice(h * LANES, (h + 1) * LANES)
            qt = rot_rows(norm(q_ref, gq, h).T)
            qt_ref[h, 0, :, r0:r0 + PREP_ROWS] = qt.astype(qt_ref.dtype)
            ko_ref[r0:r0 + PREP_ROWS, sl] = rot_lanes(norm(k_ref, gk, h)).astype(ko_ref.dtype)
            vt_ref[h, u, :DIFF_V, off:off + PREP_ROWS] = v_ref[r0:r0 + PREP_ROWS, sl].T
            vt_ref[h, u, DIFF_V:, off:off + PREP_ROWS] = ones_row


def _rotary_tables():
    inv_freq = ROPE_THETA ** (-jnp.arange(ROT_HALF, dtype=F32) / ROT_HALF)
    invf = jnp.concatenate([inv_freq, inv_freq]).reshape(2 * ROT_HALF, 1)
    src = jnp.arange(LANES)[:, None]
    d = jnp.arange(LANES)[None, :] % DIFF_D
    cos_part = jnp.where((d < ROT) & (src == d % ROT_HALF), 1.0, 0.0)
    hi_part = jnp.where((d < ROT_HALF) & (src == ROT_HALF + d), -1.0, 0.0)
    lo_part = jnp.where((d >= ROT_HALF) & (d < ROT) & (src == d), 1.0, 0.0)
    expand = jnp.concatenate([cos_part, hi_part, lo_part], axis=1).astype(BF16)
    keep = jnp.where(d < ROT, 0.0, 1.0).astype(F32)
    return invf, expand, keep


def qk_prep(proj, pos, g_q, g_k, *, q_col, k_col, v_col, tm, tk):
    T = proj.shape[0]
    assert T % tm == 0 and tm % tk == 0 and tk % PREP_ROWS == 0
    assert q_col % DIFF_QK == 0 and k_col % DIFF_QK == 0 and v_col % DIFF_VW == 0 and DIFF_V == LANES
    gq = (jnp.tile(g_q, LANES // DIFF_D) * Q_SCALE).reshape(1, LANES)
    gk = jnp.tile(g_k, LANES // DIFF_D).reshape(1, LANES)
    invf, expand, keep = _rotary_tables()
    return pl.pallas_call(
        functools.partial(_qk_prep_kernel, tk=tk),
        grid=(T // tm,),
        in_specs=[
            pl.BlockSpec((None, 1, tm), lambda i: (i, 0, 0)),
            pl.BlockSpec((2 * ROT_HALF, 1), lambda i: (0, 0)),
            pl.BlockSpec((LANES, 3 * LANES), lambda i: (0, 0)),
            pl.BlockSpec((1, LANES), lambda i: (0, 0)),
            pl.BlockSpec((1, LANES), lambda i: (0, 0)),
            pl.BlockSpec((1, LANES), lambda i: (0, 0)),
            pl.BlockSpec((tm, DIFF_QK), lambda i: (i, q_col // DIFF_QK)),
            pl.BlockSpec((tm, DIFF_QK), lambda i: (i, k_col // DIFF_QK)),
            pl.BlockSpec((tm, DIFF_VW), lambda i: (i, v_col // DIFF_VW)),
        ],
        out_specs=[
            pl.BlockSpec((DIFF_HEADS, 1, LANES, tm), lambda i: (0, i, 0, 0)),
            pl.BlockSpec((tm, DIFF_QK), lambda i: (i, 0)),
            pl.BlockSpec((DIFF_HEADS, tm // tk, V_ROWS, tk), lambda i: (0, i, 0, 0)),
        ],
        out_shape=[
            jax.ShapeDtypeStruct((DIFF_HEADS, T // tm, LANES, tm), BF16),
            jax.ShapeDtypeStruct((T, DIFF_QK), BF16),
            jax.ShapeDtypeStruct((DIFF_HEADS, T // tk, V_ROWS, tk), BF16),
        ],
        compiler_params=_compiler_params(("parallel",)),
        name="qk_prep",
    )(pos, invf, expand, keep, gq, gk, proj, proj, proj)


def _diff_attn_kernel(lam_ref, g_ref, qt_ref, k_ref, vt_ref, o_ref, m_ref, acc_ref,
                      s0_ref, s1_ref, p0_ref, p1_ref, a0_ref, a1_ref, bm0_ref, bm1_ref,
                      *, tq, tk, lam_init):
    assert tq == 2 * tk
    s_bufs, p_bufs, a_bufs, bm_bufs = (s0_ref, s1_ref), (p0_ref, p1_ref), (a0_ref, a1_ref), (bm0_ref, bm1_ref)
    n_tiles = qt_ref.shape[1]

    lv = lam_ref[...]
    e1 = jnp.exp(jnp.sum(lv[0:1, :] * lv[1:2, :], axis=-1, keepdims=True))
    e2 = jnp.exp(jnp.sum(lv[2:3, :] * lv[3:4, :], axis=-1, keepdims=True))
    lam = e1 - e2 + lam_init

    acc_ref[...] = jnp.zeros(acc_ref.shape, F32)

    def scores(i, u, slot, masked, q0=0):
        k = k_ref[pl.ds(pl.multiple_of(u * tk, tk), tk), :]
        qt = qt_ref[0, i, :, q0:]
        nq_cols = tq - q0
        zeros = jnp.zeros((DIFF_D, nq_cols), qt.dtype)
        q_comp = (jnp.concatenate([qt[:DIFF_D], zeros], axis=0),
                  jnp.concatenate([zeros, qt[DIFF_D:]], axis=0))
        if masked:
            kpos = u * tk + lax.broadcasted_iota(jnp.int32, (tk, nq_cols), 0)
            qpos = i * tq + q0 + lax.broadcasted_iota(jnp.int32, (tk, nq_cols), 1)
            causal = kpos <= qpos
        for c in range(2):
            s = jnp.dot(k, q_comp[c], preferred_element_type=F32)
            if masked:
                s = jnp.where(causal, s, -jnp.inf)
            s_bufs[slot][c, :, q0:] = s
            bm_bufs[slot][c, :, q0:] = jnp.max(s, axis=0, keepdims=True)

    def softmax(slot, first=False, q0=0):
        for c in range(2):
            if first:
                m_new = bm_bufs[slot][c]
                a_bufs[slot][c] = jnp.zeros_like(m_new)
            else:
                m_prev = m_ref[c, :, q0:]
                m_new = jnp.maximum(m_prev, bm_bufs[slot][c, :, q0:])
                a_bufs[slot][c, :, q0:] = jnp.exp2(m_prev - m_new)
            m_ref[c, :, q0:] = m_new
            p_bufs[slot][c, :, q0:] = jnp.exp2(s_bufs[slot][c, :, q0:] - m_new).astype(p_bufs[slot].dtype)

    def accumulate(u, slot, q0=0):
        vt = vt_ref[0, u]
        for c in range(2):
            pv = jnp.dot(vt, p_bufs[slot][c, :, q0:], preferred_element_type=F32)
            acc_ref[c, :, q0:] = a_bufs[slot][c, :, q0:] * acc_ref[c, :, q0:] + pv

    def finalize(i):
        o = (acc_ref[0, :DIFF_V] / acc_ref[0, DIFF_V:DIFF_V + 1]
             - lam * (acc_ref[1, :DIFF_V] / acc_ref[1, DIFF_V:DIFF_V + 1]))
        o = o * lax.rsqrt(jnp.mean(o * o, axis=0, keepdims=True) + EPS)
        rows = pl.ds(pl.multiple_of(i * tq, tq), tq)
        o_ref[rows, :] = (o.T * g_ref[...] * (1.0 - lam_init)).astype(o_ref.dtype)

    def step_pair(i, u, masked):
        scores(i, u, 0, masked)
        accumulate(u - 2, 0)
        softmax(1)
        scores(i, u + 1, 1, masked, q0=tk if masked else 0)
        accumulate(u - 1, 1)
        softmax(0)

    scores(0, 0, 0, masked=True)
    scores(0, 1, 1, masked=True, q0=tk)
    softmax(0, first=True)

    def next_tile(i):
        last = 2 * (i - 1)
        scores(i, 0, 0, masked=False)
        accumulate(last, 0)
        softmax(1, q0=tk)
        scores(i, 1, 1, masked=False)
        accumulate(last + 1, 1, q0=tk)
        finalize(i - 1)
        softmax(0, first=True)

    def full_pairs(i):
        def full_pair(r, c):
            step_pair(i, 2 * r, masked=False)
            return c

        lax.fori_loop(1, i, full_pair, 0)

    if n_tiles > 1:
        next_tile(1)

        def q_tile(i, carry):
            full_pairs(i)
            step_pair(i, 2 * i, masked=True)
            next_tile(i + 1)
            return carry

        lax.fori_loop(1, n_tiles - 1, q_tile, 0)
        full_pairs(n_tiles - 1)
        step_pair(n_tiles - 1, 2 * (n_tiles - 1), masked=True)

    last = 2 * (n_tiles - 1)
    accumulate(last, 0)
    softmax(1, q0=tk)
    accumulate(last + 1, 1, q0=tk)
    finalize(n_tiles - 1)


def diff_attn(qt, k, vt, lambda_vecs, g_subln, *, batch, tq, tk, lam_init):
    T = k.shape[0]
    S = T // batch
    assert S % tq == 0 and tq == 2 * tk
    nq, nk = S // tq, S // tk
    stat = pltpu.VMEM((2, 1, tq), F32)
    return pl.pallas_call(
        functools.partial(_diff_attn_kernel, tq=tq, tk=tk, lam_init=lam_init),
        grid=(batch, DIFF_HEADS),
        in_specs=[
            pl.BlockSpec((4, DIFF_D), lambda b, h: (0, 0)),
            pl.BlockSpec((1, DIFF_V), lambda b, h: (0, 0)),
            pl.BlockSpec((1, nq, LANES, tq), lambda b, h: (h, b, 0, 0)),
            pl.BlockSpec((S, LANES), lambda b, h: (b, h)),
            pl.BlockSpec((1, nk, V_ROWS, tk), lambda b, h: (h, b, 0, 0)),
        ],
        out_specs=pl.BlockSpec((S, DIFF_V), lambda b, h: (b, h)),
        out_shape=jax.ShapeDtypeStruct((T, DIFF_VW), BF16),
        scratch_shapes=[
            stat,
            pltpu.VMEM((2, V_ROWS, tq), F32),
            pltpu.VMEM((2, tk, tq), F32),
            pltpu.VMEM((2, tk, tq), F32),
            pltpu.VMEM((2, tk, tq), BF16),
            pltpu.VMEM((2, tk, tq), BF16),
            stat, stat,
            stat, stat,
        ],
        compiler_params=_compiler_params(("parallel", "parallel")),
        name="diff_attn",
    )(lambda_vecs, g_subln.reshape(1, DIFF_V), qt, k, vt)


def _mem_prep_kernel(kv_ref, g_ref, mk_ref, mv_ref):
    g = g_ref[...]
    for h in range(X_HEADS):
        sl = slice(h * X_HEAD_DIM, (h + 1) * X_HEAD_DIM)
        kh = _rms_rows(kv_ref[:, sl], g) * (X_HEAD_DIM ** -0.5)
        mk_ref[:, sl] = kh.astype(mk_ref.dtype)
    mv_ref[...] = kv_ref[:, X_WIDTH:].astype(mv_ref.dtype)


def mem_prep(kv, g_k_x):
    R = kv.shape[0]
    out = jax.ShapeDtypeStruct((R, X_WIDTH), BF16)
    return pl.pallas_call(
        _mem_prep_kernel,
        grid=(1,),
        in_specs=[
            pl.BlockSpec((R, 2 * X_WIDTH), lambda i: (0, 0)),
            pl.BlockSpec((1, X_HEAD_DIM), lambda i: (0, 0)),
        ],
        out_specs=[pl.BlockSpec((R, X_WIDTH), lambda i: (0, 0))] * 2,
        out_shape=[out, out],
        compiler_params=_compiler_params(("arbitrary",)),
        name="mem_prep",
    )(kv, g_k_x.reshape(1, X_HEAD_DIM))


MERGE_CHUNK = 512
GATE_BLOCK = 1024
assert GATE_BLOCK % MERGE_CHUNK == 0


def _merge_kernel(*refs, tiles_per_seq, n_gate_blocks):
    gate_refs = refs[:n_gate_blocks]
    (cb_ref, cc_ref, cx_ref, cch_ref, cxh_ref, xq_ref, ob_ref, x_ref,
     convw_ref, gqx_ref, gn_ref, mk_ref, mv_ref, wa_ref, wb_ref, wc_ref, wo_ref,
     o_ref, hn_ref, ua_ref, oc_ref, mg_ref, yb_ref) = refs[n_gate_blocks:]
    tm = x_ref.shape[0]
    d_model = x_ref.shape[1]
    first = pl.program_id(0) % tiles_per_seq == 0

    def gate(branch, c):
        blk, off = divmod(branch * d_model + c * MERGE_CHUNK, GATE_BLOCK)
        return jax.nn.sigmoid(gate_refs[blk][:, off:off + MERGE_CHUNK].astype(F32))

    ob = ob_ref[...]
    n_chunks = d_model // MERGE_CHUNK

    def diff_branch(c):
        sl = slice(c * MERGE_CHUNK, (c + 1) * MERGE_CHUNK)
        yb_ref[:, sl] = gate(1, c) * jnp.dot(ob, wb_ref[:, sl], preferred_element_type=F32)

    diff_branch(0)

    gq = gqx_ref[...]
    scores = []
    for h in range(X_HEADS):
        sl = slice(h * X_HEAD_DIM, (h + 1) * X_HEAD_DIM)
        qn = _rms_rows(xq_ref[:, sl].astype(F32), gq).astype(BF16)
        scores.append(lax.dot_general(qn, mk_ref[:, sl], (((1,), (1,)), ((), ())),
                                      preferred_element_type=F32))
    diff_branch(1)
    for h, s in enumerate(scores):
        sl = slice(h * X_HEAD_DIM, (h + 1) * X_HEAD_DIM)
        s = s - jnp.max(s, axis=-1, keepdims=True)
        p = jnp.exp(s)
        p = p / jnp.sum(p, axis=-1, keepdims=True)
        oc_ref[:, sl] = jnp.dot(p.astype(BF16), mv_ref[:, sl], preferred_element_type=F32).astype(oc_ref.dtype)
    for c in range(2, n_chunks):
        diff_branch(c)

    z = cc_ref[...].astype(F32) * cx_ref[...].astype(F32)
    zp = cch_ref[...].astype(F32) * cxh_ref[...].astype(F32)
    zp = jnp.where(first, 0.0, zp)
    w = convw_ref[...]
    row8 = lax.broadcasted_iota(jnp.int32, (SUBLANES, 1), 0)
    y = z * w[CONV_K - 1:CONV_K, :]
    for back in range(1, CONV_K):
        zs = pltpu.roll(z, back, axis=0)
        ps = pltpu.roll(zp, back, axis=0)
        head = jnp.where(row8 < back, ps, zs[:SUBLANES])
        zs = jnp.concatenate([head, zs[SUBLANES:]], axis=0)
        y = y + zs * w[CONV_K - 1 - back:CONV_K - back, :]
    ua_ref[...] = (cb_ref[...].astype(F32) * y).astype(ua_ref.dtype)

    ua = ua_ref[...]
    oc = oc_ref[...]
    for c in range(n_chunks):
        sl = slice(c * MERGE_CHUNK, (c + 1) * MERGE_CHUNK)
        merged = yb_ref[:, sl]
        for br, act, w_ref in ((0, ua, wa_ref), (2, oc, wc_ref)):
            merged = merged + gate(br, c) * jnp.dot(act, w_ref[:, sl], preferred_element_type=F32)
        mg_ref[:, sl] = merged.astype(mg_ref.dtype)

    y = x_ref[...] + jnp.dot(mg_ref[...], wo_ref[...], preferred_element_type=F32)
    o_ref[...] = y
    hn_ref[...] = _rms_rows(y, gn_ref[...]).astype(hn_ref.dtype)


def merge(proj, ob, x, conv_w, g_q_x, g_next, mk, mv, w_conv_out, w_diff_out, w_x_out, w_o,
          *, layer, batch, cols, tm):
    T, D = x.shape
    S = T // batch
    M = mk.shape[0] // batch
    assert S % tm == 0 and tm % SUBLANES == 0
    tiles_per_seq = S // tm
    halo_blocks = tm // SUBLANES

    def col_spec(name, width):
        assert cols[name] % width == 0
        return pl.BlockSpec((tm, width), lambda i: (i, cols[name] // width))

    def halo_spec(name):
        return pl.BlockSpec((SUBLANES, CONV_WIDTH),
                            lambda i: (jnp.maximum(i * halo_blocks - 1, 0), cols[name] // CONV_WIDTH))

    def resident(shape):
        return pl.BlockSpec((None,) + shape, lambda i: (layer, 0, 0), pipeline_mode=pl.Buffered(1))

    assert cols["gates"] % GATE_BLOCK == 0 and (N_BRANCH * D) % GATE_BLOCK == 0 and D % MERGE_CHUNK == 0
    n_gate_blocks = N_BRANCH * D // GATE_BLOCK
    gate_specs = [pl.BlockSpec((tm, GATE_BLOCK), lambda i, g=g: (i, cols["gates"] // GATE_BLOCK + g))
                  for g in range(n_gate_blocks)]

    return pl.pallas_call(
        functools.partial(_merge_kernel, tiles_per_seq=tiles_per_seq, n_gate_blocks=n_gate_blocks),
        grid=(T // tm,),
        in_specs=gate_specs + [
            col_spec("cb", CONV_WIDTH),
            col_spec("cc", CONV_WIDTH),
            col_spec("cx", CONV_WIDTH),
            halo_spec("cc"),
            halo_spec("cx"),
            col_spec("xq", X_WIDTH),
            pl.BlockSpec((tm, DIFF_VW), lambda i: (i, 0)),
            pl.BlockSpec((tm, D), lambda i: (i, 0)),
            pl.BlockSpec((CONV_K, CONV_WIDTH), lambda i: (0, 0)),
            pl.BlockSpec((1, X_HEAD_DIM), lambda i: (0, 0)),
            pl.BlockSpec((1, D), lambda i: (0, 0)),
            pl.BlockSpec((M, X_WIDTH), lambda i: (i // tiles_per_seq, 0)),
            pl.BlockSpec((M, X_WIDTH), lambda i: (i // tiles_per_seq, 0)),
            resident((CONV_WIDTH, D)),
            resident((DIFF_VW, D)),
            resident((X_WIDTH, D)),
            resident((D, D)),
        ],
        out_specs=[pl.BlockSpec((tm, D), lambda i: (i, 0))] * 2,
        out_shape=[jax.ShapeDtypeStruct((T, D), F32), jax.ShapeDtypeStruct((T, D), BF16)],
        scratch_shapes=[
            pltpu.VMEM((tm, CONV_WIDTH), BF16),
            pltpu.VMEM((tm, X_WIDTH), BF16),
            pltpu.VMEM((tm, D), BF16),
            pltpu.VMEM((tm, D), F32),
        ],
        compiler_params=_compiler_params(("parallel",)),
        name="merge",
    )(*([proj] * (n_gate_blocks + 6)), ob, x, conv_w, g_q_x.reshape(1, X_HEAD_DIM), g_next.reshape(1, D),
      mk, mv, w_conv_out, w_diff_out, w_x_out, w_o)


def _ffn_up_kernel(h_ref, wg_ref, wu_ref, wd_ref, o_ref, wdb_ref, wgb_ref, wub_ref):
    @pl.when(pl.program_id(1) == 0)
    def _():
        wgb_ref[...] = wg_ref[...].astype(wgb_ref.dtype)
        wub_ref[...] = wu_ref[...].astype(wub_ref.dtype)

    wdb_ref[...] = wd_ref[...].astype(wdb_ref.dtype)

    h = h_ref[...]
    a = jnp.dot(h, wgb_ref[...], preferred_element_type=F32)
    b = jnp.dot(h, wub_ref[...], preferred_element_type=F32)
    o_ref[...] = (a * jax.nn.sigmoid(a) * b).astype(o_ref.dtype)


def ffn_up(h, w_gate_up, w_down, *, layer, tm, tf):
    T, D = h.shape
    d_ff = w_gate_up.shape[2] // 2
    assert T % tm == 0 and d_ff % tf == 0
    nf, ni = d_ff // tf, T // tm
    wd_rows = d_ff // (nf * ni)
    assert wd_rows * nf * ni == d_ff and wd_rows % BF16_ROWS == 0
    return pl.pallas_call(
        _ffn_up_kernel,
        grid=(nf, ni),
        in_specs=[
            pl.BlockSpec((tm, D), lambda f, i: (i, 0)),
            pl.BlockSpec((None, D, tf), lambda f, i: (layer, 0, f)),
            pl.BlockSpec((None, D, tf), lambda f, i: (layer, 0, nf + f)),
            pl.BlockSpec((None, wd_rows, D), lambda f, i: (layer, f * ni + i, 0)),
        ],
        out_specs=[
            pl.BlockSpec((tm, tf), lambda f, i: (i, f)),
            pl.BlockSpec((wd_rows, D), lambda f, i: (f * ni + i, 0)),
        ],
        out_shape=[jax.ShapeDtypeStruct((T, d_ff), BF16), jax.ShapeDtypeStruct((d_ff, D), BF16)],
        scratch_shapes=[pltpu.VMEM((D, tf), BF16), pltpu.VMEM((D, tf), BF16)],
        compiler_params=_compiler_params(("parallel", "arbitrary")),
        name="ffn_up",
    )(h, w_gate_up, w_gate_up, w_down)


def _ffn_down_kernel(a_ref, x_ref, wd_ref, *rest, normed_out):
    if normed_out:
        gn_ref, o_ref, hn_ref = rest
    else:
        (o_ref,) = rest
    y = x_ref[...] + jnp.dot(a_ref[...], wd_ref[...], preferred_element_type=F32)
    o_ref[...] = y
    if normed_out:
        hn_ref[...] = _rms_rows(y, gn_ref[...]).astype(hn_ref.dtype)


def ffn_down(act, x, w_down, *, tm, g_next=None):
    T, D = x.shape
    d_ff = act.shape[1]
    assert T % tm == 0
    normed_out = g_next is not None
    row_spec = pl.BlockSpec((tm, D), lambda i: (i, 0))
    in_specs = [
        pl.BlockSpec((tm, d_ff), lambda i: (i, 0)),
        row_spec,
        pl.BlockSpec((d_ff, D), lambda i: (0, 0), pipeline_mode=pl.Buffered(1)),
    ]
    args = [act, x, w_down]
    out_specs, out_shape = row_spec, jax.ShapeDtypeStruct((T, D), F32)
    if normed_out:
        in_specs.append(pl.BlockSpec((1, D), lambda i: (0, 0)))
        args.append(g_next.reshape(1, D))
        out_specs = [row_spec, row_spec]
        out_shape = [out_shape, jax.ShapeDtypeStruct((T, D), BF16)]
    return pl.pallas_call(
        functools.partial(_ffn_down_kernel, normed_out=normed_out),
        grid=(T // tm,),
        in_specs=in_specs,
        out_specs=out_specs,
        out_shape=out_shape,
        compiler_params=_compiler_params(("parallel",)),
        name="ffn_down",
    )(*args)


def _tile(n, want):
    t = min(n, want)
    while n % t:
        t -= SUBLANES
    return t


def kernel(x, mem, positions, g_mix, w_in, conv_w, w_conv_out, g_q_diff, g_k_diff, lambda_vecs,
           g_subln, w_diff_out, g_mem, w_mem_kv, g_q_x, g_k_x, w_x_out, w_o, g_ffn, w_gate_up, w_down):
    B, S, D = x.shape
    M = mem.shape[1]
    T = B * S
    depth = w_in.shape[0]
    d_ff = w_down.shape[1]

    n_conv = 3 * CONV_WIDTH
    n_diff = 2 * DIFF_QK + DIFF_VW
    cols = {
        "cb": 0,
        "cc": CONV_WIDTH,
        "cx": 2 * CONV_WIDTH,
        "dq": n_conv,
        "dk": n_conv + DIFF_QK,
        "dv": n_conv + 2 * DIFF_QK,
        "xq": n_conv + n_diff,
        "gates": n_conv + n_diff + X_WIDTH,
    }

    xf = x.reshape(T, D)
    memf = mem.reshape(B * M, D)

    tm_proj = _tile(T, 2048)
    tq = _tile(S, 1024)
    tk = tq // 2
    pos = positions.reshape(T // tq, 1, tq)
    tm_merge = _tile(S, 256)
    tm_up = _tile(T, 2048)
    tf = _tile(d_ff, 512)
    tm_down = _tile(T, 256)

    w_mem_kv, w_conv_out, w_diff_out, w_x_out, w_o = (
        w.astype(BF16) for w in (w_mem_kv, w_conv_out, w_diff_out, w_x_out, w_o))

    h = rms_norm_rows(xf, g_mix[0], tm=_tile(T, 1024))
    for l in range(depth):
        lam_init = 0.8 - 0.6 * math.exp(-0.3 * l)

        proj = in_proj(h, w_in, layer=l, tm=tm_proj, tn=1024)
        qt, k, vt = qk_prep(proj, pos, g_q_diff[l], g_k_diff[l],
                            q_col=cols["dq"], k_col=cols["dk"], v_col=cols["dv"], tm=tq, tk=tk)
        ob = diff_attn(qt, k, vt, lambda_vecs[l], g_subln[l], batch=B, tq=tq, tk=tk, lam_init=lam_init)

        kv = norm_matmul(memf, g_mem[l], w_mem_kv, layer=l, tm=B * M, tn=1024, out_dtype=F32)
        mk, mv = mem_prep(kv, g_k_x[l])

        xf, h2 = merge(proj, ob, xf, conv_w[l], g_q_x[l], g_ffn[l], mk, mv,
                       w_conv_out, w_diff_out, w_x_out, w_o, layer=l, batch=B, cols=cols, tm=tm_merge)
        act, w_down_l = ffn_up(h2, w_gate_up, w_down, layer=l, tm=tm_up, tf=tf)
        if l + 1 < depth:
            xf, h = ffn_down(act, xf, w_down_l, tm=tm_down, g_next=g_mix[l + 1])
        else:
            xf = ffn_down(act, xf, w_down_l, tm=tm_down)

    return xf.reshape(B, S, D)
```

```python
import functools
import math

import jax
import jax.numpy as jnp
from jax import lax
from jax.experimental import pallas as pl
from jax.experimental.pallas import tpu as pltpu

EPS = 1e-6
CONV_WIDTH = 1024
CONV_K = 3
DIFF_HEADS = 8
DIFF_D = 64
DIFF_V = 2 * DIFF_D
DIFF_QK = DIFF_HEADS * 2 * DIFF_D
DIFF_VW = DIFF_HEADS * DIFF_V
X_HEADS = 4
X_HEAD_DIM = 256
X_WIDTH = X_HEADS * X_HEAD_DIM
N_BRANCH = 3
ROPE_THETA = 500000.0
ROT_FRAC = 4
ROT = DIFF_D // ROT_FRAC
ROT_HALF = ROT // 2

LANES = 128
SUBLANES = 8
BF16_ROWS = 16
V_ROWS = DIFF_V + BF16_ROWS
Q_SCALE = DIFF_D ** -0.5 * math.log2(math.e)
VMEM_LIMIT_BYTES = 56 * 1024 * 1024
MXU_ROW_CHUNK = 512

F32 = jnp.float32
BF16 = jnp.bfloat16


def _compiler_params(semantics):
    return pltpu.CompilerParams(dimension_semantics=semantics, vmem_limit_bytes=VMEM_LIMIT_BYTES)


def _rms_rows(t, g):
    ms = jnp.mean(t * t, axis=-1, keepdims=True)
    return t * lax.rsqrt(ms + EPS) * g


def _norm_matmul_kernel(x_ref, g_ref, w_ref, o_ref, h_ref):
    @pl.when(pl.program_id(1) == 0)
    def _():
        h_ref[...] = _rms_rows(x_ref[...], g_ref[...]).astype(h_ref.dtype)

    o_ref[...] = jnp.dot(h_ref[...], w_ref[...], preferred_element_type=F32).astype(o_ref.dtype)


def norm_matmul(x, g, w, *, layer, tm, tn, out_dtype):
    T, D = x.shape
    N = w.shape[2]
    assert T % tm == 0 and N % tn == 0
    return pl.pallas_call(
        _norm_matmul_kernel,
        grid=(T // tm, N // tn),
        in_specs=[
            pl.BlockSpec((tm, D), lambda i, j: (i, 0)),
            pl.BlockSpec((1, D), lambda i, j: (0, 0)),
            pl.BlockSpec((None, D, tn), lambda i, j: (layer, 0, j)),
        ],
        out_specs=pl.BlockSpec((tm, tn), lambda i, j: (i, j)),
        out_shape=jax.ShapeDtypeStruct((T, N), out_dtype),
        scratch_shapes=[pltpu.VMEM((tm, D), BF16)],
        compiler_params=_compiler_params(("parallel", "arbitrary")),
        name="norm_matmul",
    )(x, g.reshape(1, D), w)


def _rms_norm_kernel(x_ref, g_ref, o_ref):
    o_ref[...] = _rms_rows(x_ref[...], g_ref[...]).astype(o_ref.dtype)


def rms_norm_rows(x, g, *, tm):
    T, D = x.shape
    assert T % tm == 0
    return pl.pallas_call(
        _rms_norm_kernel,
        grid=(T // tm,),
        in_specs=[pl.BlockSpec((tm, D), lambda i: (i, 0)), pl.BlockSpec((1, D), lambda i: (0, 0))],
        out_specs=pl.BlockSpec((tm, D), lambda i: (i, 0)),
        out_shape=jax.ShapeDtypeStruct((T, D), BF16),
        compiler_params=_compiler_params(("parallel",)),
        name="rms_norm",
    )(x, g.reshape(1, D))


def _in_proj_kernel(h_ref, w_ref, o_ref, wb_ref):
    @pl.when(pl.program_id(1) == 0)
    def _():
        wb_ref[...] = w_ref[...].astype(wb_ref.dtype)

    for r0 in range(0, h_ref.shape[0], MXU_ROW_CHUNK):
        rows = slice(r0, r0 + MXU_ROW_CHUNK)
        o_ref[rows, :] = jnp.dot(h_ref[rows, :], wb_ref[...], preferred_element_type=F32).astype(o_ref.dtype)


def in_proj(h, w, *, layer, tm, tn):
    T, D = h.shape
    N = w.shape[2]
    assert T % tm == 0 and N % tn == 0
    return pl.pallas_call(
        _in_proj_kernel,
        grid=(N // tn, T // tm),
        in_specs=[
            pl.BlockSpec((tm, D), lambda j, i: (i, 0)),
            pl.BlockSpec((None, D, tn), lambda j, i: (layer, 0, j)),
        ],
        out_specs=pl.BlockSpec((tm, tn), lambda j, i: (i, j)),
        out_shape=jax.ShapeDtypeStruct((T, N), BF16),
        scratch_shapes=[pltpu.VMEM((D, tn), BF16)],
        compiler_params=_compiler_params(("parallel", "arbitrary")),
        name="in_proj",
    )(h, w)


def _group_mean_sq(t, gsum):
    sq = t * t
    hi = sq.astype(BF16)
    lo = (sq - hi.astype(F32)).astype(BF16)
    return jnp.dot(hi, gsum, preferred_element_type=F32) + jnp.dot(lo, gsum, preferred_element_type=F32)


PREP_ROWS = LANES


def _qk_prep_kernel(pos_ref, invf_ref, expand_ref, keep_ref, gq_ref, gk_ref, q_ref, k_ref, v_ref,
                    qt_ref, ko_ref, vt_ref, *, tk):
    tm = q_ref.shape[0]
    row = lax.broadcasted_iota(jnp.int32, (LANES, LANES), 0)
    col = lax.broadcasted_iota(jnp.int32, (LANES, LANES), 1)
    gsum = jnp.where((row < DIFF_D) == (col < DIFF_D), 1.0 / DIFF_D, 0.0).astype(BF16)
    trig_row = lax.broadcasted_iota(jnp.int32, (2 * ROT_HALF, 1), 0)
    trig_pad = jnp.zeros((LANES - 2 * ROT_HALF, PREP_ROWS), F32)
    pad_row = lax.broadcasted_iota(jnp.int32, (V_ROWS - DIFF_V, PREP_ROWS), 0)
    ones_row = jnp.where(pad_row == 0, 1.0, 0.0).astype(vt_ref.dtype)
    gq = gq_ref[...]
    gk = gk_ref[...]
    expand = expand_ref[...]

    for r0 in range(0, tm, PREP_ROWS):
        ang = invf_ref[...] * pos_ref[:, r0:r0 + PREP_ROWS].astype(F32)
        trig_t = jnp.where(trig_row < ROT_HALF, jnp.cos(ang), jnp.sin(ang))
        cos_t, sin_t = trig_t[:ROT_HALF], trig_t[ROT_HALF:]
        trig = jnp.concatenate([trig_t, trig_pad], axis=0).T
        hi = trig.astype(BF16)
        lo = (trig - hi.astype(F32)).astype(BF16)
        tab = (jnp.dot(hi, expand, preferred_element_type=F32)
               + jnp.dot(lo, expand, preferred_element_type=F32))
        c_keep = tab[:, :LANES] + keep_ref[...]
        s_from_hi = tab[:, LANES:2 * LANES]
        s_from_lo = tab[:, 2 * LANES:]

        def norm(src_ref, g, c):
            t = src_ref[r0:r0 + PREP_ROWS, c * LANES:(c + 1) * LANES].astype(F32)
            return t * lax.rsqrt(_group_mean_sq(t, gsum) + EPS) * g

        def rot_lanes(tn):
            up = pltpu.roll(tn, LANES - ROT_HALF, axis=1)
            dn = pltpu.roll(tn, ROT_HALF, axis=1)
            return tn * c_keep + up * s_from_hi + dn * s_from_lo

        def rot_rows(tt):
            parts = []
            for base in range(0, LANES, DIFF_D):
                x1 = tt[base:base + ROT_HALF]
                x2 = tt[base + ROT_HALF:base + ROT]
                parts += [x1 * cos_t - x2 * sin_t, x2 * cos_t + x1 * sin_t, tt[base + ROT:base + DIFF_D]]
            return jnp.concatenate(parts, axis=0)

        u, off = divmod(r0, tk)
        for h in range(DIFF_HEADS):
            sl = slice(h * LANES, (h + 1) * LANES)
            qt = rot_rows(norm(q_ref, gq, h).T)
            qt_ref[h, 0, :, r0:r0 + PREP_ROWS] = qt.astype(qt_ref.dtype)
            ko_ref[r0:r0 + PREP_ROWS, sl] = rot_lanes(norm(k_ref, gk, h)).astype(ko_ref.dtype)
            vt_ref[h, u, :DIFF_V, off:off + PREP_ROWS] = v_ref[r0:r0 + PREP_ROWS, sl].T
            vt_ref[h, u, DIFF_V:, off:off + PREP_ROWS] = ones_row


def _rotary_tables():
    inv_freq = ROPE_THETA ** (-jnp.arange(ROT_HALF, dtype=F32) / ROT_HALF)
    invf = jnp.concatenate([inv_freq, inv_freq]).reshape(2 * ROT_HALF, 1)
    src = jnp.arange(LANES)[:, None]
    d = jnp.arange(LANES)[None, :] % DIFF_D
    cos_part = jnp.where((d < ROT) & (src == d % ROT_HALF), 1.0, 0.0)
    hi_part = jnp.where((d < ROT_HALF) & (src == ROT_HALF + d), -1.0, 0.0)
    lo_part = jnp.where((d >= ROT_HALF) & (d < ROT) & (src == d), 1.0, 0.0)
    expand = jnp.concatenate([cos_part, hi_part, lo_part], axis=1).astype(BF16)
    keep = jnp.where(d < ROT, 0.0, 1.0).astype(F32)
    return invf, expand, keep


def qk_prep(proj, pos, g_q, g_k, *, q_col, k_col, v_col, tm, tk):
    T = proj.shape[0]
    assert T % tm == 0 and tm % tk == 0 and tk % PREP_ROWS == 0
    assert q_col % DIFF_QK == 0 and k_col % DIFF_QK == 0 and v_col % DIFF_VW == 0 and DIFF_V == LANES
    gq = (jnp.tile(g_q, LANES // DIFF_D) * Q_SCALE).reshape(1, LANES)
    gk = jnp.tile(g_k, LANES // DIFF_D).reshape(1, LANES)
    invf, expand, keep = _rotary_tables()
    return pl.pallas_call(
        functools.partial(_qk_prep_kernel, tk=tk),
        grid=(T // tm,),
        in_specs=[
            pl.BlockSpec((None, 1, tm), lambda i: (i, 0, 0)),
            pl.BlockSpec((2 * ROT_HALF, 1), lambda i: (0, 0)),
            pl.BlockSpec((LANES, 3 * LANES), lambda i: (0, 0)),
            pl.BlockSpec((1, LANES), lambda i: (0, 0)),
            pl.BlockSpec((1, LANES), lambda i: (0, 0)),
            pl.BlockSpec((1, LANES), lambda i: (0, 0)),
            pl.BlockSpec((tm, DIFF_QK), lambda i: (i, q_col // DIFF_QK)),
            pl.BlockSpec((tm, DIFF_QK), lambda i: (i, k_col // DIFF_QK)),
            pl.BlockSpec((tm, DIFF_VW), lambda i: (i, v_col // DIFF_VW)),
        ],
        out_specs=[
            pl.BlockSpec((DIFF_HEADS, 1, LANES, tm), lambda i: (0, i, 0, 0)),
            pl.BlockSpec((tm, DIFF_QK), lambda i: (i, 0)),
            pl.BlockSpec((DIFF_HEADS, tm // tk, V_ROWS, tk), lambda i: (0, i, 0, 0)),
        ],
        out_shape=[
            jax.ShapeDtypeStruct((DIFF_HEADS, T // tm, LANES, tm), BF16),
            jax.ShapeDtypeStruct((T, DIFF_QK), BF16),
            jax.ShapeDtypeStruct((DIFF_HEADS, T // tk, V_ROWS, tk), BF16),
        ],
        compiler_params=_compiler_params(("parallel",)),
        name="qk_prep",
    )(pos, invf, expand, keep, gq, gk, proj, proj, proj)


def _diff_attn_kernel(lam_ref, g_ref, qt_ref, k_ref, vt_ref, o_ref, m_ref, acc_ref,
                      s0_ref, s1_ref, p0_ref, p1_ref, a0_ref, a1_ref, bm0_ref, bm1_ref,
                      *, tq, tk, lam_init):
    assert tq == 2 * tk
    s_bufs, p_bufs, a_bufs, bm_bufs = (s0_ref, s1_ref), (p0_ref, p1_ref), (a0_ref, a1_ref), (bm0_ref, bm1_ref)
    n_tiles = qt_ref.shape[1]

    lv = lam_ref[...]
    e1 = jnp.exp(jnp.sum(lv[0:1, :] * lv[1:2, :], axis=-1, keepdims=True))
    e2 = jnp.exp(jnp.sum(lv[2:3, :] * lv[3:4, :], axis=-1, keepdims=True))
    lam = e1 - e2 + lam_init

    acc_ref[...] = jnp.zeros(acc_ref.shape, F32)

    def scores(i, u, slot, masked, q0=0):
        k = k_ref[pl.ds(pl.multiple_of(u * tk, tk), tk), :]
        qt = qt_ref[0, i, :, q0:]
        nq_cols = tq - q0
        zeros = jnp.zeros((DIFF_D, nq_cols), qt.dtype)
        q_comp = (jnp.concatenate([qt[:DIFF_D], zeros], axis=0),
                  jnp.concatenate([zeros, qt[DIFF_D:]], axis=0))
        if masked:
            kpos = u * tk + lax.broadcasted_iota(jnp.int32, (tk, nq_cols), 0)
            qpos = i * tq + q0 + lax.broadcasted_iota(jnp.int32, (tk, nq_cols), 1)
            causal = kpos <= qpos
        for c in range(2):
            s = jnp.dot(k, q_comp[c], preferred_element_type=F32)
            if masked:
                s = jnp.where(causal, s, -jnp.inf)
            s_bufs[slot][c, :, q0:] = s
            bm_bufs[slot][c, :, q0:] = jnp.max(s, axis=0, keepdims=True)

    def softmax(slot, first=False, q0=0):
        for c in range(2):
            if first:
                m_new = bm_bufs[slot][c]
                a_bufs[slot][c] = jnp.zeros_like(m_new)
            else:
                m_prev = m_ref[c, :, q0:]
                m_new = jnp.maximum(m_prev, bm_bufs[slot][c, :, q0:])
                a_bufs[slot][c, :, q0:] = jnp.exp2(m_prev - m_new)
            m_ref[c, :, q0:] = m_new
            p_bufs[slot][c, :, q0:] = jnp.exp2(s_bufs[slot][c, :, q0:] - m_new).astype(p_bufs[slot].dtype)

    def accumulate(u, slot, q0=0):
        vt = vt_ref[0, u]
        for c in range(2):
            pv = jnp.dot(vt, p_bufs[slot][c, :, q0:], preferred_element_type=F32)
            acc_ref[c, :, q0:] = a_bufs[slot][c, :, q0:] * acc_ref[c, :, q0:] + pv

    def finalize(i):
        o = (acc_ref[0, :DIFF_V] / acc_ref[0, DIFF_V:DIFF_V + 1]
             - lam * (acc_ref[1, :DIFF_V] / acc_ref[1, DIFF_V:DIFF_V + 1]))
        o = o * lax.rsqrt(jnp.mean(o * o, axis=0, keepdims=True) + EPS)
        rows = pl.ds(pl.multiple_of(i * tq, tq), tq)
        o_ref[rows, :] = (o.T * g_ref[...] * (1.0 - lam_init)).astype(o_ref.dtype)

    def step_pair(i, u, masked):
        scores(i, u, 0, masked)
        accumulate(u - 2, 0)
        softmax(1)
        scores(i, u + 1, 1, masked, q0=tk if masked else 0)
        accumulate(u - 1, 1)
        softmax(0)

    scores(0, 0, 0, masked=True)
    scores(0, 1, 1, masked=True, q0=tk)
    softmax(0, first=True)

    def next_tile(i):
        last = 2 * (i - 1)
        scores(i, 0, 0, masked=False)
        accumulate(last, 0)
        softmax(1, q0=tk)
        scores(i, 1, 1, masked=False)
        accumulate(last + 1, 1, q0=tk)
        finalize(i - 1)
        softmax(0, first=True)

    def full_pairs(i):
        def full_pair(r, c):
            step_pair(i, 2 * r, masked=False)
            return c

        lax.fori_loop(1, i, full_pair, 0)

    if n_tiles > 1:
        next_tile(1)

        def q_tile(i, carry):
            full_pairs(i)
            step_pair(i, 2 * i, masked=True)
            next_tile(i + 1)
            return carry

        lax.fori_loop(1, n_tiles - 1, q_tile, 0)
        full_pairs(n_tiles - 1)
        step_pair(n_tiles - 1, 2 * (n_tiles - 1), masked=True)

    last = 2 * (n_tiles - 1)
    accumulate(last, 0)
    softmax(1, q0=tk)
    accumulate(last + 1, 1, q0=tk)
    finalize(n_tiles - 1)


def diff_attn(qt, k, vt, lambda_vecs, g_subln, *, batch, tq, tk, lam_init):
    T = k.shape[0]
    S = T // batch
    assert S % tq == 0 and tq == 2 * tk
    nq, nk = S // tq, S // tk
    stat = pltpu.VMEM((2, 1, tq), F32)
    return pl.pallas_call(
        functools.partial(_diff_attn_kernel, tq=tq, tk=tk, lam_init=lam_init),
        grid=(batch, DIFF_HEADS),
        in_specs=[
            pl.BlockSpec((4, DIFF_D), lambda b, h: (0, 0)),
            pl.BlockSpec((1, DIFF_V), lambda b, h: (0, 0)),
            pl.BlockSpec((1, nq, LANES, tq), lambda b, h: (h, b, 0, 0)),
            pl.BlockSpec((S, LANES), lambda b, h: (b, h)),
            pl.BlockSpec((1, nk, V_ROWS, tk), lambda b, h: (h, b, 0, 0)),
        ],
        out_specs=pl.BlockSpec((S, DIFF_V), lambda b, h: (b, h)),
        out_shape=jax.ShapeDtypeStruct((T, DIFF_VW), BF16),
        scratch_shapes=[
            stat,
            pltpu.VMEM((2, V_ROWS, tq), F32),
            pltpu.VMEM((2, tk, tq), F32),
            pltpu.VMEM((2, tk, tq), F32),
            pltpu.VMEM((2, tk, tq), BF16),
            pltpu.VMEM((2, tk, tq), BF16),
            stat, stat,
            stat, stat,
        ],
        compiler_params=_compiler_params(("parallel", "parallel")),
        name="diff_attn",
    )(lambda_vecs, g_subln.reshape(1, DIFF_V), qt, k, vt)


def _mem_prep_kernel(kv_ref, g_ref, mk_ref, mv_ref):
    g = g_ref[...]
    for h in range(X_HEADS):
        sl = slice(h * X_HEAD_DIM, (h + 1) * X_HEAD_DIM)
        kh = _rms_rows(kv_ref[:, sl], g) * (X_HEAD_DIM ** -0.5)
        mk_ref[:, sl] = kh.astype(mk_ref.dtype)
    mv_ref[...] = kv_ref[:, X_WIDTH:].astype(mv_ref.dtype)


def mem_prep(kv, g_k_x):
    R = kv.shape[0]
    out = jax.ShapeDtypeStruct((R, X_WIDTH), BF16)
    return pl.pallas_call(
        _mem_prep_kernel,
        grid=(1,),
        in_specs=[
            pl.BlockSpec((R, 2 * X_WIDTH), lambda i: (0, 0)),
            pl.BlockSpec((1, X_HEAD_DIM), lambda i: (0, 0)),
        ],
        out_specs=[pl.BlockSpec((R, X_WIDTH), lambda i: (0, 0))] * 2,
        out_shape=[out, out],
        compiler_params=_compiler_params(("arbitrary",)),
        name="mem_prep",
    )(kv, g_k_x.reshape(1, X_HEAD_DIM))


MERGE_CHUNK = 512
GATE_BLOCK = 1024
assert GATE_BLOCK % MERGE_CHUNK == 0


def _merge_kernel(*refs, tiles_per_seq, n_gate_blocks):
    gate_refs = refs[:n_gate_blocks]
    (cb_ref, cc_ref, cx_ref, cch_ref, cxh_ref, xq_ref, ob_ref, x_ref,
     convw_ref, gqx_ref, gn_ref, mk_ref, mv_ref, wa_ref, wb_ref, wc_ref, wo_ref,
     o_ref, hn_ref, ua_ref, oc_ref, mg_ref, yb_ref) = refs[n_gate_blocks:]
    tm = x_ref.shape[0]
    d_model = x_ref.shape[1]
    first = pl.program_id(0) % tiles_per_seq == 0

    def gate(branch, c):
        blk, off = divmod(branch * d_model + c * MERGE_CHUNK, GATE_BLOCK)
        return jax.nn.sigmoid(gate_refs[blk][:, off:off + MERGE_CHUNK].astype(F32))

    ob = ob_ref[...]
    n_chunks = d_model // MERGE_CHUNK

    def diff_branch(c):
        sl = slice(c * MERGE_CHUNK, (c + 1) * MERGE_CHUNK)
        yb_ref[:, sl] = gate(1, c) * jnp.dot(ob, wb_ref[:, sl], preferred_element_type=F32)

    diff_branch(0)

    gq = gqx_ref[...]
    scores = []
    for h in range(X_HEADS):
        sl = slice(h * X_HEAD_DIM, (h + 1) * X_HEAD_DIM)
        qn = _rms_rows(xq_ref[:, sl].astype(F32), gq).astype(BF16)
        scores.append(lax.dot_general(qn, mk_ref[:, sl], (((1,), (1,)), ((), ())),
                                      preferred_element_type=F32))
    diff_branch(1)
    for h, s in enumerate(scores):
        sl = slice(h * X_HEAD_DIM, (h + 1) * X_HEAD_DIM)
        s = s - jnp.max(s, axis=-1, keepdims=True)
        p = jnp.exp(s)
        p = p / jnp.sum(p, axis=-1, keepdims=True)
        oc_ref[:, sl] = jnp.dot(p.astype(BF16), mv_ref[:, sl], preferred_element_type=F32).astype(oc_ref.dtype)
    for c in range(2, n_chunks):
        diff_branch(c)

    z = cc_ref[...].astype(F32) * cx_ref[...].astype(F32)
    zp = cch_ref[...].astype(F32) * cxh_ref[...].astype(F32)
    zp = jnp.where(first, 0.0, zp)
    w = convw_ref[...]
    row8 = lax.broadcasted_iota(jnp.int32, (SUBLANES, 1), 0)
    y = z * w[CONV_K - 1:CONV_K, :]
    for back in range(1, CONV_K):
        zs = pltpu.roll(z, back, axis=0)
        ps = pltpu.roll(zp, back, axis=0)
        head = jnp.where(row8 < back, ps, zs[:SUBLANES])
        zs = jnp.concatenate([head, zs[SUBLANES:]], axis=0)
        y = y + zs * w[CONV_K - 1 - back:CONV_K - back, :]
    ua_ref[...] = (cb_ref[...].astype(F32) * y).astype(ua_ref.dtype)

    ua = ua_ref[...]
    oc = oc_ref[...]
    for c in range(n_chunks):
        sl = slice(c * MERGE_CHUNK, (c + 1) * MERGE_CHUNK)
        merged = yb_ref[:, sl]
        for br, act, w_ref in ((0, ua, wa_ref), (2, oc, wc_ref)):
            merged = merged + gate(br, c) * jnp.dot(act, w_ref[:, sl], preferred_element_type=F32)
        mg_ref[:, sl] = merged.astype(mg_ref.dtype)

    y = x_ref[...] + jnp.dot(mg_ref[...], wo_ref[...], preferred_element_type=F32)
    o_ref[...] = y
    hn_ref[...] = _rms_rows(y, gn_ref[...]).astype(hn_ref.dtype)


def merge(proj, ob, x, conv_w, g_q_x, g_next, mk, mv, w_conv_out, w_diff_out, w_x_out, w_o,
          *, layer, batch, cols, tm):
    T, D = x.shape
    S = T // batch
    M = mk.shape[0] // batch
    assert S % tm == 0 and tm % SUBLANES == 0
    tiles_per_seq = S // tm
    halo_blocks = tm // SUBLANES

    def col_spec(name, width):
        assert cols[name] % width == 0
        return pl.BlockSpec((tm, width), lambda i: (i, cols[name] // width))

    def halo_spec(name):
        return pl.BlockSpec((SUBLANES, CONV_WIDTH),
                            lambda i: (jnp.maximum(i * halo_blocks - 1, 0), cols[name] // CONV_WIDTH))

    def resident(shape):
        return pl.BlockSpec((None,) + shape, lambda i: (layer, 0, 0), pipeline_mode=pl.Buffered(1))

    assert cols["gates"] % GATE_BLOCK == 0 and (N_BRANCH * D) % GATE_BLOCK == 0 and D % MERGE_CHUNK == 0
    n_gate_blocks = N_BRANCH * D // GATE_BLOCK
    gate_specs = [pl.BlockSpec((tm, GATE_BLOCK), lambda i, g=g: (i, cols["gates"] // GATE_BLOCK + g))
                  for g in range(n_gate_blocks)]

    return pl.pallas_call(
        functools.partial(_merge_kernel, tiles_per_seq=tiles_per_seq, n_gate_blocks=n_gate_blocks),
        grid=(T // tm,),
        in_specs=gate_specs + [
            col_spec("cb", CONV_WIDTH),
            col_spec("cc", CONV_WIDTH),
            col_spec("cx", CONV_WIDTH),
            halo_spec("cc"),
            halo_spec("cx"),
            col_spec("xq", X_WIDTH),
            pl.BlockSpec((tm, DIFF_VW), lambda i: (i, 0)),
            pl.BlockSpec((tm, D), lambda i: (i, 0)),
            pl.BlockSpec((CONV_K, CONV_WIDTH), lambda i: (0, 0)),
            pl.BlockSpec((1, X_HEAD_DIM), lambda i: (0, 0)),
            pl.BlockSpec((1, D), lambda i: (0, 0)),
            pl.BlockSpec((M, X_WIDTH), lambda i: (i // tiles_per_seq, 0)),
            pl.BlockSpec((M, X_WIDTH), lambda i: (i // tiles_per_seq, 0)),
            resident((CONV_WIDTH, D)),
            resident((DIFF_VW, D)),
            resident((X_WIDTH, D)),
            resident((D, D)),
        ],
        out_specs=[pl.BlockSpec((tm, D), lambda i: (i, 0))] * 2,
        out_shape=[jax.ShapeDtypeStruct((T, D), F32), jax.ShapeDtypeStruct((T, D), BF16)],
        scratch_shapes=[
            pltpu.VMEM((tm, CONV_WIDTH), BF16),
            pltpu.VMEM((tm, X_WIDTH), BF16),
            pltpu.VMEM((tm, D), BF16),
            pltpu.VMEM((tm, D), F32),
        ],
        compiler_params=_compiler_params(("parallel",)),
        name="merge",
    )(*([proj] * (n_gate_blocks + 6)), ob, x, conv_w, g_q_x.reshape(1, X_HEAD_DIM), g_next.reshape(1, D),
      mk, mv, w_conv_out, w_diff_out, w_x_out, w_o)


def _ffn_up_kernel(h_ref, wg_ref, wu_ref, o_ref, wgb_ref, wub_ref):
    @pl.when(pl.program_id(1) == 0)
    def _():
        wgb_ref[...] = wg_ref[...].astype(wgb_ref.dtype)
        wub_ref[...] = wu_ref[...].astype(wub_ref.dtype)

    for r0 in range(0, h_ref.shape[0], MXU_ROW_CHUNK):
        h = h_ref[r0:r0 + MXU_ROW_CHUNK, :]
        a = jnp.dot(h, wgb_ref[...], preferred_element_type=F32)
        b = jnp.dot(h, wub_ref[...], preferred_element_type=F32)
        o_ref[r0:r0 + MXU_ROW_CHUNK, :] = (a * jax.nn.sigmoid(a) * b).astype(o_ref.dtype)


def ffn_up(h, w_gate_up, *, layer, tm, tf):
    T, D = h.shape
    d_ff = w_gate_up.shape[2] // 2
    assert T % tm == 0 and d_ff % tf == 0
    nf = d_ff // tf
    return pl.pallas_call(
        _ffn_up_kernel,
        grid=(nf, T // tm),
        in_specs=[
            pl.BlockSpec((tm, D), lambda f, i: (i, 0)),
            pl.BlockSpec((None, D, tf), lambda f, i: (layer, 0, f)),
            pl.BlockSpec((None, D, tf), lambda f, i: (layer, 0, nf + f)),
        ],
        out_specs=pl.BlockSpec((tm, tf), lambda f, i: (i, f)),
        out_shape=jax.ShapeDtypeStruct((T, d_ff), BF16),
        scratch_shapes=[pltpu.VMEM((D, tf), BF16), pltpu.VMEM((D, tf), BF16)],
        compiler_params=_compiler_params(("parallel", "arbitrary")),
        name="ffn_up",
    )(h, w_gate_up, w_gate_up)


def _ffn_down_kernel(a_ref, x_ref, wd_ref, *rest, normed_out):
    if normed_out:
        gn_ref, o_ref, hn_ref = rest
    else:
        (o_ref,) = rest
    y = x_ref[...] + jnp.dot(a_ref[...], wd_ref[...], preferred_element_type=F32)
    o_ref[...] = y
    if normed_out:
        hn_ref[...] = _rms_rows(y, gn_ref[...]).astype(hn_ref.dtype)


def ffn_down(act, x, w_down, *, layer, tm, g_next=None):
    T, D = x.shape
    d_ff = act.shape[1]
    assert T % tm == 0
    normed_out = g_next is not None
    row_spec = pl.BlockSpec((tm, D), lambda i: (i, 0))
    in_specs = [
        pl.BlockSpec((tm, d_ff), lambda i: (i, 0)),
        row_spec,
        pl.BlockSpec((None, d_ff, D), lambda i: (layer, 0, 0), pipeline_mode=pl.Buffered(1)),
    ]
    args = [act, x, w_down]
    out_specs, out_shape = row_spec, jax.ShapeDtypeStruct((T, D), F32)
    if normed_out:
        in_specs.append(pl.BlockSpec((1, D), lambda i: (0, 0)))
        args.append(g_next.reshape(1, D))
        out_specs = [row_spec, row_spec]
        out_shape = [out_shape, jax.ShapeDtypeStruct((T, D), BF16)]
    return pl.pallas_call(
        functools.partial(_ffn_down_kernel, normed_out=normed_out),
        grid=(T // tm,),
        in_specs=in_specs,
        out_specs=out_specs,
        out_shape=out_shape,
        compiler_params=_compiler_params(("parallel",)),
        name="ffn_down",
    )(*args)


def _tile(n, want):
    t = min(n, want)
    while n % t:
        t -= SUBLANES
    return t


def kernel(x, mem, positions, g_mix, w_in, conv_w, w_conv_out, g_q_diff, g_k_diff, lambda_vecs,
           g_subln, w_diff_out, g_mem, w_mem_kv, g_q_x, g_k_x, w_x_out, w_o, g_ffn, w_gate_up, w_down):
    B, S, D = x.shape
    M = mem.shape[1]
    T = B * S
    depth = w_in.shape[0]
    d_ff = w_down.shape[1]

    n_conv = 3 * CONV_WIDTH
    n_diff = 2 * DIFF_QK + DIFF_VW
    cols = {
        "cb": 0,
        "cc": CONV_WIDTH,
        "cx": 2 * CONV_WIDTH,
        "dq": n_conv,
        "dk": n_conv + DIFF_QK,
        "dv": n_conv + 2 * DIFF_QK,
        "xq": n_conv + n_diff,
        "gates": n_conv + n_diff + X_WIDTH,
    }

    xf = x.reshape(T, D)
    memf = mem.reshape(B * M, D)

    tm_proj = _tile(T, 2048)
    tq = _tile(S, 1024)
    tk = tq // 2
    pos = positions.reshape(T // tq, 1, tq)
    tm_merge = _tile(S, 256)
    tm_up = _tile(T, 2048)
    tf = _tile(d_ff, 512)
    tm_down = _tile(T, 256)

    w_mem_kv, w_conv_out, w_diff_out, w_x_out, w_o, w_down = (
        w.astype(BF16) for w in (w_mem_kv, w_conv_out, w_diff_out, w_x_out, w_o, w_down))

    h = rms_norm_rows(xf, g_mix[0], tm=_tile(T, 1024))
    for l in range(depth):
        lam_init = 0.8 - 0.6 * math.exp(-0.3 * l)

        proj = in_proj(h, w_in, layer=l, tm=tm_proj, tn=1024)
        qt, k, vt = qk_prep(proj, pos, g_q_diff[l], g_k_diff[l],
                            q_col=cols["dq"], k_col=cols["dk"], v_col=cols["dv"], tm=tq, tk=tk)
        ob = diff_attn(qt, k, vt, lambda_vecs[l], g_subln[l], batch=B, tq=tq, tk=tk, lam_init=lam_init)

        kv = norm_matmul(memf, g_mem[l], w_mem_kv, layer=l, tm=B * M, tn=1024, out_dtype=F32)
        mk, mv = mem_prep(kv, g_k_x[l])

        xf, h2 = merge(proj, ob, xf, conv_w[l], g_q_x[l], g_ffn[l], mk, mv,
                       w_conv_out, w_diff_out, w_x_out, w_o, layer=l, batch=B, cols=cols, tm=tm_merge)
        act = ffn_up(h2, w_gate_up, layer=l, tm=tm_up, tf=tf)
        if l + 1 < depth:
            xf, h = ffn_down(act, xf, w_down, layer=l, tm=tm_down, g_next=g_mix[l + 1])
        else:
            xf = ffn_down(act, xf, w_down, layer=l, tm=tm_down)

    return xf.reshape(B, S, D)
```

```python
import functools
import math

import jax
import jax.numpy as jnp
from jax import lax
from jax.experimental import pallas as pl
from jax.experimental.pallas import tpu as pltpu

EPS = 1e-6
CONV_WIDTH = 1024
CONV_K = 3
DIFF_HEADS = 8
DIFF_D = 64
DIFF_V = 2 * DIFF_D
DIFF_QK = DIFF_HEADS * 2 * DIFF_D
DIFF_VW = DIFF_HEADS * DIFF_V
X_HEADS = 4
X_HEAD_DIM = 256
X_WIDTH = X_HEADS * X_HEAD_DIM
N_BRANCH = 3
ROPE_THETA = 500000.0
ROT_FRAC = 4
ROT = DIFF_D // ROT_FRAC
ROT_HALF = ROT // 2

LANES = 128
SUBLANES = 8
BF16_ROWS = 16
V_ROWS = DIFF_V + BF16_ROWS
Q_SCALE = DIFF_D ** -0.5 * math.log2(math.e)
VMEM_LIMIT_BYTES = 56 * 1024 * 1024
MXU_ROW_CHUNK = 512

F32 = jnp.float32
BF16 = jnp.bfloat16


def _compiler_params(semantics):
    return pltpu.CompilerParams(dimension_semantics=semantics, vmem_limit_bytes=VMEM_LIMIT_BYTES)


def _rms_rows(t, g):
    ms = jnp.mean(t * t, axis=-1, keepdims=True)
    return t * lax.rsqrt(ms + EPS) * g


def _norm_matmul_kernel(x_ref, g_ref, w_ref, o_ref, h_ref):
    @pl.when(pl.program_id(1) == 0)
    def _():
        h_ref[...] = _rms_rows(x_ref[...], g_ref[...]).astype(h_ref.dtype)

    o_ref[...] = jnp.dot(h_ref[...], w_ref[...], preferred_element_type=F32).astype(o_ref.dtype)


def norm_matmul(x, g, w, *, layer, tm, tn, out_dtype):
    T, D = x.shape
    N = w.shape[2]
    assert T % tm == 0 and N % tn == 0
    return pl.pallas_call(
        _norm_matmul_kernel,
        grid=(T // tm, N // tn),
        in_specs=[
            pl.BlockSpec((tm, D), lambda i, j: (i, 0)),
            pl.BlockSpec((1, D), lambda i, j: (0, 0)),
            pl.BlockSpec((None, D, tn), lambda i, j: (layer, 0, j)),
        ],
        out_specs=pl.BlockSpec((tm, tn), lambda i, j: (i, j)),
        out_shape=jax.ShapeDtypeStruct((T, N), out_dtype),
        scratch_shapes=[pltpu.VMEM((tm, D), BF16)],
        compiler_params=_compiler_params(("parallel", "arbitrary")),
        name="norm_matmul",
    )(x, g.reshape(1, D), w)


def _rms_norm_kernel(x_ref, g_ref, o_ref):
    o_ref[...] = _rms_rows(x_ref[...], g_ref[...]).astype(o_ref.dtype)


def rms_norm_rows(x, g, *, tm):
    T, D = x.shape
    assert T % tm == 0
    return pl.pallas_call(
        _rms_norm_kernel,
        grid=(T // tm,),
        in_specs=[pl.BlockSpec((tm, D), lambda i: (i, 0)), pl.BlockSpec((1, D), lambda i: (0, 0))],
        out_specs=pl.BlockSpec((tm, D), lambda i: (i, 0)),
        out_shape=jax.ShapeDtypeStruct((T, D), BF16),
        compiler_params=_compiler_params(("parallel",)),
        name="rms_norm",
    )(x, g.reshape(1, D))


def _in_proj_kernel(h_ref, w_ref, o_ref, wb_ref):
    @pl.when(pl.program_id(1) == 0)
    def _():
        wb_ref[...] = w_ref[...].astype(wb_ref.dtype)

    for r0 in range(0, h_ref.shape[0], MXU_ROW_CHUNK):
        rows = slice(r0, r0 + MXU_ROW_CHUNK)
        o_ref[rows, :] = jnp.dot(h_ref[rows, :], wb_ref[...], preferred_element_type=F32).astype(o_ref.dtype)


def in_proj(h, w, *, layer, tm, tn):
    T, D = h.shape
    N = w.shape[2]
    assert T % tm == 0 and N % tn == 0
    return pl.pallas_call(
        _in_proj_kernel,
        grid=(N // tn, T // tm),
        in_specs=[
            pl.BlockSpec((tm, D), lambda j, i: (i, 0)),
            pl.BlockSpec((None, D, tn), lambda j, i: (layer, 0, j)),
        ],
        out_specs=pl.BlockSpec((tm, tn), lambda j, i: (i, j)),
        out_shape=jax.ShapeDtypeStruct((T, N), BF16),
        scratch_shapes=[pltpu.VMEM((D, tn), BF16)],
        compiler_params=_compiler_params(("parallel", "arbitrary")),
        name="in_proj",
    )(h, w)


def _group_mean_sq(t, gsum):
    sq = t * t
    hi = sq.astype(BF16)
    lo = (sq - hi.astype(F32)).astype(BF16)
    return jnp.dot(hi, gsum, preferred_element_type=F32) + jnp.dot(lo, gsum, preferred_element_type=F32)


PREP_ROWS = LANES


def _qk_prep_kernel(pos_ref, invf_ref, expand_ref, keep_ref, gq_ref, gk_ref, q_ref, k_ref, v_ref,
                    qt_ref, ko_ref, vt_ref, *, tk):
    tm = q_ref.shape[0]
    row = lax.broadcasted_iota(jnp.int32, (LANES, LANES), 0)
    col = lax.broadcasted_iota(jnp.int32, (LANES, LANES), 1)
    gsum = jnp.where((row < DIFF_D) == (col < DIFF_D), 1.0 / DIFF_D, 0.0).astype(BF16)
    trig_row = lax.broadcasted_iota(jnp.int32, (2 * ROT_HALF, 1), 0)
    trig_pad = jnp.zeros((LANES - 2 * ROT_HALF, PREP_ROWS), F32)
    pad_row = lax.broadcasted_iota(jnp.int32, (V_ROWS - DIFF_V, PREP_ROWS), 0)
    ones_row = jnp.where(pad_row == 0, 1.0, 0.0).astype(vt_ref.dtype)
    gq = gq_ref[...]
    gk = gk_ref[...]
    expand = expand_ref[...]

    for r0 in range(0, tm, PREP_ROWS):
        ang = invf_ref[...] * pos_ref[:, r0:r0 + PREP_ROWS].astype(F32)
        trig_t = jnp.where(trig_row < ROT_HALF, jnp.cos(ang), jnp.sin(ang))
        cos_t, sin_t = trig_t[:ROT_HALF], trig_t[ROT_HALF:]
        trig = jnp.concatenate([trig_t, trig_pad], axis=0).T
        hi = trig.astype(BF16)
        lo = (trig - hi.astype(F32)).astype(BF16)
        tab = (jnp.dot(hi, expand, preferred_element_type=F32)
               + jnp.dot(lo, expand, preferred_element_type=F32))
        c_keep = tab[:, :LANES] + keep_ref[...]
        s_from_hi = tab[:, LANES:2 * LANES]
        s_from_lo = tab[:, 2 * LANES:]

        def norm(src_ref, g, c):
            t = src_ref[r0:r0 + PREP_ROWS, c * LANES:(c + 1) * LANES].astype(F32)
            return t * lax.rsqrt(_group_mean_sq(t, gsum) + EPS) * g

        def rot_lanes(tn):
            up = pltpu.roll(tn, LANES - ROT_HALF, axis=1)
            dn = pltpu.roll(tn, ROT_HALF, axis=1)
            return tn * c_keep + up * s_from_hi + dn * s_from_lo

        def rot_rows(tt):
            parts = []
            for base in range(0, LANES, DIFF_D):
                x1 = tt[base:base + ROT_HALF]
                x2 = tt[base + ROT_HALF:base + ROT]
                parts += [x1 * cos_t - x2 * sin_t, x2 * cos_t + x1 * sin_t, tt[base + ROT:base + DIFF_D]]
            return jnp.concatenate(parts, axis=0)

        u, off = divmod(r0, tk)
        for h in range(DIFF_HEADS):
            sl = slice(h * LANES, (h + 1) * LANES)
            qt = rot_rows(norm(q_ref, gq, h).T)
            qt_ref[h, 0, :, r0:r0 + PREP_ROWS] = qt.astype(qt_ref.dtype)
            ko_ref[r0:r0 + PREP_ROWS, sl] = rot_lanes(norm(k_ref, gk, h)).astype(ko_ref.dtype)
            vt_ref[h, u, :DIFF_V, off:off + PREP_ROWS] = v_ref[r0:r0 + PREP_ROWS, sl].T
            vt_ref[h, u, DIFF_V:, off:off + PREP_ROWS] = ones_row


def _rotary_tables():
    inv_freq = ROPE_THETA ** (-jnp.arange(ROT_HALF, dtype=F32) / ROT_HALF)
    invf = jnp.concatenate([inv_freq, inv_freq]).reshape(2 * ROT_HALF, 1)
    src = jnp.arange(LANES)[:, None]
    d = jnp.arange(LANES)[None, :] % DIFF_D
    cos_part = jnp.where((d < ROT) & (src == d % ROT_HALF), 1.0, 0.0)
    hi_part = jnp.where((d < ROT_HALF) & (src == ROT_HALF + d), -1.0, 0.0)
    lo_part = jnp.where((d >= ROT_HALF) & (d < ROT) & (src == d), 1.0, 0.0)
    expand = jnp.concatenate([cos_part, hi_part, lo_part], axis=1).astype(BF16)
    keep = jnp.where(d < ROT, 0.0, 1.0).astype(F32)
    return invf, expand, keep


def qk_prep(proj, pos, g_q, g_k, *, q_col, k_col, v_col, tm, tk):
    T = proj.shape[0]
    assert T % tm == 0 and tm % tk == 0 and tk % PREP_ROWS == 0
    assert q_col % DIFF_QK == 0 and k_col % DIFF_QK == 0 and v_col % DIFF_VW == 0 and DIFF_V == LANES
    gq = (jnp.tile(g_q, LANES // DIFF_D) * Q_SCALE).reshape(1, LANES)
    gk = jnp.tile(g_k, LANES // DIFF_D).reshape(1, LANES)
    invf, expand, keep = _rotary_tables()
    return pl.pallas_call(
        functools.partial(_qk_prep_kernel, tk=tk),
        grid=(T // tm,),
        in_specs=[
            pl.BlockSpec((None, 1, tm), lambda i: (i, 0, 0)),
            pl.BlockSpec((2 * ROT_HALF, 1), lambda i: (0, 0)),
            pl.BlockSpec((LANES, 3 * LANES), lambda i: (0, 0)),
            pl.BlockSpec((1, LANES), lambda i: (0, 0)),
            pl.BlockSpec((1, LANES), lambda i: (0, 0)),
            pl.BlockSpec((1, LANES), lambda i: (0, 0)),
            pl.BlockSpec((tm, DIFF_QK), lambda i: (i, q_col // DIFF_QK)),
            pl.BlockSpec((tm, DIFF_QK), lambda i: (i, k_col // DIFF_QK)),
            pl.BlockSpec((tm, DIFF_VW), lambda i: (i, v_col // DIFF_VW)),
        ],
        out_specs=[
            pl.BlockSpec((DIFF_HEADS, 1, LANES, tm), lambda i: (0, i, 0, 0)),
            pl.BlockSpec((tm, DIFF_QK), lambda i: (i, 0)),
            pl.BlockSpec((DIFF_HEADS, tm // tk, V_ROWS, tk), lambda i: (0, i, 0, 0)),
        ],
        out_shape=[
            jax.ShapeDtypeStruct((DIFF_HEADS, T // tm, LANES, tm), BF16),
            jax.ShapeDtypeStruct((T, DIFF_QK), BF16),
            jax.ShapeDtypeStruct((DIFF_HEADS, T // tk, V_ROWS, tk), BF16),
        ],
        compiler_params=_compiler_params(("parallel",)),
        name="qk_prep",
    )(pos, invf, expand, keep, gq, gk, proj, proj, proj)


def _diff_attn_kernel(lam_ref, g_ref, qt_ref, k_ref, vt_ref, o_ref, m_ref, acc_ref,
                      s0_ref, s1_ref, p0_ref, p1_ref, a0_ref, a1_ref, bm0_ref, bm1_ref,
                      *, tq, tk, lam_init):
    assert tq == 2 * tk
    s_bufs, p_bufs, a_bufs, bm_bufs = (s0_ref, s1_ref), (p0_ref, p1_ref), (a0_ref, a1_ref), (bm0_ref, bm1_ref)
    n_tiles = qt_ref.shape[1]

    lv = lam_ref[...]
    e1 = jnp.exp(jnp.sum(lv[0:1, :] * lv[1:2, :], axis=-1, keepdims=True))
    e2 = jnp.exp(jnp.sum(lv[2:3, :] * lv[3:4, :], axis=-1, keepdims=True))
    lam = e1 - e2 + lam_init

    acc_ref[...] = jnp.zeros(acc_ref.shape, F32)

    def scores(i, u, slot, masked, q0=0):
        k = k_ref[pl.ds(pl.multiple_of(u * tk, tk), tk), :]
        qt = qt_ref[0, i, :, q0:]
        nq_cols = tq - q0
        zeros = jnp.zeros((DIFF_D, nq_cols), qt.dtype)
        q_comp = (jnp.concatenate([qt[:DIFF_D], zeros], axis=0),
                  jnp.concatenate([zeros, qt[DIFF_D:]], axis=0))
        if masked:
            kpos = u * tk + lax.broadcasted_iota(jnp.int32, (tk, nq_cols), 0)
            qpos = i * tq + q0 + lax.broadcasted_iota(jnp.int32, (tk, nq_cols), 1)
            causal = kpos <= qpos
        for c in range(2):
            for j0 in range(0, nq_cols, tk):
                cols = slice(j0, j0 + tk)
                s = jnp.dot(k, q_comp[c][:, cols], preferred_element_type=F32)
                if masked:
                    s = jnp.where(causal[:, cols], s, -jnp.inf)
                s_bufs[slot][c, :, q0 + j0:q0 + j0 + tk] = s
                bm_bufs[slot][c, :, q0 + j0:q0 + j0 + tk] = jnp.max(s, axis=0, keepdims=True)

    def softmax(slot, first=False, q0=0):
        for c in range(2):
            if first:
                m_new = bm_bufs[slot][c]
                a_bufs[slot][c] = jnp.zeros_like(m_new)
            else:
                m_prev = m_ref[c, :, q0:]
                m_new = jnp.maximum(m_prev, bm_bufs[slot][c, :, q0:])
                a_bufs[slot][c, :, q0:] = jnp.exp2(m_prev - m_new)
            m_ref[c, :, q0:] = m_new
            p_bufs[slot][c, :, q0:] = jnp.exp2(s_bufs[slot][c, :, q0:] - m_new).astype(p_bufs[slot].dtype)

    def accumulate(u, slot, q0=0):
        vt = vt_ref[0, u]
        for c in range(2):
            pv = jnp.dot(vt, p_bufs[slot][c, :, q0:], preferred_element_type=F32)
            acc_ref[c, :, q0:] = a_bufs[slot][c, :, q0:] * acc_ref[c, :, q0:] + pv

    def finalize(i):
        o = (acc_ref[0, :DIFF_V] / acc_ref[0, DIFF_V:DIFF_V + 1]
             - lam * (acc_ref[1, :DIFF_V] / acc_ref[1, DIFF_V:DIFF_V + 1]))
        o = o * lax.rsqrt(jnp.mean(o * o, axis=0, keepdims=True) + EPS)
        rows = pl.ds(pl.multiple_of(i * tq, tq), tq)
        o_ref[rows, :] = (o.T * g_ref[...] * (1.0 - lam_init)).astype(o_ref.dtype)

    def step_pair(i, u, masked):
        scores(i, u, 0, masked)
        accumulate(u - 2, 0)
        softmax(1)
        scores(i, u + 1, 1, masked, q0=tk if masked else 0)
        accumulate(u - 1, 1)
        softmax(0)

    scores(0, 0, 0, masked=True)
    scores(0, 1, 1, masked=True, q0=tk)
    softmax(0, first=True)

    def next_tile(i):
        last = 2 * (i - 1)
        scores(i, 0, 0, masked=False)
        accumulate(last, 0)
        softmax(1, q0=tk)
        scores(i, 1, 1, masked=False)
        accumulate(last + 1, 1, q0=tk)
        finalize(i - 1)
        softmax(0, first=True)

    def full_pairs(i):
        def full_pair(r, c):
            step_pair(i, 2 * r, masked=False)
            return c

        lax.fori_loop(1, i, full_pair, 0)

    if n_tiles > 1:
        next_tile(1)

        def q_tile(i, carry):
            full_pairs(i)
            step_pair(i, 2 * i, masked=True)
            next_tile(i + 1)
            return carry

        lax.fori_loop(1, n_tiles - 1, q_tile, 0)
        full_pairs(n_tiles - 1)
        step_pair(n_tiles - 1, 2 * (n_tiles - 1), masked=True)

    last = 2 * (n_tiles - 1)
    accumulate(last, 0)
    softmax(1, q0=tk)
    accumulate(last + 1, 1, q0=tk)
    finalize(n_tiles - 1)


def diff_attn(qt, k, vt, lambda_vecs, g_subln, *, batch, tq, tk, lam_init):
    T = k.shape[0]
    S = T // batch
    assert S % tq == 0 and tq == 2 * tk
    nq, nk = S // tq, S // tk
    stat = pltpu.VMEM((2, 1, tq), F32)
    return pl.pallas_call(
        functools.partial(_diff_attn_kernel, tq=tq, tk=tk, lam_init=lam_init),
        grid=(batch, DIFF_HEADS),
        in_specs=[
            pl.BlockSpec((4, DIFF_D), lambda b, h: (0, 0)),
            pl.BlockSpec((1, DIFF_V), lambda b, h: (0, 0)),
            pl.BlockSpec((1, nq, LANES, tq), lambda b, h: (h, b, 0, 0)),
            pl.BlockSpec((S, LANES), lambda b, h: (b, h)),
            pl.BlockSpec((1, nk, V_ROWS, tk), lambda b, h: (h, b, 0, 0)),
        ],
        out_specs=pl.BlockSpec((S, DIFF_V), lambda b, h: (b, h)),
        out_shape=jax.ShapeDtypeStruct((T, DIFF_VW), BF16),
        scratch_shapes=[
            stat,
            pltpu.VMEM((2, V_ROWS, tq), F32),
            pltpu.VMEM((2, tk, tq), F32),
            pltpu.VMEM((2, tk, tq), F32),
            pltpu.VMEM((2, tk, tq), BF16),
            pltpu.VMEM((2, tk, tq), BF16),
            stat, stat,
            stat, stat,
        ],
        compiler_params=_compiler_params(("parallel", "parallel")),
        name="diff_attn",
    )(lambda_vecs, g_subln.reshape(1, DIFF_V), qt, k, vt)


def _mem_prep_kernel(kv_ref, g_ref, mk_ref, mv_ref):
    g = g_ref[...]
    for h in range(X_HEADS):
        sl = slice(h * X_HEAD_DIM, (h + 1) * X_HEAD_DIM)
        kh = _rms_rows(kv_ref[:, sl], g) * (X_HEAD_DIM ** -0.5)
        mk_ref[:, sl] = kh.astype(mk_ref.dtype)
    mv_ref[...] = kv_ref[:, X_WIDTH:].astype(mv_ref.dtype)


def mem_prep(kv, g_k_x):
    R = kv.shape[0]
    out = jax.ShapeDtypeStruct((R, X_WIDTH), BF16)
    return pl.pallas_call(
        _mem_prep_kernel,
        grid=(1,),
        in_specs=[
            pl.BlockSpec((R, 2 * X_WIDTH), lambda i: (0, 0)),
            pl.BlockSpec((1, X_HEAD_DIM), lambda i: (0, 0)),
        ],
        out_specs=[pl.BlockSpec((R, X_WIDTH), lambda i: (0, 0))] * 2,
        out_shape=[out, out],
        compiler_params=_compiler_params(("arbitrary",)),
        name="mem_prep",
    )(kv, g_k_x.reshape(1, X_HEAD_DIM))


MERGE_CHUNK = 512
GATE_BLOCK = 1024
assert GATE_BLOCK % MERGE_CHUNK == 0


def _merge_kernel(*refs, tiles_per_seq, n_gate_blocks):
    gate_refs = refs[:n_gate_blocks]
    (cb_ref, cc_ref, cx_ref, cch_ref, cxh_ref, xq_ref, ob_ref, x_ref,
     convw_ref, gqx_ref, gn_ref, mk_ref, mv_ref, wa_ref, wb_ref, wc_ref, wo_ref,
     o_ref, hn_ref, ua_ref, oc_ref, mg_ref, yb_ref) = refs[n_gate_blocks:]
    tm = x_ref.shape[0]
    d_model = x_ref.shape[1]
    first = pl.program_id(0) % tiles_per_seq == 0

    def gate(branch, c):
        blk, off = divmod(branch * d_model + c * MERGE_CHUNK, GATE_BLOCK)
        return jax.nn.sigmoid(gate_refs[blk][:, off:off + MERGE_CHUNK].astype(F32))

    ob = ob_ref[...]
    n_chunks = d_model // MERGE_CHUNK

    def diff_branch(c):
        sl = slice(c * MERGE_CHUNK, (c + 1) * MERGE_CHUNK)
        yb_ref[:, sl] = gate(1, c) * jnp.dot(ob, wb_ref[:, sl], preferred_element_type=F32)

    diff_branch(0)

    gq = gqx_ref[...]
    scores = []
    for h in range(X_HEADS):
        sl = slice(h * X_HEAD_DIM, (h + 1) * X_HEAD_DIM)
        qn = _rms_rows(xq_ref[:, sl].astype(F32), gq).astype(BF16)
        scores.append(lax.dot_general(qn, mk_ref[:, sl], (((1,), (1,)), ((), ())),
                                      preferred_element_type=F32))
    diff_branch(1)
    for h, s in enumerate(scores):
        sl = slice(h * X_HEAD_DIM, (h + 1) * X_HEAD_DIM)
        s = s - jnp.max(s, axis=-1, keepdims=True)
        p = jnp.exp(s)
        p = p / jnp.sum(p, axis=-1, keepdims=True)
        oc_ref[:, sl] = jnp.dot(p.astype(BF16), mv_ref[:, sl], preferred_element_type=F32).astype(oc_ref.dtype)
    for c in range(2, n_chunks):
        diff_branch(c)

    z = cc_ref[...].astype(F32) * cx_ref[...].astype(F32)
    zp = cch_ref[...].astype(F32) * cxh_ref[...].astype(F32)
    zp = jnp.where(first, 0.0, zp)
    w = convw_ref[...]
    row8 = lax.broadcasted_iota(jnp.int32, (SUBLANES, 1), 0)
    y = z * w[CONV_K - 1:CONV_K, :]
    for back in range(1, CONV_K):
        zs = pltpu.roll(z, back, axis=0)
        ps = pltpu.roll(zp, back, axis=0)
        head = jnp.where(row8 < back, ps, zs[:SUBLANES])
        zs = jnp.concatenate([head, zs[SUBLANES:]], axis=0)
        y = y + zs * w[CONV_K - 1 - back:CONV_K - back, :]
    ua_ref[...] = (cb_ref[...].astype(F32) * y).astype(ua_ref.dtype)

    ua = ua_ref[...]
    oc = oc_ref[...]
    for c in range(n_chunks):
        sl = slice(c * MERGE_CHUNK, (c + 1) * MERGE_CHUNK)
        merged = yb_ref[:, sl]
        for br, act, w_ref in ((0, ua, wa_ref), (2, oc, wc_ref)):
            merged = merged + gate(br, c) * jnp.dot(act, w_ref[:, sl], preferred_element_type=F32)
        mg_ref[:, sl] = merged.astype(mg_ref.dtype)

    y = x_ref[...] + jnp.dot(mg_ref[...], wo_ref[...], preferred_element_type=F32)
    o_ref[...] = y
    hn_ref[...] = _rms_rows(y, gn_ref[...]).astype(hn_ref.dtype)


def merge(proj, ob, x, conv_w, g_q_x, g_next, mk, mv, w_conv_out, w_diff_out, w_x_out, w_o,
          *, layer, batch, cols, tm):
    T, D = x.shape
    S = T // batch
    M = mk.shape[0] // batch
    assert S % tm == 0 and tm % SUBLANES == 0
    tiles_per_seq = S // tm
    halo_blocks = tm // SUBLANES

    def col_spec(name, width):
        assert cols[name] % width == 0
        return pl.BlockSpec((tm, width), lambda i: (i, cols[name] // width))

    def halo_spec(name):
        return pl.BlockSpec((SUBLANES, CONV_WIDTH),
                            lambda i: (jnp.maximum(i * halo_blocks - 1, 0), cols[name] // CONV_WIDTH))

    def resident(shape):
        return pl.BlockSpec((None,) + shape, lambda i: (layer, 0, 0), pipeline_mode=pl.Buffered(1))

    assert cols["gates"] % GATE_BLOCK == 0 and (N_BRANCH * D) % GATE_BLOCK == 0 and D % MERGE_CHUNK == 0
    n_gate_blocks = N_BRANCH * D // GATE_BLOCK
    gate_specs = [pl.BlockSpec((tm, GATE_BLOCK), lambda i, g=g: (i, cols["gates"] // GATE_BLOCK + g))
                  for g in range(n_gate_blocks)]

    return pl.pallas_call(
        functools.partial(_merge_kernel, tiles_per_seq=tiles_per_seq, n_gate_blocks=n_gate_blocks),
        grid=(T // tm,),
        in_specs=gate_specs + [
            col_spec("cb", CONV_WIDTH),
            col_spec("cc", CONV_WIDTH),
            col_spec("cx", CONV_WIDTH),
            halo_spec("cc"),
            halo_spec("cx"),
            col_spec("xq", X_WIDTH),
            pl.BlockSpec((tm, DIFF_VW), lambda i: (i, 0)),
            pl.BlockSpec((tm, D), lambda i: (i, 0)),
            pl.BlockSpec((CONV_K, CONV_WIDTH), lambda i: (0, 0)),
            pl.BlockSpec((1, X_HEAD_DIM), lambda i: (0, 0)),
            pl.BlockSpec((1, D), lambda i: (0, 0)),
            pl.BlockSpec((M, X_WIDTH), lambda i: (i // tiles_per_seq, 0)),
            pl.BlockSpec((M, X_WIDTH), lambda i: (i // tiles_per_seq, 0)),
            resident((CONV_WIDTH, D)),
            resident((DIFF_VW, D)),
            resident((X_WIDTH, D)),
            resident((D, D)),
        ],
        out_specs=[pl.BlockSpec((tm, D), lambda i: (i, 0))] * 2,
        out_shape=[jax.ShapeDtypeStruct((T, D), F32), jax.ShapeDtypeStruct((T, D), BF16)],
        scratch_shapes=[
            pltpu.VMEM((tm, CONV_WIDTH), BF16),
            pltpu.VMEM((tm, X_WIDTH), BF16),
            pltpu.VMEM((tm, D), BF16),
            pltpu.VMEM((tm, D), F32),
        ],
        compiler_params=_compiler_params(("parallel",)),
        name="merge",
    )(*([proj] * (n_gate_blocks + 6)), ob, x, conv_w, g_q_x.reshape(1, X_HEAD_DIM), g_next.reshape(1, D),
      mk, mv, w_conv_out, w_diff_out, w_x_out, w_o)


def _ffn_up_kernel(h_ref, wg_ref, wu_ref, o_ref, wgb_ref, wub_ref):
    @pl.when(pl.program_id(1) == 0)
    def _():
        wgb_ref[...] = wg_ref[...].astype(wgb_ref.dtype)
        wub_ref[...] = wu_ref[...].astype(wub_ref.dtype)

    for r0 in range(0, h_ref.shape[0], MXU_ROW_CHUNK):
        h = h_ref[r0:r0 + MXU_ROW_CHUNK, :]
        a = jnp.dot(h, wgb_ref[...], preferred_element_type=F32)
        b = jnp.dot(h, wub_ref[...], preferred_element_type=F32)
        o_ref[r0:r0 + MXU_ROW_CHUNK, :] = (a * jax.nn.sigmoid(a) * b).astype(o_ref.dtype)


def ffn_up(h, w_gate_up, *, layer, tm, tf):
    T, D = h.shape
    d_ff = w_gate_up.shape[2] // 2
    assert T % tm == 0 and d_ff % tf == 0
    nf = d_ff // tf
    return pl.pallas_call(
        _ffn_up_kernel,
        grid=(nf, T // tm),
        in_specs=[
            pl.BlockSpec((tm, D), lambda f, i: (i, 0)),
            pl.BlockSpec((None, D, tf), lambda f, i: (layer, 0, f)),
            pl.BlockSpec((None, D, tf), lambda f, i: (layer, 0, nf + f)),
        ],
        out_specs=pl.BlockSpec((tm, tf), lambda f, i: (i, f)),
        out_shape=jax.ShapeDtypeStruct((T, d_ff), BF16),
        scratch_shapes=[pltpu.VMEM((D, tf), BF16), pltpu.VMEM((D, tf), BF16)],
        compiler_params=_compiler_params(("parallel", "arbitrary")),
        name="ffn_up",
    )(h, w_gate_up, w_gate_up)


def _ffn_down_kernel(a_ref, x_ref, wd_ref, *rest, normed_out):
    if normed_out:
        gn_ref, o_ref, hn_ref = rest
    else:
        (o_ref,) = rest
    y = x_ref[...] + jnp.dot(a_ref[...], wd_ref[...], preferred_element_type=F32)
    o_ref[...] = y
    if normed_out:
        hn_ref[...] = _rms_rows(y, gn_ref[...]).astype(hn_ref.dtype)


def ffn_down(act, x, w_down, *, layer, tm, g_next=None):
    T, D = x.shape
    d_ff = act.shape[1]
    assert T % tm == 0
    normed_out = g_next is not None
    row_spec = pl.BlockSpec((tm, D), lambda i: (i, 0))
    in_specs = [
        pl.BlockSpec((tm, d_ff), lambda i: (i, 0)),
        row_spec,
        pl.BlockSpec((None, d_ff, D), lambda i: (layer, 0, 0), pipeline_mode=pl.Buffered(1)),
    ]
    args = [act, x, w_down]
    out_specs, out_shape = row_spec, jax.ShapeDtypeStruct((T, D), F32)
    if normed_out:
        in_specs.append(pl.BlockSpec((1, D), lambda i: (0, 0)))
        args.append(g_next.reshape(1, D))
        out_specs = [row_spec, row_spec]
        out_shape = [out_shape, jax.ShapeDtypeStruct((T, D), BF16)]
    return pl.pallas_call(
        functools.partial(_ffn_down_kernel, normed_out=normed_out),
        grid=(T // tm,),
        in_specs=in_specs,
        out_specs=out_specs,
        out_shape=out_shape,
        compiler_params=_compiler_params(("parallel",)),
        name="ffn_down",
    )(*args)


def _tile(n, want):
    t = min(n, want)
    while n % t:
        t -= SUBLANES
    return t


def kernel(x, mem, positions, g_mix, w_in, conv_w, w_conv_out, g_q_diff, g_k_diff, lambda_vecs,
           g_subln, w_diff_out, g_mem, w_mem_kv, g_q_x, g_k_x, w_x_out, w_o, g_ffn, w_gate_up, w_down):
    B, S, D = x.shape
    M = mem.shape[1]
    T = B * S
    depth = w_in.shape[0]
    d_ff = w_down.shape[1]

    n_conv = 3 * CONV_WIDTH
    n_diff = 2 * DIFF_QK + DIFF_VW
    cols = {
        "cb": 0,
        "cc": CONV_WIDTH,
        "cx": 2 * CONV_WIDTH,
        "dq": n_conv,
        "dk": n_conv + DIFF_QK,
        "dv": n_conv + 2 * DIFF_QK,
        "xq": n_conv + n_diff,
        "gates": n_conv + n_diff + X_WIDTH,
    }

    xf = x.reshape(T, D)
    memf = mem.reshape(B * M, D)

    tm_proj = _tile(T, 2048)
    tq = _tile(S, 1024)
    tk = tq // 2
    pos = positions.reshape(T // tq, 1, tq)
    tm_merge = _tile(S, 256)
    tm_up = _tile(T, 2048)
    tf = _tile(d_ff, 512)
    tm_down = _tile(T, 256)

    w_mem_kv, w_conv_out, w_diff_out, w_x_out, w_o, w_down = (
        w.astype(BF16) for w in (w_mem_kv, w_conv_out, w_diff_out, w_x_out, w_o, w_down))

    h = rms_norm_rows(xf, g_mix[0], tm=_tile(T, 1024))
    for l in range(depth):
        lam_init = 0.8 - 0.6 * math.exp(-0.3 * l)

        proj = in_proj(h, w_in, layer=l, tm=tm_proj, tn=1024)
        qt, k, vt = qk_prep(proj, pos, g_q_diff[l], g_k_diff[l],
                            q_col=cols["dq"], k_col=cols["dk"], v_col=cols["dv"], tm=tq, tk=tk)
        ob = diff_attn(qt, k, vt, lambda_vecs[l], g_subln[l], batch=B, tq=tq, tk=tk, lam_init=lam_init)

        kv = norm_matmul(memf, g_mem[l], w_mem_kv, layer=l, tm=B * M, tn=1024, out_dtype=F32)
        mk, mv = mem_prep(kv, g_k_x[l])

        xf, h2 = merge(proj, ob, xf, conv_w[l], g_q_x[l], g_ffn[l], mk, mv,
                       w_conv_out, w_diff_out, w_x_out, w_o, layer=l, batch=B, cols=cols, tm=tm_merge)
        act = ffn_up(h2, w_gate_up, layer=l, tm=tm_up, tf=tf)
        if l + 1 < depth:
            xf, h = ffn_down(act, xf, w_down, layer=l, tm=tm_down, g_next=g_mix[l + 1])
        else:
            xf = ffn_down(act, xf, w_down, layer=l, tm=tm_down)

    return xf.reshape(B, S, D)
```

```python
import functools
import math

import jax
import jax.numpy as jnp
from jax import lax
from jax.experimental import pallas as pl
from jax.experimental.pallas import tpu as pltpu

EPS = 1e-6
CONV_WIDTH = 1024
CONV_K = 3
DIFF_HEADS = 8
DIFF_D = 64
DIFF_V = 2 * DIFF_D
DIFF_QK = DIFF_HEADS * 2 * DIFF_D
DIFF_VW = DIFF_HEADS * DIFF_V
X_HEADS = 4
X_HEAD_DIM = 256
X_WIDTH = X_HEADS * X_HEAD_DIM
N_BRANCH = 3
ROPE_THETA = 500000.0
ROT_FRAC = 4
ROT = DIFF_D // ROT_FRAC
ROT_HALF = ROT // 2

LANES = 128
SUBLANES = 8
BF16_ROWS = 16
V_ROWS = DIFF_V + BF16_ROWS
Q_SCALE = DIFF_D ** -0.5 * math.log2(math.e)
VMEM_LIMIT_BYTES = 56 * 1024 * 1024
MXU_ROW_CHUNK = 512

F32 = jnp.float32
BF16 = jnp.bfloat16


def _compiler_params(semantics):
    return pltpu.CompilerParams(dimension_semantics=semantics, vmem_limit_bytes=VMEM_LIMIT_BYTES)


def _rms_rows(t, g):
    ms = jnp.mean(t * t, axis=-1, keepdims=True)
    return t * lax.rsqrt(ms + EPS) * g


def _norm_matmul_kernel(x_ref, g_ref, w_ref, o_ref, h_ref):
    @pl.when(pl.program_id(1) == 0)
    def _():
        h_ref[...] = _rms_rows(x_ref[...], g_ref[...]).astype(h_ref.dtype)

    o_ref[...] = jnp.dot(h_ref[...], w_ref[...], preferred_element_type=F32).astype(o_ref.dtype)


def norm_matmul(x, g, w, *, layer, tm, tn, out_dtype):
    T, D = x.shape
    N = w.shape[2]
    assert T % tm == 0 and N % tn == 0
    return pl.pallas_call(
        _norm_matmul_kernel,
        grid=(T // tm, N // tn),
        in_specs=[
            pl.BlockSpec((tm, D), lambda i, j: (i, 0)),
            pl.BlockSpec((1, D), lambda i, j: (0, 0)),
            pl.BlockSpec((None, D, tn), lambda i, j: (layer, 0, j)),
        ],
        out_specs=pl.BlockSpec((tm, tn), lambda i, j: (i, j)),
        out_shape=jax.ShapeDtypeStruct((T, N), out_dtype),
        scratch_shapes=[pltpu.VMEM((tm, D), BF16)],
        compiler_params=_compiler_params(("parallel", "arbitrary")),
        name="norm_matmul",
    )(x, g.reshape(1, D), w)


def _rms_norm_kernel(x_ref, g_ref, o_ref):
    o_ref[...] = _rms_rows(x_ref[...], g_ref[...]).astype(o_ref.dtype)


def rms_norm_rows(x, g, *, tm):
    T, D = x.shape
    assert T % tm == 0
    return pl.pallas_call(
        _rms_norm_kernel,
        grid=(T // tm,),
        in_specs=[pl.BlockSpec((tm, D), lambda i: (i, 0)), pl.BlockSpec((1, D), lambda i: (0, 0))],
        out_specs=pl.BlockSpec((tm, D), lambda i: (i, 0)),
        out_shape=jax.ShapeDtypeStruct((T, D), BF16),
        compiler_params=_compiler_params(("parallel",)),
        name="rms_norm",
    )(x, g.reshape(1, D))


def _in_proj_kernel(h_ref, w_ref, o_ref, wb_ref):
    @pl.when(pl.program_id(1) == 0)
    def _():
        wb_ref[...] = w_ref[...].astype(wb_ref.dtype)

    for r0 in range(0, h_ref.shape[0], MXU_ROW_CHUNK):
        rows = slice(r0, r0 + MXU_ROW_CHUNK)
        o_ref[rows, :] = jnp.dot(h_ref[rows, :], wb_ref[...], preferred_element_type=F32).astype(o_ref.dtype)


def in_proj(h, w, *, layer, tm, tn):
    T, D = h.shape
    N = w.shape[2]
    assert T % tm == 0 and N % tn == 0
    return pl.pallas_call(
        _in_proj_kernel,
        grid=(N // tn, T // tm),
        in_specs=[
            pl.BlockSpec((tm, D), lambda j, i: (i, 0)),
            pl.BlockSpec((None, D, tn), lambda j, i: (layer, 0, j)),
        ],
        out_specs=pl.BlockSpec((tm, tn), lambda j, i: (i, j)),
        out_shape=jax.ShapeDtypeStruct((T, N), BF16),
        scratch_shapes=[pltpu.VMEM((D, tn), BF16)],
        compiler_params=_compiler_params(("parallel", "arbitrary")),
        name="in_proj",
    )(h, w)


def _group_mean_sq(t, gsum):
    sq = t * t
    hi = sq.astype(BF16)
    lo = (sq - hi.astype(F32)).astype(BF16)
    return jnp.dot(hi, gsum, preferred_element_type=F32) + jnp.dot(lo, gsum, preferred_element_type=F32)


PREP_ROWS = LANES


def _qk_prep_kernel(pos_ref, invf_ref, expand_ref, keep_ref, gq_ref, gk_ref, q_ref, k_ref, v_ref,
                    qt_ref, ko_ref, vt_ref, *, tk):
    tm = q_ref.shape[0]
    row = lax.broadcasted_iota(jnp.int32, (LANES, LANES), 0)
    col = lax.broadcasted_iota(jnp.int32, (LANES, LANES), 1)
    gsum = jnp.where((row < DIFF_D) == (col < DIFF_D), 1.0 / DIFF_D, 0.0).astype(BF16)
    trig_row = lax.broadcasted_iota(jnp.int32, (2 * ROT_HALF, 1), 0)
    trig_pad = jnp.zeros((LANES - 2 * ROT_HALF, PREP_ROWS), F32)
    pad_row = lax.broadcasted_iota(jnp.int32, (V_ROWS - DIFF_V, PREP_ROWS), 0)
    ones_row = jnp.where(pad_row == 0, 1.0, 0.0).astype(vt_ref.dtype)
    gq = gq_ref[...]
    gk = gk_ref[...]
    expand = expand_ref[...]

    for r0 in range(0, tm, PREP_ROWS):
        ang = invf_ref[...] * pos_ref[:, r0:r0 + PREP_ROWS].astype(F32)
        trig_t = jnp.where(trig_row < ROT_HALF, jnp.cos(ang), jnp.sin(ang))
        cos_t, sin_t = trig_t[:ROT_HALF], trig_t[ROT_HALF:]
        trig = jnp.concatenate([trig_t, trig_pad], axis=0).T
        hi = trig.astype(BF16)
        lo = (trig - hi.astype(F32)).astype(BF16)
        tab = (jnp.dot(hi, expand, preferred_element_type=F32)
               + jnp.dot(lo, expand, preferred_element_type=F32))
        c_keep = tab[:, :LANES] + keep_ref[...]
        s_from_hi = tab[:, LANES:2 * LANES]
        s_from_lo = tab[:, 2 * LANES:]

        def norm(src_ref, g, c):
            t = src_ref[r0:r0 + PREP_ROWS, c * LANES:(c + 1) * LANES].astype(F32)
            return t * lax.rsqrt(_group_mean_sq(t, gsum) + EPS) * g

        def rot_lanes(tn):
            up = pltpu.roll(tn, LANES - ROT_HALF, axis=1)
            dn = pltpu.roll(tn, ROT_HALF, axis=1)
            return tn * c_keep + up * s_from_hi + dn * s_from_lo

        def rot_rows(tt):
            parts = []
            for base in range(0, LANES, DIFF_D):
                x1 = tt[base:base + ROT_HALF]
                x2 = tt[base + ROT_HALF:base + ROT]
                parts += [x1 * cos_t - x2 * sin_t, x2 * cos_t + x1 * sin_t, tt[base + ROT:base + DIFF_D]]
            return jnp.concatenate(parts, axis=0)

        u, off = divmod(r0, tk)
        for h in range(DIFF_HEADS):
            sl = slice(h * LANES, (h + 1) * LANES)
            qt = rot_rows(norm(q_ref, gq, h).T)
            qt_ref[h, 0, :, r0:r0 + PREP_ROWS] = qt.astype(qt_ref.dtype)
            ko_ref[r0:r0 + PREP_ROWS, sl] = rot_lanes(norm(k_ref, gk, h)).astype(ko_ref.dtype)
            vt_ref[h, u, :DIFF_V, off:off + PREP_ROWS] = v_ref[r0:r0 + PREP_ROWS, sl].T
            vt_ref[h, u, DIFF_V:, off:off + PREP_ROWS] = ones_row


def _rotary_tables():
    inv_freq = ROPE_THETA ** (-jnp.arange(ROT_HALF, dtype=F32) / ROT_HALF)
    invf = jnp.concatenate([inv_freq, inv_freq]).reshape(2 * ROT_HALF, 1)
    src = jnp.arange(LANES)[:, None]
    d = jnp.arange(LANES)[None, :] % DIFF_D
    cos_part = jnp.where((d < ROT) & (src == d % ROT_HALF), 1.0, 0.0)
    hi_part = jnp.where((d < ROT_HALF) & (src == ROT_HALF + d), -1.0, 0.0)
    lo_part = jnp.where((d >= ROT_HALF) & (d < ROT) & (src == d), 1.0, 0.0)
    expand = jnp.concatenate([cos_part, hi_part, lo_part], axis=1).astype(BF16)
    keep = jnp.where(d < ROT, 0.0, 1.0).astype(F32)
    return invf, expand, keep


def qk_prep(proj, pos, g_q, g_k, *, q_col, k_col, v_col, tm, tk):
    T = proj.shape[0]
    assert T % tm == 0 and tm % tk == 0 and tk % PREP_ROWS == 0
    assert q_col % DIFF_QK == 0 and k_col % DIFF_QK == 0 and v_col % DIFF_VW == 0 and DIFF_V == LANES
    gq = (jnp.tile(g_q, LANES // DIFF_D) * Q_SCALE).reshape(1, LANES)
    gk = jnp.tile(g_k, LANES // DIFF_D).reshape(1, LANES)
    invf, expand, keep = _rotary_tables()
    return pl.pallas_call(
        functools.partial(_qk_prep_kernel, tk=tk),
        grid=(T // tm,),
        in_specs=[
            pl.BlockSpec((None, 1, tm), lambda i: (i, 0, 0)),
            pl.BlockSpec((2 * ROT_HALF, 1), lambda i: (0, 0)),
            pl.BlockSpec((LANES, 3 * LANES), lambda i: (0, 0)),
            pl.BlockSpec((1, LANES), lambda i: (0, 0)),
            pl.BlockSpec((1, LANES), lambda i: (0, 0)),
            pl.BlockSpec((1, LANES), lambda i: (0, 0)),
            pl.BlockSpec((tm, DIFF_QK), lambda i: (i, q_col // DIFF_QK)),
            pl.BlockSpec((tm, DIFF_QK), lambda i: (i, k_col // DIFF_QK)),
            pl.BlockSpec((tm, DIFF_VW), lambda i: (i, v_col // DIFF_VW)),
        ],
        out_specs=[
            pl.BlockSpec((DIFF_HEADS, 1, LANES, tm), lambda i: (0, i, 0, 0)),
            pl.BlockSpec((tm, DIFF_QK), lambda i: (i, 0)),
            pl.BlockSpec((DIFF_HEADS, tm // tk, V_ROWS, tk), lambda i: (0, i, 0, 0)),
        ],
        out_shape=[
            jax.ShapeDtypeStruct((DIFF_HEADS, T // tm, LANES, tm), BF16),
            jax.ShapeDtypeStruct((T, DIFF_QK), BF16),
            jax.ShapeDtypeStruct((DIFF_HEADS, T // tk, V_ROWS, tk), BF16),
        ],
        compiler_params=_compiler_params(("parallel",)),
        name="qk_prep",
    )(pos, invf, expand, keep, gq, gk, proj, proj, proj)


def _diff_attn_kernel(lam_ref, g_ref, qt_ref, k_ref, vt_ref, o_ref, m_ref, acc_ref,
                      s0_ref, s1_ref, p0_ref, p1_ref, a0_ref, a1_ref, bm0_ref, bm1_ref,
                      *, tq, tk, lam_init):
    assert tq == 2 * tk
    s_bufs, p_bufs, a_bufs, bm_bufs = (s0_ref, s1_ref), (p0_ref, p1_ref), (a0_ref, a1_ref), (bm0_ref, bm1_ref)
    n_tiles = qt_ref.shape[1]

    lv = lam_ref[...]
    e1 = jnp.exp(jnp.sum(lv[0:1, :] * lv[1:2, :], axis=-1, keepdims=True))
    e2 = jnp.exp(jnp.sum(lv[2:3, :] * lv[3:4, :], axis=-1, keepdims=True))
    lam = e1 - e2 + lam_init

    acc_ref[...] = jnp.zeros(acc_ref.shape, F32)

    def scores(i, u, slot, masked, q0=0):
        k = k_ref[pl.ds(pl.multiple_of(u * tk, tk), tk), :]
        qt = qt_ref[0, i, :, q0:]
        nq_cols = tq - q0
        zeros = jnp.zeros((DIFF_D, nq_cols), qt.dtype)
        q_comp = (jnp.concatenate([qt[:DIFF_D], zeros], axis=0),
                  jnp.concatenate([zeros, qt[DIFF_D:]], axis=0))
        if masked:
            kpos = u * tk + lax.broadcasted_iota(jnp.int32, (tk, nq_cols), 0)
            qpos = i * tq + q0 + lax.broadcasted_iota(jnp.int32, (tk, nq_cols), 1)
            causal = kpos <= qpos
        for c in range(2):
            s = jnp.dot(k, q_comp[c], preferred_element_type=F32)
            if masked:
                s = jnp.where(causal, s, -jnp.inf)
            s_bufs[slot][c, :, q0:] = s
            bm_bufs[slot][c, :, q0:] = jnp.max(s, axis=0, keepdims=True)

    def softmax(slot, first=False, q0=0):
        for c in range(2):
            if first:
                m_new = bm_bufs[slot][c]
                a_bufs[slot][c] = jnp.zeros_like(m_new)
            else:
                m_prev = m_ref[c, :, q0:]
                m_new = jnp.maximum(m_prev, bm_bufs[slot][c, :, q0:])
                a_bufs[slot][c, :, q0:] = jnp.exp2(m_prev - m_new)
            m_ref[c, :, q0:] = m_new
            p_bufs[slot][c, :, q0:] = jnp.exp2(s_bufs[slot][c, :, q0:] - m_new).astype(p_bufs[slot].dtype)

    def accumulate(u, slot, q0=0):
        vt = vt_ref[0, u]
        for c in range(2):
            pv = jnp.dot(vt, p_bufs[slot][c, :, q0:], preferred_element_type=F32)
            acc_ref[c, :, q0:] = a_bufs[slot][c, :, q0:] * acc_ref[c, :, q0:] + pv

    def finalize(i):
        o = (acc_ref[0, :DIFF_V] / acc_ref[0, DIFF_V:DIFF_V + 1]
             - lam * (acc_ref[1, :DIFF_V] / acc_ref[1, DIFF_V:DIFF_V + 1]))
        o = o * lax.rsqrt(jnp.mean(o * o, axis=0, keepdims=True) + EPS)
        rows = pl.ds(pl.multiple_of(i * tq, tq), tq)
        o_ref[rows, :] = (o.T * g_ref[...] * (1.0 - lam_init)).astype(o_ref.dtype)

    def step_pair(i, u, masked):
        scores(i, u, 0, masked)
        accumulate(u - 2, 0)
        softmax(1)
        scores(i, u + 1, 1, masked, q0=tk if masked else 0)
        accumulate(u - 1, 1)
        softmax(0)

    scores(0, 0, 0, masked=True)
    scores(0, 1, 1, masked=True, q0=tk)
    softmax(0, first=True)

    def next_tile(i):
        last = 2 * (i - 1)
        scores(i, 0, 0, masked=False)
        accumulate(last, 0)
        softmax(1, q0=tk)
        scores(i, 1, 1, masked=False)
        accumulate(last + 1, 1, q0=tk)
        finalize(i - 1)
        softmax(0, first=True)

    def full_pairs(i):
        def full_pair(r, c):
            step_pair(i, 2 * r, masked=False)
            return c

        lax.fori_loop(1, i, full_pair, 0)

    if n_tiles > 1:
        next_tile(1)

        def q_tile(i, carry):
            full_pairs(i)
            step_pair(i, 2 * i, masked=True)
            next_tile(i + 1)
            return carry

        lax.fori_loop(1, n_tiles - 1, q_tile, 0)
        full_pairs(n_tiles - 1)
        step_pair(n_tiles - 1, 2 * (n_tiles - 1), masked=True)

    last = 2 * (n_tiles - 1)
    accumulate(last, 0)
    softmax(1, q0=tk)
    accumulate(last + 1, 1, q0=tk)
    finalize(n_tiles - 1)


def diff_attn(qt, k, vt, lambda_vecs, g_subln, *, batch, tq, tk, lam_init):
    T = k.shape[0]
    S = T // batch
    assert S % tq == 0 and tq == 2 * tk
    nq, nk = S // tq, S // tk
    stat = pltpu.VMEM((2, 1, tq), F32)
    return pl.pallas_call(
        functools.partial(_diff_attn_kernel, tq=tq, tk=tk, lam_init=lam_init),
        grid=(batch, DIFF_HEADS),
        in_specs=[
            pl.BlockSpec((4, DIFF_D), lambda b, h: (0, 0)),
            pl.BlockSpec((1, DIFF_V), lambda b, h: (0, 0)),
            pl.BlockSpec((1, nq, LANES, tq), lambda b, h: (h, b, 0, 0)),
            pl.BlockSpec((S, LANES), lambda b, h: (b, h)),
            pl.BlockSpec((1, nk, V_ROWS, tk), lambda b, h: (h, b, 0, 0)),
        ],
        out_specs=pl.BlockSpec((S, DIFF_V), lambda b, h: (b, h)),
        out_shape=jax.ShapeDtypeStruct((T, DIFF_VW), BF16),
        scratch_shapes=[
            stat,
            pltpu.VMEM((2, V_ROWS, tq), F32),
            pltpu.VMEM((2, tk, tq), F32),
            pltpu.VMEM((2, tk, tq), F32),
            pltpu.VMEM((2, tk, tq), BF16),
            pltpu.VMEM((2, tk, tq), BF16),
            stat, stat,
            stat, stat,
        ],
        compiler_params=_compiler_params(("parallel", "parallel")),
        name="diff_attn",
    )(lambda_vecs, g_subln.reshape(1, DIFF_V), qt, k, vt)


def _mem_prep_kernel(kv_ref, g_ref, mk_ref, mv_ref):
    g = g_ref[...]
    for h in range(X_HEADS):
        sl = slice(h * X_HEAD_DIM, (h + 1) * X_HEAD_DIM)
        kh = _rms_rows(kv_ref[:, sl], g) * (X_HEAD_DIM ** -0.5)
        mk_ref[:, sl] = kh.astype(mk_ref.dtype)
    mv_ref[...] = kv_ref[:, X_WIDTH:].astype(mv_ref.dtype)


def mem_prep(kv, g_k_x):
    R = kv.shape[0]
    out = jax.ShapeDtypeStruct((R, X_WIDTH), BF16)
    return pl.pallas_call(
        _mem_prep_kernel,
        grid=(1,),
        in_specs=[
            pl.BlockSpec((R, 2 * X_WIDTH), lambda i: (0, 0)),
            pl.BlockSpec((1, X_HEAD_DIM), lambda i: (0, 0)),
        ],
        out_specs=[pl.BlockSpec((R, X_WIDTH), lambda i: (0, 0))] * 2,
        out_shape=[out, out],
        compiler_params=_compiler_params(("arbitrary",)),
        name="mem_prep",
    )(kv, g_k_x.reshape(1, X_HEAD_DIM))


MERGE_CHUNK = 512
GATE_BLOCK = 1024
assert GATE_BLOCK % MERGE_CHUNK == 0


def _merge_kernel(*refs, tiles_per_seq, n_gate_blocks):
    gate_refs = refs[:n_gate_blocks]
    (cb_ref, cc_ref, cx_ref, cch_ref, cxh_ref, xq_ref, ob_ref, x_ref,
     convw_ref, gqx_ref, gn_ref, mk_ref, mv_ref, wa_ref, wb_ref, wc_ref, wo_ref,
     o_ref, hn_ref, ua_ref, oc_ref, mg_ref, yb_ref) = refs[n_gate_blocks:]
    tm = x_ref.shape[0]
    d_model = x_ref.shape[1]
    first = pl.program_id(0) % tiles_per_seq == 0

    def gate(branch, c):
        blk, off = divmod(branch * d_model + c * MERGE_CHUNK, GATE_BLOCK)
        return jax.nn.sigmoid(gate_refs[blk][:, off:off + MERGE_CHUNK].astype(F32))

    ob = ob_ref[...]
    n_chunks = d_model // MERGE_CHUNK

    def diff_branch(c):
        sl = slice(c * MERGE_CHUNK, (c + 1) * MERGE_CHUNK)
        yb_ref[:, sl] = gate(1, c) * jnp.dot(ob, wb_ref[:, sl], preferred_element_type=F32)

    diff_branch(0)

    gq = gqx_ref[...]
    scores = []
    for h in range(X_HEADS):
        sl = slice(h * X_HEAD_DIM, (h + 1) * X_HEAD_DIM)
        qn = _rms_rows(xq_ref[:, sl].astype(F32), gq).astype(BF16)
        scores.append(lax.dot_general(qn, mk_ref[:, sl], (((1,), (1,)), ((), ())),
                                      preferred_element_type=F32))
    diff_branch(1)
    for h, s in enumerate(scores):
        sl = slice(h * X_HEAD_DIM, (h + 1) * X_HEAD_DIM)
        s = s - jnp.max(s, axis=-1, keepdims=True)
        p = jnp.exp(s)
        p = p / jnp.sum(p, axis=-1, keepdims=True)
        oc_ref[:, sl] = jnp.dot(p.astype(BF16), mv_ref[:, sl], preferred_element_type=F32).astype(oc_ref.dtype)
    for c in range(2, n_chunks):
        diff_branch(c)

    z = cc_ref[...].astype(F32) * cx_ref[...].astype(F32)
    zp = cch_ref[...].astype(F32) * cxh_ref[...].astype(F32)
    zp = jnp.where(first, 0.0, zp)
    w = convw_ref[...]
    row8 = lax.broadcasted_iota(jnp.int32, (SUBLANES, 1), 0)
    y = z * w[CONV_K - 1:CONV_K, :]
    for back in range(1, CONV_K):
        zs = pltpu.roll(z, back, axis=0)
        ps = pltpu.roll(zp, back, axis=0)
        head = jnp.where(row8 < back, ps, zs[:SUBLANES])
        zs = jnp.concatenate([head, zs[SUBLANES:]], axis=0)
        y = y + zs * w[CONV_K - 1 - back:CONV_K - back, :]
    ua_ref[...] = (cb_ref[...].astype(F32) * y).astype(ua_ref.dtype)

    ua = ua_ref[...]
    oc = oc_ref[...]
    for c in range(n_chunks):
        sl = slice(c * MERGE_CHUNK, (c + 1) * MERGE_CHUNK)
        merged = yb_ref[:, sl]
        for br, act, w_ref in ((0, ua, wa_ref), (2, oc, wc_ref)):
            merged = merged + gate(br, c) * jnp.dot(act, w_ref[:, sl], preferred_element_type=F32)
        mg_ref[:, sl] = merged.astype(mg_ref.dtype)

    y = x_ref[...] + jnp.dot(mg_ref[...], wo_ref[...], preferred_element_type=F32)
    o_ref[...] = y
    hn_ref[...] = _rms_rows(y, gn_ref[...]).astype(hn_ref.dtype)


def merge(proj, ob, x, conv_w, g_q_x, g_next, mk, mv, w_conv_out, w_diff_out, w_x_out, w_o,
          *, layer, batch, cols, tm):
    T, D = x.shape
    S = T // batch
    M = mk.shape[0] // batch
    assert S % tm == 0 and tm % SUBLANES == 0
    tiles_per_seq = S // tm
    halo_blocks = tm // SUBLANES

    def col_spec(name, width):
        assert cols[name] % width == 0
        return pl.BlockSpec((tm, width), lambda i: (i, cols[name] // width))

    def halo_spec(name):
        return pl.BlockSpec((SUBLANES, CONV_WIDTH),
                            lambda i: (jnp.maximum(i * halo_blocks - 1, 0), cols[name] // CONV_WIDTH))

    def resident(shape):
        return pl.BlockSpec((None,) + shape, lambda i: (layer, 0, 0), pipeline_mode=pl.Buffered(1))

    assert cols["gates"] % GATE_BLOCK == 0 and (N_BRANCH * D) % GATE_BLOCK == 0 and D % MERGE_CHUNK == 0
    n_gate_blocks = N_BRANCH * D // GATE_BLOCK
    gate_specs = [pl.BlockSpec((tm, GATE_BLOCK), lambda i, g=g: (i, cols["gates"] // GATE_BLOCK + g))
                  for g in range(n_gate_blocks)]

    return pl.pallas_call(
        functools.partial(_merge_kernel, tiles_per_seq=tiles_per_seq, n_gate_blocks=n_gate_blocks),
        grid=(T // tm,),
        in_specs=gate_specs + [
            col_spec("cb", CONV_WIDTH),
            col_spec("cc", CONV_WIDTH),
            col_spec("cx", CONV_WIDTH),
            halo_spec("cc"),
            halo_spec("cx"),
            col_spec("xq", X_WIDTH),
            pl.BlockSpec((tm, DIFF_VW), lambda i: (i, 0)),
            pl.BlockSpec((tm, D), lambda i: (i, 0)),
            pl.BlockSpec((CONV_K, CONV_WIDTH), lambda i: (0, 0)),
            pl.BlockSpec((1, X_HEAD_DIM), lambda i: (0, 0)),
            pl.BlockSpec((1, D), lambda i: (0, 0)),
            pl.BlockSpec((M, X_WIDTH), lambda i: (i // tiles_per_seq, 0)),
            pl.BlockSpec((M, X_WIDTH), lambda i: (i // tiles_per_seq, 0)),
            resident((CONV_WIDTH, D)),
            resident((DIFF_VW, D)),
            resident((X_WIDTH, D)),
            resident((D, D)),
        ],
        out_specs=[pl.BlockSpec((tm, D), lambda i: (i, 0))] * 2,
        out_shape=[jax.ShapeDtypeStruct((T, D), F32), jax.ShapeDtypeStruct((T, D), BF16)],
        scratch_shapes=[
            pltpu.VMEM((tm, CONV_WIDTH), BF16),
            pltpu.VMEM((tm, X_WIDTH), BF16),
            pltpu.VMEM((tm, D), BF16),
            pltpu.VMEM((tm, D), F32),
        ],
        compiler_params=_compiler_params(("parallel",)),
        name="merge",
    )(*([proj] * (n_gate_blocks + 6)), ob, x, conv_w, g_q_x.reshape(1, X_HEAD_DIM), g_next.reshape(1, D),
      mk, mv, w_conv_out, w_diff_out, w_x_out, w_o)


def _ffn_up_kernel(h_ref, wg_ref, wu_ref, o_ref, wgb_ref, wub_ref):
    @pl.when(pl.program_id(1) == 0)
    def _():
        wgb_ref[...] = wg_ref[...].astype(wgb_ref.dtype)
        wub_ref[...] = wu_ref[...].astype(wub_ref.dtype)

    for r0 in range(0, h_ref.shape[0], MXU_ROW_CHUNK):
        h = h_ref[r0:r0 + MXU_ROW_CHUNK, :]
        a = jnp.dot(h, wgb_ref[...], preferred_element_type=F32)
        b = jnp.dot(h, wub_ref[...], preferred_element_type=F32)
        o_ref[r0:r0 + MXU_ROW_CHUNK, :] = (a * jax.nn.sigmoid(a) * b).astype(o_ref.dtype)


def ffn_up(h, w_gate_up, *, layer, tm, tf):
    T, D = h.shape
    d_ff = w_gate_up.shape[2] // 2
    assert T % tm == 0 and d_ff % tf == 0
    nf = d_ff // tf
    return pl.pallas_call(
        _ffn_up_kernel,
        grid=(nf, T // tm),
        in_specs=[
            pl.BlockSpec((tm, D), lambda f, i: (i, 0)),
            pl.BlockSpec((None, D, tf), lambda f, i: (layer, 0, f)),
            pl.BlockSpec((None, D, tf), lambda f, i: (layer, 0, nf + f)),
        ],
        out_specs=pl.BlockSpec((tm, tf), lambda f, i: (i, f)),
        out_shape=jax.ShapeDtypeStruct((T, d_ff), BF16),
        scratch_shapes=[pltpu.VMEM((D, tf), BF16), pltpu.VMEM((D, tf), BF16)],
        compiler_params=_compiler_params(("parallel", "arbitrary")),
        name="ffn_up",
    )(h, w_gate_up, w_gate_up)


def _ffn_down_kernel(a_ref, x_ref, wd_ref, *rest, normed_out):
    if normed_out:
        gn_ref, o_ref, hn_ref = rest
    else:
        (o_ref,) = rest
    y = x_ref[...] + jnp.dot(a_ref[...], wd_ref[...], preferred_element_type=F32)
    o_ref[...] = y
    if normed_out:
        hn_ref[...] = _rms_rows(y, gn_ref[...]).astype(hn_ref.dtype)


def ffn_down(act, x, w_down, *, layer, tm, g_next=None):
    T, D = x.shape
    d_ff = act.shape[1]
    assert T % tm == 0
    normed_out = g_next is not None
    row_spec = pl.BlockSpec((tm, D), lambda i: (i, 0))
    in_specs = [
        pl.BlockSpec((tm, d_ff), lambda i: (i, 0)),
        row_spec,
        pl.BlockSpec((None, d_ff, D), lambda i: (layer, 0, 0), pipeline_mode=pl.Buffered(1)),
    ]
    args = [act, x, w_down]
    out_specs, out_shape = row_spec, jax.ShapeDtypeStruct((T, D), F32)
    if normed_out:
        in_specs.append(pl.BlockSpec((1, D), lambda i: (0, 0)))
        args.append(g_next.reshape(1, D))
        out_specs = [row_spec, row_spec]
        out_shape = [out_shape, jax.ShapeDtypeStruct((T, D), BF16)]
    return pl.pallas_call(
        functools.partial(_ffn_down_kernel, normed_out=normed_out),
        grid=(T // tm,),
        in_specs=in_specs,
        out_specs=out_specs,
        out_shape=out_shape,
        compiler_params=_compiler_params(("parallel",)),
        name="ffn_down",
    )(*args)


def _tile(n, want):
    t = min(n, want)
    while n % t:
        t -= SUBLANES
    return t


def kernel(x, mem, positions, g_mix, w_in, conv_w, w_conv_out, g_q_diff, g_k_diff, lambda_vecs,
           g_subln, w_diff_out, g_mem, w_mem_kv, g_q_x, g_k_x, w_x_out, w_o, g_ffn, w_gate_up, w_down):
    B, S, D = x.shape
    M = mem.shape[1]
    T = B * S
    depth = w_in.shape[0]
    d_ff = w_down.shape[1]

    n_conv = 3 * CONV_WIDTH
    n_diff = 2 * DIFF_QK + DIFF_VW
    cols = {
        "cb": 0,
        "cc": CONV_WIDTH,
        "cx": 2 * CONV_WIDTH,
        "dq": n_conv,
        "dk": n_conv + DIFF_QK,
        "dv": n_conv + 2 * DIFF_QK,
        "xq": n_conv + n_diff,
        "gates": n_conv + n_diff + X_WIDTH,
    }

    xf = x.reshape(T, D)
    memf = mem.reshape(B * M, D)

    tm_proj = _tile(T, 2048)
    tq = _tile(S, 1024)
    tk = tq // 2
    pos = positions.reshape(T // tq, 1, tq)
    tm_merge = _tile(S, 256)
    tm_up = _tile(T, 2048)
    tf = _tile(d_ff, 512)
    tm_down = _tile(T, 256)

    w_mem_kv, w_conv_out, w_diff_out, w_x_out, w_o, w_down = (
        w.astype(BF16) for w in (w_mem_kv, w_conv_out, w_diff_out, w_x_out, w_o, w_down))

    h = rms_norm_rows(xf, g_mix[0], tm=_tile(T, 1024))
    for l in range(depth):
        lam_init = 0.8 - 0.6 * math.exp(-0.3 * l)

        proj = in_proj(h, w_in, layer=l, tm=tm_proj, tn=1024)
        qt, k, vt = qk_prep(proj, pos, g_q_diff[l], g_k_diff[l],
                            q_col=cols["dq"], k_col=cols["dk"], v_col=cols["dv"], tm=tq, tk=tk)
        ob = diff_attn(qt, k, vt, lambda_vecs[l], g_subln[l], batch=B, tq=tq, tk=tk, lam_init=lam_init)

        kv = norm_matmul(memf, g_mem[l], w_mem_kv, layer=l, tm=B * M, tn=1024, out_dtype=F32)
        mk, mv = mem_prep(kv, g_k_x[l])

        xf, h2 = merge(proj, ob, xf, conv_w[l], g_q_x[l], g_ffn[l], mk, mv,
                       w_conv_out, w_diff_out, w_x_out, w_o, layer=l, batch=B, cols=cols, tm=tm_merge)
        act = ffn_up(h2, w_gate_up, layer=l, tm=tm_up, tf=tf)
        if l + 1 < depth:
            xf, h = ffn_down(act, xf, w_down, layer=l, tm=tm_down, g_next=g_mix[l + 1])
        else:
            xf = ffn_down(act, xf, w_down, layer=l, tm=tm_down)

    return xf.reshape(B, S, D)
```

```python
import functools
import math

import jax
import jax.numpy as jnp
from jax import lax
from jax.experimental import pallas as pl
from jax.experimental.pallas import tpu as pltpu

EPS = 1e-6
CONV_WIDTH = 1024
CONV_K = 3
DIFF_HEADS = 8
DIFF_D = 64
DIFF_V = 2 * DIFF_D
DIFF_QK = DIFF_HEADS * 2 * DIFF_D
DIFF_VW = DIFF_HEADS * DIFF_V
X_HEADS = 4
X_HEAD_DIM = 256
X_WIDTH = X_HEADS * X_HEAD_DIM
N_BRANCH = 3
ROPE_THETA = 500000.0
ROT_FRAC = 4
ROT = DIFF_D // ROT_FRAC
ROT_HALF = ROT // 2

LANES = 128
SUBLANES = 8
BF16_ROWS = 16
V_ROWS = DIFF_V + BF16_ROWS
Q_SCALE = DIFF_D ** -0.5 * math.log2(math.e)
VMEM_LIMIT_BYTES = 56 * 1024 * 1024
MXU_ROW_CHUNK = 512

F32 = jnp.float32
BF16 = jnp.bfloat16


def _compiler_params(semantics):
    return pltpu.CompilerParams(dimension_semantics=semantics, vmem_limit_bytes=VMEM_LIMIT_BYTES)


def _rms_rows(t, g):
    ms = jnp.mean(t * t, axis=-1, keepdims=True)
    return t * lax.rsqrt(ms + EPS) * g


def _rms_norm_kernel(x_ref, g_ref, o_ref):
    o_ref[...] = _rms_rows(x_ref[...], g_ref[...]).astype(o_ref.dtype)


def rms_norm_rows(x, g, *, tm):
    T, D = x.shape
    assert T % tm == 0
    return pl.pallas_call(
        _rms_norm_kernel,
        grid=(T // tm,),
        in_specs=[pl.BlockSpec((tm, D), lambda i: (i, 0)), pl.BlockSpec((1, D), lambda i: (0, 0))],
        out_specs=pl.BlockSpec((tm, D), lambda i: (i, 0)),
        out_shape=jax.ShapeDtypeStruct((T, D), BF16),
        compiler_params=_compiler_params(("parallel",)),
        name="rms_norm",
    )(x, g.reshape(1, D))


def _in_proj_kernel(h_ref, w_ref, o_ref, wb_ref):
    @pl.when(pl.program_id(1) == 0)
    def _():
        wb_ref[...] = w_ref[...].astype(wb_ref.dtype)

    for r0 in range(0, h_ref.shape[0], MXU_ROW_CHUNK):
        rows = slice(r0, r0 + MXU_ROW_CHUNK)
        o_ref[rows, :] = jnp.dot(h_ref[rows, :], wb_ref[...], preferred_element_type=F32).astype(o_ref.dtype)


def in_proj(h, w, *, layer, tm, tn):
    T, D = h.shape
    N = w.shape[2]
    assert T % tm == 0 and N % tn == 0
    return pl.pallas_call(
        _in_proj_kernel,
        grid=(N // tn, T // tm),
        in_specs=[
            pl.BlockSpec((tm, D), lambda j, i: (i, 0)),
            pl.BlockSpec((None, D, tn), lambda j, i: (layer, 0, j)),
        ],
        out_specs=pl.BlockSpec((tm, tn), lambda j, i: (i, j)),
        out_shape=jax.ShapeDtypeStruct((T, N), BF16),
        scratch_shapes=[pltpu.VMEM((D, tn), BF16)],
        compiler_params=_compiler_params(("parallel", "arbitrary")),
        name="in_proj",
    )(h, w)


def _group_mean_sq(t, gsum):
    sq = t * t
    hi = sq.astype(BF16)
    lo = (sq - hi.astype(F32)).astype(BF16)
    return jnp.dot(hi, gsum, preferred_element_type=F32) + jnp.dot(lo, gsum, preferred_element_type=F32)


PREP_ROWS = LANES


def _qk_prep_kernel(pos_ref, invf_ref, expand_ref, keep_ref, gq_ref, gk_ref, q_ref, k_ref, v_ref,
                    qt_ref, ko_ref, vt_ref, *, tk):
    tm = q_ref.shape[0]
    row = lax.broadcasted_iota(jnp.int32, (LANES, LANES), 0)
    col = lax.broadcasted_iota(jnp.int32, (LANES, LANES), 1)
    gsum = jnp.where((row < DIFF_D) == (col < DIFF_D), 1.0 / DIFF_D, 0.0).astype(BF16)
    trig_row = lax.broadcasted_iota(jnp.int32, (2 * ROT_HALF, 1), 0)
    trig_pad = jnp.zeros((LANES - 2 * ROT_HALF, PREP_ROWS), F32)
    pad_row = lax.broadcasted_iota(jnp.int32, (V_ROWS - DIFF_V, PREP_ROWS), 0)
    ones_row = jnp.where(pad_row == 0, 1.0, 0.0).astype(vt_ref.dtype)
    gq = gq_ref[...]
    gk = gk_ref[...]
    expand = expand_ref[...]

    for r0 in range(0, tm, PREP_ROWS):
        ang = invf_ref[...] * pos_ref[:, r0:r0 + PREP_ROWS].astype(F32)
        trig_t = jnp.where(trig_row < ROT_HALF, jnp.cos(ang), jnp.sin(ang))
        cos_t, sin_t = trig_t[:ROT_HALF], trig_t[ROT_HALF:]
        trig = jnp.concatenate([trig_t, trig_pad], axis=0).T
        hi = trig.astype(BF16)
        lo = (trig - hi.astype(F32)).astype(BF16)
        tab = (jnp.dot(hi, expand, preferred_element_type=F32)
               + jnp.dot(lo, expand, preferred_element_type=F32))
        c_keep = tab[:, :LANES] + keep_ref[...]
        s_from_hi = tab[:, LANES:2 * LANES]
        s_from_lo = tab[:, 2 * LANES:]

        def norm(src_ref, g, c):
            t = src_ref[r0:r0 + PREP_ROWS, c * LANES:(c + 1) * LANES].astype(F32)
            return t * lax.rsqrt(_group_mean_sq(t, gsum) + EPS) * g

        def rot_lanes(tn):
            up = pltpu.roll(tn, LANES - ROT_HALF, axis=1)
            dn = pltpu.roll(tn, ROT_HALF, axis=1)
            return tn * c_keep + up * s_from_hi + dn * s_from_lo

        def rot_rows(tt):
            parts = []
            for base in range(0, LANES, DIFF_D):
                x1 = tt[base:base + ROT_HALF]
                x2 = tt[base + ROT_HALF:base + ROT]
                parts += [x1 * cos_t - x2 * sin_t, x2 * cos_t + x1 * sin_t, tt[base + ROT:base + DIFF_D]]
            return jnp.concatenate(parts, axis=0)

        u, off = divmod(r0, tk)
        for h in range(DIFF_HEADS):
            sl = slice(h * LANES, (h + 1) * LANES)
            qt = rot_rows(norm(q_ref, gq, h).T)
            qt_ref[h, 0, :, r0:r0 + PREP_ROWS] = qt.astype(qt_ref.dtype)
            ko_ref[r0:r0 + PREP_ROWS, sl] = rot_lanes(norm(k_ref, gk, h)).astype(ko_ref.dtype)
            vt_ref[h, u, :DIFF_V, off:off + PREP_ROWS] = v_ref[r0:r0 + PREP_ROWS, sl].T
            vt_ref[h, u, DIFF_V:, off:off + PREP_ROWS] = ones_row


def _rotary_tables():
    inv_freq = ROPE_THETA ** (-jnp.arange(ROT_HALF, dtype=F32) / ROT_HALF)
    invf = jnp.concatenate([inv_freq, inv_freq]).reshape(2 * ROT_HALF, 1)
    src = jnp.arange(LANES)[:, None]
    d = jnp.arange(LANES)[None, :] % DIFF_D
    cos_part = jnp.where((d < ROT) & (src == d % ROT_HALF), 1.0, 0.0)
    hi_part = jnp.where((d < ROT_HALF) & (src == ROT_HALF + d), -1.0, 0.0)
    lo_part = jnp.where((d >= ROT_HALF) & (d < ROT) & (src == d), 1.0, 0.0)
    expand = jnp.concatenate([cos_part, hi_part, lo_part], axis=1).astype(BF16)
    keep = jnp.where(d < ROT, 0.0, 1.0).astype(F32)
    return invf, expand, keep


def qk_prep(proj, pos, g_q, g_k, *, q_col, k_col, v_col, tm, tk):
    T = proj.shape[0]
    assert T % tm == 0 and tm % tk == 0 and tk % PREP_ROWS == 0
    assert q_col % DIFF_QK == 0 and k_col % DIFF_QK == 0 and v_col % DIFF_VW == 0 and DIFF_V == LANES
    gq = (jnp.tile(g_q, LANES // DIFF_D) * Q_SCALE).reshape(1, LANES)
    gk = jnp.tile(g_k, LANES // DIFF_D).reshape(1, LANES)
    invf, expand, keep = _rotary_tables()
    return pl.pallas_call(
        functools.partial(_qk_prep_kernel, tk=tk),
        grid=(T // tm,),
        in_specs=[
            pl.BlockSpec((None, 1, tm), lambda i: (i, 0, 0)),
            pl.BlockSpec((2 * ROT_HALF, 1), lambda i: (0, 0)),
            pl.BlockSpec((LANES, 3 * LANES), lambda i: (0, 0)),
            pl.BlockSpec((1, LANES), lambda i: (0, 0)),
            pl.BlockSpec((1, LANES), lambda i: (0, 0)),
            pl.BlockSpec((1, LANES), lambda i: (0, 0)),
            pl.BlockSpec((tm, DIFF_QK), lambda i: (i, q_col // DIFF_QK)),
            pl.BlockSpec((tm, DIFF_QK), lambda i: (i, k_col // DIFF_QK)),
            pl.BlockSpec((tm, DIFF_VW), lambda i: (i, v_col // DIFF_VW)),
        ],
        out_specs=[
            pl.BlockSpec((DIFF_HEADS, 1, LANES, tm), lambda i: (0, i, 0, 0)),
            pl.BlockSpec((tm, DIFF_QK), lambda i: (i, 0)),
            pl.BlockSpec((DIFF_HEADS, tm // tk, V_ROWS, tk), lambda i: (0, i, 0, 0)),
        ],
        out_shape=[
            jax.ShapeDtypeStruct((DIFF_HEADS, T // tm, LANES, tm), BF16),
            jax.ShapeDtypeStruct((T, DIFF_QK), BF16),
            jax.ShapeDtypeStruct((DIFF_HEADS, T // tk, V_ROWS, tk), BF16),
        ],
        compiler_params=_compiler_params(("parallel",)),
        name="qk_prep",
    )(pos, invf, expand, keep, gq, gk, proj, proj, proj)


def _diff_attn_kernel(lam_ref, g_ref, qt_ref, k_ref, vt_ref, o_ref, m_ref, acc_ref,
                      s0_ref, s1_ref, p0_ref, p1_ref, a0_ref, a1_ref, bm0_ref, bm1_ref,
                      *, tq, tk, lam_init):
    assert tq == 2 * tk
    s_bufs, p_bufs, a_bufs, bm_bufs = (s0_ref, s1_ref), (p0_ref, p1_ref), (a0_ref, a1_ref), (bm0_ref, bm1_ref)
    n_tiles = qt_ref.shape[1]

    lv = lam_ref[...]
    e1 = jnp.exp(jnp.sum(lv[0:1, :] * lv[1:2, :], axis=-1, keepdims=True))
    e2 = jnp.exp(jnp.sum(lv[2:3, :] * lv[3:4, :], axis=-1, keepdims=True))
    lam = e1 - e2 + lam_init

    acc_ref[...] = jnp.zeros(acc_ref.shape, F32)

    def scores(i, u, slot, masked, q0=0):
        k = k_ref[pl.ds(pl.multiple_of(u * tk, tk), tk), :]
        qt = qt_ref[0, i, :, q0:]
        nq_cols = tq - q0
        zeros = jnp.zeros((DIFF_D, nq_cols), qt.dtype)
        q_comp = (jnp.concatenate([qt[:DIFF_D], zeros], axis=0),
                  jnp.concatenate([zeros, qt[DIFF_D:]], axis=0))
        if masked:
            kpos = u * tk + lax.broadcasted_iota(jnp.int32, (tk, nq_cols), 0)
            qpos = i * tq + q0 + lax.broadcasted_iota(jnp.int32, (tk, nq_cols), 1)
            causal = kpos <= qpos
        for c in range(2):
            s = jnp.dot(k, q_comp[c], preferred_element_type=F32)
            if masked:
                s = jnp.where(causal, s, -jnp.inf)
            s_bufs[slot][c, :, q0:] = s
            bm_bufs[slot][c, :, q0:] = jnp.max(s, axis=0, keepdims=True)

    def softmax(slot, first=False, q0=0):
        for c in range(2):
            if first:
                m_new = bm_bufs[slot][c]
                a_bufs[slot][c] = jnp.zeros_like(m_new)
            else:
                m_prev = m_ref[c, :, q0:]
                m_new = jnp.maximum(m_prev, bm_bufs[slot][c, :, q0:])
                a_bufs[slot][c, :, q0:] = jnp.exp2(m_prev - m_new)
            m_ref[c, :, q0:] = m_new
            p_bufs[slot][c, :, q0:] = jnp.exp2(s_bufs[slot][c, :, q0:] - m_new).astype(p_bufs[slot].dtype)

    def accumulate(u, slot, q0=0):
        vt = vt_ref[0, u]
        for c in range(2):
            pv = jnp.dot(vt, p_bufs[slot][c, :, q0:], preferred_element_type=F32)
            acc_ref[c, :, q0:] = a_bufs[slot][c, :, q0:] * acc_ref[c, :, q0:] + pv

    def finalize(i):
        o = (acc_ref[0, :DIFF_V] / acc_ref[0, DIFF_V:DIFF_V + 1]
             - lam * (acc_ref[1, :DIFF_V] / acc_ref[1, DIFF_V:DIFF_V + 1]))
        o = o * lax.rsqrt(jnp.mean(o * o, axis=0, keepdims=True) + EPS)
        rows = pl.ds(pl.multiple_of(i * tq, tq), tq)
        o_ref[rows, :] = (o.T * g_ref[...] * (1.0 - lam_init)).astype(o_ref.dtype)

    def step_pair(i, u, masked):
        scores(i, u, 0, masked)
        accumulate(u - 2, 0)
        softmax(1)
        scores(i, u + 1, 1, masked, q0=tk if masked else 0)
        accumulate(u - 1, 1)
        softmax(0)

    scores(0, 0, 0, masked=True)
    scores(0, 1, 1, masked=True, q0=tk)
    softmax(0, first=True)

    def next_tile(i):
        last = 2 * (i - 1)
        scores(i, 0, 0, masked=False)
        accumulate(last, 0)
        softmax(1, q0=tk)
        scores(i, 1, 1, masked=False)
        accumulate(last + 1, 1, q0=tk)
        finalize(i - 1)
        softmax(0, first=True)

    def full_pairs(i):
        def full_pair(r, c):
            step_pair(i, 2 * r, masked=False)
            return c

        lax.fori_loop(1, i, full_pair, 0)

    if n_tiles > 1:
        next_tile(1)

        def q_tile(i, carry):
            full_pairs(i)
            step_pair(i, 2 * i, masked=True)
            next_tile(i + 1)
            return carry

        lax.fori_loop(1, n_tiles - 1, q_tile, 0)
        full_pairs(n_tiles - 1)
        step_pair(n_tiles - 1, 2 * (n_tiles - 1), masked=True)

    last = 2 * (n_tiles - 1)
    accumulate(last, 0)
    softmax(1, q0=tk)
    accumulate(last + 1, 1, q0=tk)
    finalize(n_tiles - 1)


def diff_attn(qt, k, vt, lambda_vecs, g_subln, *, batch, tq, tk, lam_init):
    T = k.shape[0]
    S = T // batch
    assert S % tq == 0 and tq == 2 * tk
    nq, nk = S // tq, S // tk
    stat = pltpu.VMEM((2, 1, tq), F32)
    return pl.pallas_call(
        functools.partial(_diff_attn_kernel, tq=tq, tk=tk, lam_init=lam_init),
        grid=(batch, DIFF_HEADS),
        in_specs=[
            pl.BlockSpec((4, DIFF_D), lambda b, h: (0, 0)),
            pl.BlockSpec((1, DIFF_V), lambda b, h: (0, 0)),
            pl.BlockSpec((1, nq, LANES, tq), lambda b, h: (h, b, 0, 0)),
            pl.BlockSpec((S, LANES), lambda b, h: (b, h)),
            pl.BlockSpec((1, nk, V_ROWS, tk), lambda b, h: (h, b, 0, 0)),
        ],
        out_specs=pl.BlockSpec((S, DIFF_V), lambda b, h: (b, h)),
        out_shape=jax.ShapeDtypeStruct((T, DIFF_VW), BF16),
        scratch_shapes=[
            stat,
            pltpu.VMEM((2, V_ROWS, tq), F32),
            pltpu.VMEM((2, tk, tq), F32),
            pltpu.VMEM((2, tk, tq), F32),
            pltpu.VMEM((2, tk, tq), BF16),
            pltpu.VMEM((2, tk, tq), BF16),
            stat, stat,
            stat, stat,
        ],
        compiler_params=_compiler_params(("parallel", "parallel")),
        name="diff_attn",
    )(lambda_vecs, g_subln.reshape(1, DIFF_V), qt, k, vt)


def _mem_kv_kernel(mem_ref, gm_ref, w_ref, gk_ref, mk_ref, mv_ref):
    h = _rms_rows(mem_ref[...], gm_ref[...]).astype(BF16)
    kv = jnp.dot(h, w_ref[...], preferred_element_type=F32)
    g = gk_ref[...]
    for hd in range(X_HEADS):
        sl = slice(hd * X_HEAD_DIM, (hd + 1) * X_HEAD_DIM)
        kh = _rms_rows(kv[:, sl], g) * (X_HEAD_DIM ** -0.5)
        mk_ref[:, sl] = kh.astype(mk_ref.dtype)
    mv_ref[...] = kv[:, X_WIDTH:].astype(mv_ref.dtype)


def mem_kv(mem, g_mem, w_mem_kv, g_k_x, *, layer):
    R, D = mem.shape
    out = jax.ShapeDtypeStruct((R, X_WIDTH), BF16)
    return pl.pallas_call(
        _mem_kv_kernel,
        grid=(1,),
        in_specs=[
            pl.BlockSpec((R, D), lambda i: (0, 0)),
            pl.BlockSpec((1, D), lambda i: (0, 0)),
            pl.BlockSpec((None, D, 2 * X_WIDTH), lambda i: (layer, 0, 0)),
            pl.BlockSpec((1, X_HEAD_DIM), lambda i: (0, 0)),
        ],
        out_specs=[pl.BlockSpec((R, X_WIDTH), lambda i: (0, 0))] * 2,
        out_shape=[out, out],
        compiler_params=_compiler_params(("arbitrary",)),
        name="mem_kv",
    )(mem, g_mem.reshape(1, D), w_mem_kv, g_k_x.reshape(1, X_HEAD_DIM))


MERGE_CHUNK = 512
GATE_BLOCK = 1024
assert GATE_BLOCK % MERGE_CHUNK == 0


def _merge_kernel(*refs, tiles_per_seq, n_gate_blocks):
    gate_refs = refs[:n_gate_blocks]
    (cb_ref, cc_ref, cx_ref, cch_ref, cxh_ref, xq_ref, ob_ref, x_ref,
     convw_ref, gqx_ref, gn_ref, mk_ref, mv_ref, wa_ref, wb_ref, wc_ref, wo_ref,
     o_ref, hn_ref, ua_ref, oc_ref, mg_ref, yb_ref) = refs[n_gate_blocks:]
    tm = x_ref.shape[0]
    d_model = x_ref.shape[1]
    first = pl.program_id(0) % tiles_per_seq == 0

    def gate(branch, c):
        blk, off = divmod(branch * d_model + c * MERGE_CHUNK, GATE_BLOCK)
        return jax.nn.sigmoid(gate_refs[blk][:, off:off + MERGE_CHUNK].astype(F32))

    ob = ob_ref[...]
    n_chunks = d_model // MERGE_CHUNK

    def diff_branch(c):
        sl = slice(c * MERGE_CHUNK, (c + 1) * MERGE_CHUNK)
        yb_ref[:, sl] = gate(1, c) * jnp.dot(ob, wb_ref[:, sl], preferred_element_type=F32)

    diff_branch(0)

    gq = gqx_ref[...]
    scores = []
    for h in range(X_HEADS):
        sl = slice(h * X_HEAD_DIM, (h + 1) * X_HEAD_DIM)
        qn = _rms_rows(xq_ref[:, sl].astype(F32), gq).astype(BF16)
        scores.append(lax.dot_general(qn, mk_ref[:, sl], (((1,), (1,)), ((), ())),
                                      preferred_element_type=F32))
    diff_branch(1)
    for h, s in enumerate(scores):
        sl = slice(h * X_HEAD_DIM, (h + 1) * X_HEAD_DIM)
        s = s - jnp.max(s, axis=-1, keepdims=True)
        p = jnp.exp(s)
        p = p / jnp.sum(p, axis=-1, keepdims=True)
        oc_ref[:, sl] = jnp.dot(p.astype(BF16), mv_ref[:, sl], preferred_element_type=F32).astype(oc_ref.dtype)
    for c in range(2, n_chunks):
        diff_branch(c)

    z = cc_ref[...].astype(F32) * cx_ref[...].astype(F32)
    zp = cch_ref[...].astype(F32) * cxh_ref[...].astype(F32)
    zp = jnp.where(first, 0.0, zp)
    w = convw_ref[...]
    row8 = lax.broadcasted_iota(jnp.int32, (SUBLANES, 1), 0)
    y = z * w[CONV_K - 1:CONV_K, :]
    for back in range(1, CONV_K):
        zs = pltpu.roll(z, back, axis=0)
        ps = pltpu.roll(zp, back, axis=0)
        head = jnp.where(row8 < back, ps, zs[:SUBLANES])
        zs = jnp.concatenate([head, zs[SUBLANES:]], axis=0)
        y = y + zs * w[CONV_K - 1 - back:CONV_K - back, :]
    ua_ref[...] = (cb_ref[...].astype(F32) * y).astype(ua_ref.dtype)

    ua = ua_ref[...]
    oc = oc_ref[...]
    for c in range(n_chunks):
        sl = slice(c * MERGE_CHUNK, (c + 1) * MERGE_CHUNK)
        merged = yb_ref[:, sl]
        for br, act, w_ref in ((0, ua, wa_ref), (2, oc, wc_ref)):
            merged = merged + gate(br, c) * jnp.dot(act, w_ref[:, sl], preferred_element_type=F32)
        mg_ref[:, sl] = merged.astype(mg_ref.dtype)

    y = x_ref[...] + jnp.dot(mg_ref[...], wo_ref[...], preferred_element_type=F32)
    o_ref[...] = y
    hn_ref[...] = _rms_rows(y, gn_ref[...]).astype(hn_ref.dtype)


def merge(proj, ob, x, conv_w, g_q_x, g_next, mk, mv, w_conv_out, w_diff_out, w_x_out, w_o,
          *, layer, batch, cols, tm):
    T, D = x.shape
    S = T // batch
    M = mk.shape[0] // batch
    assert S % tm == 0 and tm % SUBLANES == 0
    tiles_per_seq = S // tm
    halo_blocks = tm // SUBLANES

    def col_spec(name, width):
        assert cols[name] % width == 0
        return pl.BlockSpec((tm, width), lambda i: (i, cols[name] // width))

    def halo_spec(name):
        return pl.BlockSpec((SUBLANES, CONV_WIDTH),
                            lambda i: (jnp.maximum(i * halo_blocks - 1, 0), cols[name] // CONV_WIDTH))

    def resident(shape):
        return pl.BlockSpec((None,) + shape, lambda i: (layer, 0, 0), pipeline_mode=pl.Buffered(1))

    assert cols["gates"] % GATE_BLOCK == 0 and (N_BRANCH * D) % GATE_BLOCK == 0 and D % MERGE_CHUNK == 0
    n_gate_blocks = N_BRANCH * D // GATE_BLOCK
    gate_specs = [pl.BlockSpec((tm, GATE_BLOCK), lambda i, g=g: (i, cols["gates"] // GATE_BLOCK + g))
                  for g in range(n_gate_blocks)]

    return pl.pallas_call(
        functools.partial(_merge_kernel, tiles_per_seq=tiles_per_seq, n_gate_blocks=n_gate_blocks),
        grid=(T // tm,),
        in_specs=gate_specs + [
            col_spec("cb", CONV_WIDTH),
            col_spec("cc", CONV_WIDTH),
            col_spec("cx", CONV_WIDTH),
            halo_spec("cc"),
            halo_spec("cx"),
            col_spec("xq", X_WIDTH),
            pl.BlockSpec((tm, DIFF_VW), lambda i: (i, 0)),
            pl.BlockSpec((tm, D), lambda i: (i, 0)),
            pl.BlockSpec((CONV_K, CONV_WIDTH), lambda i: (0, 0)),
            pl.BlockSpec((1, X_HEAD_DIM), lambda i: (0, 0)),
            pl.BlockSpec((1, D), lambda i: (0, 0)),
            pl.BlockSpec((M, X_WIDTH), lambda i: (i // tiles_per_seq, 0)),
            pl.BlockSpec((M, X_WIDTH), lambda i: (i // tiles_per_seq, 0)),
            resident((CONV_WIDTH, D)),
            resident((DIFF_VW, D)),
            resident((X_WIDTH, D)),
            resident((D, D)),
        ],
        out_specs=[pl.BlockSpec((tm, D), lambda i: (i, 0))] * 2,
        out_shape=[jax.ShapeDtypeStruct((T, D), F32), jax.ShapeDtypeStruct((T, D), BF16)],
        scratch_shapes=[
            pltpu.VMEM((tm, CONV_WIDTH), BF16),
            pltpu.VMEM((tm, X_WIDTH), BF16),
            pltpu.VMEM((tm, D), BF16),
            pltpu.VMEM((tm, D), F32),
        ],
        compiler_params=_compiler_params(("parallel",)),
        name="merge",
    )(*([proj] * (n_gate_blocks + 6)), ob, x, conv_w, g_q_x.reshape(1, X_HEAD_DIM), g_next.reshape(1, D),
      mk, mv, w_conv_out, w_diff_out, w_x_out, w_o)


def _ffn_up_kernel(h_ref, wg_ref, wu_ref, o_ref, wgb_ref, wub_ref):
    @pl.when(pl.program_id(1) == 0)
    def _():
        wgb_ref[...] = wg_ref[...].astype(wgb_ref.dtype)
        wub_ref[...] = wu_ref[...].astype(wub_ref.dtype)

    for r0 in range(0, h_ref.shape[0], MXU_ROW_CHUNK):
        h = h_ref[r0:r0 + MXU_ROW_CHUNK, :]
        a = jnp.dot(h, wgb_ref[...], preferred_element_type=F32)
        b = jnp.dot(h, wub_ref[...], preferred_element_type=F32)
        o_ref[r0:r0 + MXU_ROW_CHUNK, :] = (a * jax.nn.sigmoid(a) * b).astype(o_ref.dtype)


def ffn_up(h, w_gate_up, *, layer, tm, tf):
    T, D = h.shape
    d_ff = w_gate_up.shape[2] // 2
    assert T % tm == 0 and d_ff % tf == 0
    nf = d_ff // tf
    return pl.pallas_call(
        _ffn_up_kernel,
        grid=(nf, T // tm),
        in_specs=[
            pl.BlockSpec((tm, D), lambda f, i: (i, 0)),
            pl.BlockSpec((None, D, tf), lambda f, i: (layer, 0, f)),
            pl.BlockSpec((None, D, tf), lambda f, i: (layer, 0, nf + f)),
        ],
        out_specs=pl.BlockSpec((tm, tf), lambda f, i: (i, f)),
        out_shape=jax.ShapeDtypeStruct((T, d_ff), BF16),
        scratch_shapes=[pltpu.VMEM((D, tf), BF16), pltpu.VMEM((D, tf), BF16)],
        compiler_params=_compiler_params(("parallel", "arbitrary")),
        name="ffn_up",
    )(h, w_gate_up, w_gate_up)


def _ffn_down_kernel(a_ref, x_ref, wd_ref, *rest, normed_out):
    if normed_out:
        gn_ref, o_ref, hn_ref = rest
    else:
        (o_ref,) = rest
    y = x_ref[...] + jnp.dot(a_ref[...], wd_ref[...], preferred_element_type=F32)
    o_ref[...] = y
    if normed_out:
        hn_ref[...] = _rms_rows(y, gn_ref[...]).astype(hn_ref.dtype)


def ffn_down(act, x, w_down, *, layer, tm, g_next=None):
    T, D = x.shape
    d_ff = act.shape[1]
    assert T % tm == 0
    normed_out = g_next is not None
    row_spec = pl.BlockSpec((tm, D), lambda i: (i, 0))
    in_specs = [
        pl.BlockSpec((tm, d_ff), lambda i: (i, 0)),
        row_spec,
        pl.BlockSpec((None, d_ff, D), lambda i: (layer, 0, 0), pipeline_mode=pl.Buffered(1)),
    ]
    args = [act, x, w_down]
    out_specs, out_shape = row_spec, jax.ShapeDtypeStruct((T, D), F32)
    if normed_out:
        in_specs.append(pl.BlockSpec((1, D), lambda i: (0, 0)))
        args.append(g_next.reshape(1, D))
        out_specs = [row_spec, row_spec]
        out_shape = [out_shape, jax.ShapeDtypeStruct((T, D), BF16)]
    return pl.pallas_call(
        functools.partial(_ffn_down_kernel, normed_out=normed_out),
        grid=(T // tm,),
        in_specs=in_specs,
        out_specs=out_specs,
        out_shape=out_shape,
        compiler_params=_compiler_params(("parallel",)),
        name="ffn_down",
    )(*args)


def _tile(n, want):
    t = min(n, want)
    while n % t:
        t -= SUBLANES
    return t


def kernel(x, mem, positions, g_mix, w_in, conv_w, w_conv_out, g_q_diff, g_k_diff, lambda_vecs,
           g_subln, w_diff_out, g_mem, w_mem_kv, g_q_x, g_k_x, w_x_out, w_o, g_ffn, w_gate_up, w_down):
    B, S, D = x.shape
    M = mem.shape[1]
    T = B * S
    depth = w_in.shape[0]
    d_ff = w_down.shape[1]

    n_conv = 3 * CONV_WIDTH
    n_diff = 2 * DIFF_QK + DIFF_VW
    cols = {
        "cb": 0,
        "cc": CONV_WIDTH,
        "cx": 2 * CONV_WIDTH,
        "dq": n_conv,
        "dk": n_conv + DIFF_QK,
        "dv": n_conv + 2 * DIFF_QK,
        "xq": n_conv + n_diff,
        "gates": n_conv + n_diff + X_WIDTH,
    }

    xf = x.reshape(T, D)
    memf = mem.reshape(B * M, D)

    tm_proj = _tile(T, 2048)
    tq = _tile(S, 1024)
    tk = tq // 2
    pos = positions.reshape(T // tq, 1, tq)
    tm_merge = _tile(S, 256)
    tm_up = _tile(T, 2048)
    tf = _tile(d_ff, 512)
    tm_down = _tile(T, 256)

    w_mem_kv, w_conv_out, w_diff_out, w_x_out, w_o, w_down = (
        w.astype(BF16) for w in (w_mem_kv, w_conv_out, w_diff_out, w_x_out, w_o, w_down))

    h = rms_norm_rows(xf, g_mix[0], tm=_tile(T, 1024))
    for l in range(depth):
        lam_init = 0.8 - 0.6 * math.exp(-0.3 * l)

        proj = in_proj(h, w_in, layer=l, tm=tm_proj, tn=1024)
        qt, k, vt = qk_prep(proj, pos, g_q_diff[l], g_k_diff[l],
                            q_col=cols["dq"], k_col=cols["dk"], v_col=cols["dv"], tm=tq, tk=tk)
        ob = diff_attn(qt, k, vt, lambda_vecs[l], g_subln[l], batch=B, tq=tq, tk=tk, lam_init=lam_init)

        mk, mv = mem_kv(memf, g_mem[l], w_mem_kv, g_k_x[l], layer=l)

        xf, h2 = merge(proj, ob, xf, conv_w[l], g_q_x[l], g_ffn[l], mk, mv,
                       w_conv_out, w_diff_out, w_x_out, w_o, layer=l, batch=B, cols=cols, tm=tm_merge)
        act = ffn_up(h2, w_gate_up, layer=l, tm=tm_up, tf=tf)
        if l + 1 < depth:
            xf, h = ffn_down(act, xf, w_down, layer=l, tm=tm_down, g_next=g_mix[l + 1])
        else:
            xf = ffn_down(act, xf, w_down, layer=l, tm=tm_down)

    return xf.reshape(B, S, D)
```
